```python
import math
import jax, jax.numpy as jnp
from jax import lax
import numpy as np

D_MODEL = 1024
BATCH = 8
SEQ = 2048
DEPTH = 2

GRID_W = 64
BRANCH_W = 512
N_BRANCH = 3
EPS = 1e-6
SGU_CHUNK = 128
SGU_GROUPS = 4
SGU_GROUP_W = BRANCH_W // SGU_GROUPS
ATT_HEADS = 4
ATT_KV_HEADS = 2
ATT_HEAD_DIM = BRANCH_W // ATT_HEADS
ATT_BLOCK = 128
ROPE_THETA = 10000.0
LSTM_HEADS = 4
LSTM_HEAD_DIM = BRANCH_W // LSTM_HEADS
LSTM_CHUNK = 128
CONV_K = 5
FFN_HIDDEN = -(-8 * D_MODEL // (3 * 256)) * 256
SEG_SIZES = (
    BRANCH_W, BRANCH_W,
    ATT_HEADS * ATT_HEAD_DIM, ATT_KV_HEADS * ATT_HEAD_DIM, ATT_KV_HEADS * ATT_HEAD_DIM,
    BRANCH_W, BRANCH_W, BRANCH_W, BRANCH_W,
    2 * LSTM_HEADS, 2 * LSTM_HEADS,
)
N_IN = sum(SEG_SIZES)

kernel_name = "hybrid_sgu_gqa_mlstm_encoder"


def rmsnorm(x, g):
    xf = x.astype(jnp.float32)
    y = xf * lax.rsqrt(jnp.mean(xf * xf, axis=-1, keepdims=True) + EPS)
    return (y * g.astype(jnp.float32)).astype(x.dtype)


def layernorm(x, g, b):
    xf = x.astype(jnp.float32)
    mu = jnp.mean(xf, axis=-1, keepdims=True)
    xc = xf - mu
    y = xc * lax.rsqrt(jnp.mean(xc * xc, axis=-1, keepdims=True) + EPS)
    return (y * g.astype(jnp.float32) + b.astype(jnp.float32)).astype(x.dtype)


def axial_rope_tables(S):
    rows = S // GRID_W
    row = jnp.repeat(jnp.arange(rows, dtype=jnp.float32), GRID_W)
    col = jnp.tile(jnp.arange(GRID_W, dtype=jnp.float32), rows)
    axis_dim = ATT_HEAD_DIM // 2
    freqs = ROPE_THETA ** (-jnp.arange(axis_dim // 2, dtype=jnp.float32) * 2.0 / axis_dim)
    ang = jnp.concatenate([row[:, None] * freqs[None], col[:, None] * freqs[None]], axis=-1)
    return jnp.cos(ang), jnp.sin(ang)


def apply_rope(x, cos, sin):
    xp = x.reshape(*x.shape[:-1], x.shape[-1] // 2, 2)
    x0, x1 = xp[..., 0], xp[..., 1]
    c = cos[None, :, None, :].astype(x.dtype)
    s = sin[None, :, None, :].astype(x.dtype)
    out = jnp.stack([x0 * c - x1 * s, x0 * s + x1 * c], axis=-1)
    return out.reshape(x.shape)


def sgu_branch(su, sv, ln_g, ln_b, w_s, b_s):
    B_, S, _ = su.shape
    u = jax.nn.gelu(su)
    v = layernorm(jax.nn.gelu(sv), ln_g, ln_b)
    nc = S // SGU_CHUNK
    vr = v.reshape(B_, nc, SGU_CHUNK, SGU_GROUPS, SGU_GROUP_W)
    s = jnp.einsum('gts,bnsgc->bntgc', w_s, vr) + b_s.T[None, None, :, :, None]
    return u * s.reshape(B_, S, BRANCH_W)


def gqa_branch(aq, ak, av, q_norm, k_norm, cos, sin):
    B_, S, _ = aq.shape
    rep = ATT_HEADS // ATT_KV_HEADS
    q = apply_rope(rmsnorm(aq.reshape(B_, S, ATT_HEADS, ATT_HEAD_DIM), q_norm), cos, sin)
    k = apply_rope(rmsnorm(ak.reshape(B_, S, ATT_KV_HEADS, ATT_HEAD_DIM), k_norm), cos, sin)
    v = av.reshape(B_, S, ATT_KV_HEADS, ATT_HEAD_DIM)
    nb = S // ATT_BLOCK
    q = q.reshape(B_, nb, ATT_BLOCK, ATT_KV_HEADS, rep, ATT_HEAD_DIM)
    qb = jnp.moveaxis(q, 1, 0)
    scale = ATT_HEAD_DIM ** -0.5

    def attend(qblk):
        s = jnp.einsum('bqgrd,bkgd->bgrqk', qblk, k).astype(jnp.float32) * scale
        p = jax.nn.softmax(s, axis=-1).astype(v.dtype)
        return jnp.einsum('bgrqk,bkgd->bqgrd', p, v)

    o = lax.map(attend, qb)
    return jnp.moveaxis(o, 0, 1).reshape(B_, S, BRANCH_W)


def centred_depthwise_conv(x, w, b):
    C = x.shape[-1]
    y = lax.conv_general_dilated(x, w[:, None, :], window_strides=(1,),
                                 padding=[(CONV_K // 2, CONV_K // 2)],
                                 dimension_numbers=('NWC', 'WIO', 'NWC'),
                                 feature_group_count=C)
    return y + b


def mlstm_scan(q, k, v, ig, lf):
    B_, H, S, dk = q.shape
    dv = v.shape[-1]
    L = LSTM_CHUNK
    nc = S // L

    def to_chunks(a):
        return jnp.moveaxis(a.reshape(B_, H, nc, L, *a.shape[3:]), 2, 0)

    tri = jnp.tril(jnp.ones((L, L), dtype=bool))

    def step(carry, xs):
        C, n, m = carry
        qc, kc, vc, ic, fc = xs
        b = jnp.cumsum(fc, axis=-1)
        dmat = jnp.where(tri, b[..., :, None] - b[..., None, :] + ic[..., None, :], -jnp.inf)
        m_inter = b + m[..., None]
        m_t = jnp.maximum(m_inter, jnp.max(dmat, axis=-1))
        w_inter = jnp.exp(m_inter - m_t)
        w_intra = jnp.exp(dmat - m_t[..., None])
        s = jnp.einsum('bhtd,bhsd->bhts', qc, kc) * w_intra
        num = w_inter[..., None] * jnp.einsum('bhtd,bhde->bhte', qc, C) + jnp.einsum('bhts,bhse->bhte', s, vc)
        den = w_inter * jnp.einsum('bhtd,bhd->bht', qc, n) + jnp.sum(s, axis=-1)
        h = num / jnp.maximum(jnp.abs(den), jnp.exp(-m_t))[..., None]
        m_new = m_t[..., -1]
        decay = jnp.exp(b[..., -1] + m - m_new)
        wk = jnp.exp(b[..., -1:] - b + ic - m_new[..., None])
        C_new = decay[..., None, None] * C + jnp.einsum('bhs,bhsd,bhse->bhde', wk, kc, vc)
        n_new = decay[..., None] * n + jnp.einsum('bhs,bhsd->bhd', wk, kc)
        return (C_new, n_new, m_new), h

    init = (jnp.zeros((B_, H, dk, dv), jnp.float32),
            jnp.zeros((B_, H, dk), jnp.float32),
            jnp.zeros((B_, H), jnp.float32))
    _, h = lax.scan(step, init, (to_chunks(q), to_chunks(k), to_chunks(v), to_chunks(ig), to_chunks(lf)))
    return jnp.moveaxis(h, 0, 2).reshape(B_, H, S, dv)


def mlstm_branch(lq, lk, lv, lo, li, lf, conv_w, conv_b, igate_b, fgate_b, lstm_norm):
    B_, S, _ = lq.shape
    qk = jax.nn.silu(centred_depthwise_conv(jnp.concatenate([lq, lk], axis=-1), conv_w, conv_b))
    cq, ck = qk[..., :BRANCH_W], qk[..., BRANCH_W:]

    def heads(a):
        return a.reshape(B_, S, LSTM_HEADS, LSTM_HEAD_DIM).transpose(0, 2, 1, 3).astype(jnp.float32)

    q = heads(cq)
    k = heads(ck) * (LSTM_HEAD_DIM ** -0.5)
    v = heads(lv)
    ig = (li.astype(jnp.float32).reshape(B_, S, 2, LSTM_HEADS) + igate_b.astype(jnp.float32)).transpose(2, 0, 3, 1)
    lfg = jax.nn.log_sigmoid(lf.astype(jnp.float32).reshape(B_, S, 2, LSTM_HEADS)
                             + fgate_b.astype(jnp.float32)).transpose(2, 0, 3, 1)
    flip = lambda a: jnp.flip(a, axis=2)
    h_fwd = mlstm_scan(q, k, v, ig[0], lfg[0])
    h_bwd = flip(mlstm_scan(flip(q), flip(k), flip(v), flip(ig[1]), flip(lfg[1])))
    h = (h_fwd + h_bwd).transpose(0, 2, 1, 3)
    h = rmsnorm(h, lstm_norm.reshape(LSTM_HEADS, LSTM_HEAD_DIM))
    return h.reshape(B_, S, BRANCH_W).astype(lq.dtype) * jax.nn.sigmoid(lo)


def hybrid_mixer(xn, cos, sin, w_in, sgu_ln_g, sgu_ln_b, sgu_w, sgu_b, q_norm, k_norm,
                 conv_w, conv_b, igate_b, fgate_b, lstm_norm, w_gate, b_gate, w_branch, w_out):
    B_, S, _ = xn.shape
    points = np.cumsum(SEG_SIZES)[:-1].tolist()
    z = xn @ w_in
    su, sv, aq, ak, av, lq, lk, lv, lo, li, lf = jnp.split(z, points, axis=-1)
    y_a = sgu_branch(su, sv, sgu_ln_g, sgu_ln_b, sgu_w, sgu_b)
    y_b = gqa_branch(aq, ak, av, q_norm, k_norm, cos, sin)
    y_c = mlstm_branch(lq, lk, lv, lo, li, lf, conv_w, conv_b, igate_b, fgate_b, lstm_norm)
    ys = jnp.stack([y_a, y_b, y_c], axis=2)
    proj = jnp.einsum('bsnc,ncd->bsnd', ys, w_branch)
    gates = jax.nn.sigmoid((xn @ w_gate + b_gate).reshape(B_, S, N_BRANCH, D_MODEL))
    return jnp.sum(gates * proj, axis=2) @ w_out


def swiglu(x, w_in, w_out):
    h = x @ w_in
    return (jax.nn.silu(h[..., :FFN_HIDDEN]) * h[..., FFN_HIDDEN:]) @ w_out


def setup_inputs(seed: int = 0) -> dict:
    key = jax.random.key(seed)
    ks = jax.random.split(key, 24)
    f32 = jnp.float32
    nrm = lambda k, shape, scale: jax.random.normal(k, shape, f32) * scale
    gain = lambda k, shape: 1.0 + 0.02 * jax.random.normal(k, shape, f32)
    fbias = jnp.linspace(3.0, 6.0, LSTM_HEADS, dtype=f32)[None, None, :] + nrm(ks[13], (DEPTH, 2, LSTM_HEADS), 0.1)
    return {
        "x": nrm(ks[0], (BATCH, SEQ, D_MODEL), 1.0),
        "norm_mix": gain(ks[1], (DEPTH, D_MODEL)),
        "w_in": nrm(ks[2], (DEPTH, D_MODEL, N_IN), D_MODEL ** -0.5),
        "sgu_ln_g": gain(ks[3], (DEPTH, BRANCH_W)),
        "sgu_ln_b": nrm(ks[4], (DEPTH, BRANCH_W), 0.02),
        "sgu_w": nrm(ks[5], (DEPTH, SGU_GROUPS, SGU_CHUNK, SGU_CHUNK), SGU_CHUNK ** -0.5),
        "sgu_b": gain(ks[6], (DEPTH, SGU_GROUPS, SGU_CHUNK)),
        "q_norm": gain(ks[7], (DEPTH, ATT_HEAD_DIM)),
        "k_norm": gain(ks[8], (DEPTH, ATT_HEAD_DIM)),
        "conv_w": nrm(ks[9], (DEPTH, CONV_K, 2 * BRANCH_W), CONV_K ** -0.5),
        "conv_b": nrm(ks[10], (DEPTH, 2 * BRANCH_W), 0.02),
        "igate_b": nrm(ks[11], (DEPTH, 2, LSTM_HEADS), 0.1),
        "fgate_b": fbias,
        "lstm_norm": gain(ks[12], (DEPTH, BRANCH_W)),
        "w_gate": nrm(ks[14], (DEPTH, D_MODEL, N_BRANCH * D_MODEL), D_MODEL ** -0.5),
        "b_gate": nrm(ks[15], (DEPTH, N_BRANCH * D_MODEL), 0.02),
        "w_branch": nrm(ks[16], (DEPTH, N_BRANCH, BRANCH_W, D_MODEL), BRANCH_W ** -0.5),
        "w_out": nrm(ks[17], (DEPTH, D_MODEL, D_MODEL), D_MODEL ** -0.5),
        "norm_ffn": gain(ks[18], (DEPTH, D_MODEL)),
        "w_ffn_in": nrm(ks[19], (DEPTH, D_MODEL, 2 * FFN_HIDDEN), D_MODEL ** -0.5),
        "w_ffn_out": nrm(ks[20], (DEPTH, FFN_HIDDEN, D_MODEL), FFN_HIDDEN ** -0.5),
    }


def reference(x, norm_mix, w_in, sgu_ln_g, sgu_ln_b, sgu_w, sgu_b, q_norm, k_norm, conv_w, conv_b,
              igate_b, fgate_b, lstm_norm, w_gate, b_gate, w_branch, w_out, norm_ffn, w_ffn_in, w_ffn_out):
    cos, sin = axial_rope_tables(x.shape[1])
    for l in range(DEPTH):
        xn = rmsnorm(x, norm_mix[l])
        x = x + hybrid_mixer(xn, cos, sin, w_in[l], sgu_ln_g[l], sgu_ln_b[l], sgu_w[l], sgu_b[l],
                             q_norm[l], k_norm[l], conv_w[l], conv_b[l], igate_b[l], fgate_b[l],
                             lstm_norm[l], w_gate[l], b_gate[l], w_branch[l], w_out[l])
        x = x + swiglu(rmsnorm(x, norm_ffn[l]), w_ffn_in[l], w_ffn_out[l])
    return x
```

```python
import functools

import jax
import jax.numpy as jnp
from jax import lax
from jax.experimental import pallas as pl
from jax.experimental.pallas import tpu as pltpu

D_MODEL = 1024
GRID_W = 64
BRANCH_W = 512
N_BRANCH = 3
EPS = 1e-6
SGU_CHUNK = 128
SGU_GROUPS = 4
ATT_HEADS = 4
ATT_KV_HEADS = 2
HEAD_DIM = 128
ROPE_THETA = 10000.0
LSTM_HEADS = 4
LSTM_CHUNK = 128
CONV_K = 5
FFN_HIDDEN = 2816
N_MAIN = 4096
N_GATE_COLS = 2 * 2 * LSTM_HEADS

LANES = 128
SUBLANES = 8
VMEM_LIMIT = 56 * 1024 * 1024

TOK_TILE = 512
ATT_Q_TILE = 256
CONV_ROWS = 256
CONV_PAD = SUBLANES

BF16 = jnp.bfloat16
F32 = jnp.float32


def _rms(x, g):
    return x * lax.rsqrt(jnp.mean(x * x, axis=-1, keepdims=True) + EPS) * g


def _dot(a, b):
    return jnp.dot(a, b, preferred_element_type=F32)


def _dot_nt(a, b):
    return lax.dot_general(a, b, (((1,), (1,)), ((), ())), preferred_element_type=F32)


def _dot_tn(a, b):
    return lax.dot_general(a, b, (((0,), (0,)), ((), ())), preferred_element_type=F32)


def _inproj_kernel(x_ref, g_ref, w_ref, wgc_ref, wgr_ref, lng_ref, lnb_ref, sguw_ref, sgub_ref,
                   qn_ref, kn_ref, cos_ref, sin_ref,
                   ya_ref, qa_ref, ka_ref, va_ref, lq_ref, lk_ref, lv_ref, lo_ref, gcol_ref, grow_ref):
    t = x_ref.shape[0]
    xn = _rms(x_ref[...], g_ref[...]).astype(BF16)

    def seg(lo, hi):
        return _dot(xn, w_ref[:, lo:hi])

    u = jax.nn.gelu(seg(0, 512))
    gv = jax.nn.gelu(seg(512, 1024))
    mu = jnp.mean(gv, axis=-1, keepdims=True)
    vc = gv - mu
    v = vc * lax.rsqrt(jnp.mean(vc * vc, axis=-1, keepdims=True) + EPS) * lng_ref[...] + lnb_ref[...]
    vb = v.astype(BF16)
    for j in range(t // SGU_CHUNK):
        r0 = j * SGU_CHUNK
        for grp in range(SGU_GROUPS):
            c0 = grp * LANES
            s = _dot(sguw_ref[grp], vb[r0:r0 + SGU_CHUNK, c0:c0 + LANES])
            s = s + sgub_ref[:, c0:c0 + LANES]
            ya_ref[r0:r0 + SGU_CHUNK, c0:c0 + LANES] = (u[r0:r0 + SGU_CHUNK, c0:c0 + LANES] * s).astype(BF16)

    cos = cos_ref[...]
    sin = sin_ref[...]
    even = (lax.broadcasted_iota(jnp.int32, (t, LANES), 1) % 2) == 0

    def norm_rope(xh, gain):
        xh = _rms(xh, gain)
        partner = jnp.where(even, pltpu.roll(xh, LANES - 1, 1), pltpu.roll(xh, 1, 1))
        return (xh * cos + partner * sin).astype(BF16)

    aq = seg(1024, 1536)
    for h in range(ATT_HEADS):
        qa_ref[:, h * LANES:(h + 1) * LANES] = norm_rope(aq[:, h * LANES:(h + 1) * LANES], qn_ref[...])
    ak = seg(1536, 1792)
    for h in range(ATT_KV_HEADS):
        ka_ref[:, h * LANES:(h + 1) * LANES] = norm_rope(ak[:, h * LANES:(h + 1) * LANES], kn_ref[...])
    va_ref[...] = seg(1792, 2048).astype(BF16)

    lq_ref[...] = seg(2048, 2560)
    lk_ref[...] = seg(2560, 3072)
    lv_ref[...] = seg(3072, 3584).astype(BF16)
    lo_ref[...] = seg(3584, 4096)
    gcol_ref[...] = _dot(xn, wgc_ref[...])
    grow_ref[...] = _dot_nt(wgr_ref[...], xn)


def _inproj(x2, g, w_main, w_gc, w_gr, lng, lnb, sguw, sgub, qn, kn, cos, sin, seq):
    n = x2.shape[0]
    t = TOK_TILE
    tiles_per_seq = seq // t
    const = lambda *shape: pl.BlockSpec(shape, lambda i: (0,) * len(shape))
    tok = lambda w: pl.BlockSpec((t, w), lambda i: (i, 0))
    rope = pl.BlockSpec((t, LANES), lambda i: (i % tiles_per_seq, 0))
    out_shapes = (
        jax.ShapeDtypeStruct((n, BRANCH_W), BF16),
        jax.ShapeDtypeStruct((n, 512), BF16),
        jax.ShapeDtypeStruct((n, 256), BF16),
        jax.ShapeDtypeStruct((n, 256), BF16),
        jax.ShapeDtypeStruct((n, BRANCH_W), F32),
        jax.ShapeDtypeStruct((n, BRANCH_W), F32),
        jax.ShapeDtypeStruct((n, BRANCH_W), BF16),
        jax.ShapeDtypeStruct((n, BRANCH_W), F32),
        jax.ShapeDtypeStruct((n, N_GATE_COLS), F32),
        jax.ShapeDtypeStruct((N_GATE_COLS, n), F32),
    )
    out_specs = (tok(512), tok(512), tok(256), tok(256), tok(512), tok(512), tok(512), tok(512),
                 tok(N_GATE_COLS), pl.BlockSpec((N_GATE_COLS, t), lambda i: (0, i)))
    return pl.pallas_call(
        _inproj_kernel,
        grid=(n // t,),
        in_specs=[tok(D_MODEL), const(1, D_MODEL), const(D_MODEL, N_MAIN), const(D_MODEL, N_GATE_COLS),
                  const(N_GATE_COLS, D_MODEL), const(1, BRANCH_W), const(1, BRANCH_W),
                  const(SGU_GROUPS, SGU_CHUNK, SGU_CHUNK), const(SGU_CHUNK, BRANCH_W),
                  const(1, LANES), const(1, LANES), rope, rope],
        out_specs=out_specs,
        out_shape=out_shapes,
        compiler_params=pltpu.CompilerParams(dimension_semantics=("arbitrary",),
                                             vmem_limit_bytes=VMEM_LIMIT),
        name="inproj",
    )(x2, g, w_main, w_gc, w_gr, lng, lnb, sguw, sgub, qn, kn, cos, sin)


def _attn_kernel(q_ref, k_ref, v_ref, o_ref):
    scale = HEAD_DIM ** -0.5
    rep = ATT_HEADS // ATT_KV_HEADS
    for grp in range(ATT_KV_HEADS):
        kg = k_ref[:, grp * LANES:(grp + 1) * LANES]
        vg = v_ref[:, grp * LANES:(grp + 1) * LANES]
        for r in range(rep):
            c0 = (grp * rep + r) * LANES
            s = _dot_nt(q_ref[:, c0:c0 + LANES], kg) * scale
            p = jnp.exp(s - jnp.max(s, axis=-1, keepdims=True))
            denom = jnp.sum(p, axis=-1, keepdims=True)
            o = _dot(p.astype(BF16), vg)
            o_ref[:, c0:c0 + LANES] = (o / denom).astype(BF16)


def _attention(qa, ka, va, batch, seq):
    n = qa.shape[0]
    tq = ATT_Q_TILE
    qpb = seq // tq
    return pl.pallas_call(
        _attn_kernel,
        grid=(batch, qpb),
        in_specs=[pl.BlockSpec((tq, 512), lambda b, i: (b * qpb + i, 0)),
                  pl.BlockSpec((seq, 256), lambda b, i: (b, 0)),
                  pl.BlockSpec((seq, 256), lambda b, i: (b, 0))],
        out_specs=pl.BlockSpec((tq, 512), lambda b, i: (b * qpb + i, 0)),
        out_shape=jax.ShapeDtypeStruct((n, BRANCH_W), BF16),
        compiler_params=pltpu.CompilerParams(dimension_semantics=("arbitrary", "arbitrary"),
                                             vmem_limit_bytes=VMEM_LIMIT),
        name="gqa",
    )(qa, ka, va)


def _log_sigmoid(x):
    return jnp.minimum(x, 0.0) - jnp.log1p(jnp.exp(-jnp.abs(x)))


def _lane_cumsum(x, reverse):
    lane = lax.broadcasted_iota(jnp.int32, x.shape, 1)
    k = 1
    while k < LANES:
        if reverse:
            x = x + jnp.where(lane < LANES - k, pltpu.roll(x, LANES - k, 1), 0.0)
        else:
            x = x + jnp.where(lane >= k, pltpu.roll(x, k, 1), 0.0)
        k *= 2
    return x


def _mlstm_kernel(lq_ref, lk_ref, lv_ref, lo_ref, gcol_ref, grow_ref, cw_q_ref, cw_k_ref, cb_q_ref, cb_k_ref,
                  bcol_ref, brow_ref, norm_ref, o_ref, xp_ref, q_s, k_s, hf_s, hb_s):
    seq = lq_ref.shape[0]
    L = LSTM_CHUNK
    nc = seq // L

    zpad = jnp.zeros((CONV_PAD, LANES), F32)
    xp_ref[0:CONV_PAD, :] = zpad
    xp_ref[CONV_PAD + seq:CONV_PAD + seq + CONV_PAD, :] = zpad

    def conv_silu(src_ref, w_ref, b_ref, dst_ref, post):
        xp_ref[CONV_PAD:CONV_PAD + seq, :] = src_ref[...]
        for r0 in range(0, seq, CONV_ROWS):
            acc = jnp.zeros((CONV_ROWS, LANES), F32) + b_ref[0]
            for j in range(CONV_K):
                start = CONV_PAD + r0 + j - CONV_K // 2
                acc = acc + xp_ref[start:start + CONV_ROWS, :] * w_ref[0, j:j + 1, :]
            dst_ref[r0:r0 + CONV_ROWS, :] = post(jax.nn.silu(acc))

    conv_silu(lq_ref, cw_q_ref, cb_q_ref, q_s, lambda a: a.astype(BF16))
    conv_silu(lk_ref, cw_k_ref, cb_k_ref, k_s, lambda a: a * (HEAD_DIM ** -0.5))

    brow = brow_ref[0]
    ig_rows = (grow_ref[0, 0, 0] + brow[0:1, :], grow_ref[0, 0, 1] + brow[1:2, :])
    lf_rows = (_log_sigmoid(grow_ref[0, 0, 2] + brow[2:3, :]), _log_sigmoid(grow_ref[0, 0, 3] + brow[3:4, :]))
    b_rows = (_lane_cumsum(lf_rows[0], False), _lane_cumsum(lf_rows[1], True))
    pad = jnp.zeros((LANES - nc, LANES), F32)
    b_cols = tuple(jnp.concatenate([b, pad], axis=0).T for b in b_rows)

    row_i = lax.broadcasted_iota(jnp.int32, (L, L), 0)
    col_i = lax.broadcasted_iota(jnp.int32, (L, L), 1)
    masks = (col_i <= row_i, col_i >= row_i)
    ones = jnp.ones((L, LANES), BF16)
    bcol = bcol_ref[0]

    def chunk_step(d, c, state):
        caug, m = state
        r0 = c * L
        last = L - 1 if d == 0 else 0
        b_row = b_rows[d][c:c + 1, :]
        ic_row = ig_rows[d][c:c + 1, :]
        b_col = b_cols[d][:, c:c + 1]
        ic_col = gcol_ref[0, 0, r0:r0 + L, d:d + 1] + bcol[:, d:d + 1]
        qc = q_s[r0:r0 + L, :]
        kc = k_s[r0:r0 + L, :]
        vaug = jnp.concatenate([lv_ref[r0:r0 + L, :], ones], axis=1)

        qk = _dot_nt(qc, kc.astype(BF16))
        dm = jnp.where(masks[d], b_col - b_row + ic_row, -jnp.inf)
        m_inter = b_col + m
        m_t = jnp.maximum(m_inter, jnp.max(dm, axis=-1, keepdims=True))
        w_inter = jnp.exp(m_inter - m_t)
        s = (qk * jnp.exp(dm - m_t)).astype(BF16)
        na = w_inter * _dot(qc, caug.astype(BF16)) + _dot(s, vaug)
        h = na[:, :LANES] / jnp.maximum(jnp.abs(na[:, LANES:]), jnp.exp(-m_t))

        m_new = m_t[last:last + 1, :]
        b_last = b_row[:, last:last + 1]
        decay = jnp.exp(b_last + m - m_new)
        wk = jnp.exp(b_last - b_col + ic_col - m_new)
        caug = decay * caug + _dot_tn((kc * wk).astype(BF16), vaug)
        return h, (caug, m_new)

    init = (jnp.zeros((HEAD_DIM, 2 * LANES), F32), jnp.zeros((1, 1), F32))
    st_f, st_b = init, init
    for c in range(nc):
        cb = nc - 1 - c
        h, st_f = chunk_step(0, c, st_f)
        hf_s[c * L:(c + 1) * L, :] = h
        h, st_b = chunk_step(1, cb, st_b)
        hb_s[cb * L:(cb + 1) * L, :] = h

    h = _rms(hf_s[...] + hb_s[...], norm_ref[0])
    o_ref[...] = (h * jax.nn.sigmoid(lo_ref[...])).astype(BF16)


def _mlstm(lq, lk, lv, lo, gcol4, grow4, cw, cb, bias_col, bias_row, norm, batch, seq):
    n = lq.shape[0]
    nh = LSTM_HEADS
    nc = seq // LSTM_CHUNK
    head_blk = lambda: pl.BlockSpec((seq, LANES), lambda b, h: (b, h))
    return pl.pallas_call(
        _mlstm_kernel,
        grid=(batch, nh),
        in_specs=[head_blk(), head_blk(), head_blk(), head_blk(),
                  pl.BlockSpec((1, 1, seq, 4), lambda b, h: (b, h, 0, 0)),
                  pl.BlockSpec((1, 1, 4, nc, LSTM_CHUNK), lambda b, h: (b, h, 0, 0, 0)),
                  pl.BlockSpec((1, CONV_K, LANES), lambda b, h: (h, 0, 0)),
                  pl.BlockSpec((1, CONV_K, LANES), lambda b, h: (nh + h, 0, 0)),
                  pl.BlockSpec((1, 1, LANES), lambda b, h: (h, 0, 0)),
                  pl.BlockSpec((1, 1, LANES), lambda b, h: (nh + h, 0, 0)),
                  pl.BlockSpec((1, 1, 4), lambda b, h: (h, 0, 0)),
                  pl.BlockSpec((1, 4, 1), lambda b, h: (h, 0, 0)),
                  pl.BlockSpec((1, 1, LANES), lambda b, h: (h, 0, 0))],
        out_specs=head_blk(),
        out_shape=jax.ShapeDtypeStruct((n, BRANCH_W), BF16),
        scratch_shapes=[pltpu.VMEM((seq + 2 * CONV_PAD, LANES), F32),
                        pltpu.VMEM((seq, LANES), BF16),
                        pltpu.VMEM((seq, LANES), F32),
                        pltpu.VMEM((seq, LANES), F32),
                        pltpu.VMEM((seq, LANES), F32)],
        compiler_params=pltpu.CompilerParams(dimension_semantics=("arbitrary", "arbitrary"),
                                             vmem_limit_bytes=VMEM_LIMIT),
        name="mlstm",
    )(lq, lk, lv, lo, gcol4, grow4, cw, cw, cb, cb, bias_col, bias_row, norm)


def _combine_kernel(x_ref, ya_ref, yb_ref, yc_ref, g_ref, wg_ref, bg_ref, wb_ref, wo_ref, o_ref):
    x = x_ref[...]
    xn = _rms(x, g_ref[...]).astype(BF16)
    mix = None
    for i, y_ref in enumerate((ya_ref, yb_ref, yc_ref)):
        c0 = i * D_MODEL
        gate = jax.nn.sigmoid(_dot(xn, wg_ref[:, c0:c0 + D_MODEL]) + bg_ref[:, c0:c0 + D_MODEL])
        term = gate * _dot(y_ref[...], wb_ref[i])
        mix = term if mix is None else mix + term
    o_ref[...] = x + _dot(mix.astype(BF16), wo_ref[...])


def _combine(x2, ya, yb, yc, g, wg, bg, wb, wo):
    n = x2.shape[0]
    t = TOK_TILE
    const = lambda *shape: pl.BlockSpec(shape, lambda i: (0,) * len(shape))
    tok = lambda w: pl.BlockSpec((t, w), lambda i: (i, 0))
    return pl.pallas_call(
        _combine_kernel,
        grid=(n // t,),
        in_specs=[tok(D_MODEL), tok(BRANCH_W), tok(BRANCH_W), tok(BRANCH_W), const(1, D_MODEL),
                  const(D_MODEL, N_BRANCH * D_MODEL), const(1, N_BRANCH * D_MODEL),
                  const(N_BRANCH, BRANCH_W, D_MODEL), const(D_MODEL, D_MODEL)],
        out_specs=tok(D_MODEL),
        out_shape=jax.ShapeDtypeStruct((n, D_MODEL), F32),
        compiler_params=pltpu.CompilerParams(dimension_semantics=("arbitrary",),
                                             vmem_limit_bytes=VMEM_LIMIT),
        name="combine",
    )(x2, ya, yb, yc, g, wg, bg, wb, wo)


FFN_COL_CHUNK = 512


def _ffn_kernel(x_ref, g_ref, wi_ref, wo_ref, o_ref):
    x = x_ref[...]
    xn = _rms(x, g_ref[...]).astype(BF16)
    acc = x
    for c0 in range(0, FFN_HIDDEN, FFN_COL_CHUNK):
        c1 = min(c0 + FFN_COL_CHUNK, FFN_HIDDEN)
        a = jax.nn.silu(_dot(xn, wi_ref[:, c0:c1])) * _dot(xn, wi_ref[:, FFN_HIDDEN + c0:FFN_HIDDEN + c1])
        acc = acc + _dot(a.astype(BF16), wo_ref[c0:c1, :])
    o_ref[...] = acc


def _ffn(x2, g, wi, wo):
    n = x2.shape[0]
    t = TOK_TILE
    const = lambda *shape: pl.BlockSpec(shape, lambda i: (0,) * len(shape))
    tok = lambda w: pl.BlockSpec((t, w), lambda i: (i, 0))
    return pl.pallas_call(
        _ffn_kernel,
        grid=(n // t,),
        in_specs=[tok(D_MODEL), const(1, D_MODEL), const(D_MODEL, 2 * FFN_HIDDEN), const(FFN_HIDDEN, D_MODEL)],
        out_specs=tok(D_MODEL),
        out_shape=jax.ShapeDtypeStruct((n, D_MODEL), F32),
        compiler_params=pltpu.CompilerParams(dimension_semantics=("arbitrary",),
                                             vmem_limit_bytes=VMEM_LIMIT),
        name="ffn",
    )(x2, g, wi, wo)


def _rope_tables(seq):
    rows = seq // GRID_W
    row = jnp.repeat(jnp.arange(rows, dtype=F32), GRID_W)
    col = jnp.tile(jnp.arange(GRID_W, dtype=F32), rows)
    axis_dim = HEAD_DIM // 2
    freqs = ROPE_THETA ** (-jnp.arange(axis_dim // 2, dtype=F32) * 2.0 / axis_dim)
    ang = jnp.concatenate([row[:, None] * freqs[None], col[:, None] * freqs[None]], axis=-1)
    cos, sin = jnp.cos(ang), jnp.sin(ang)
    cos_full = jnp.repeat(cos, 2, axis=-1)
    sin_signed = jnp.stack([-sin, sin], axis=-1).reshape(seq, HEAD_DIM)
    return cos_full, sin_signed


def kernel(x, norm_mix, w_in, sgu_ln_g, sgu_ln_b, sgu_w, sgu_b, q_norm, k_norm, conv_w, conv_b, igate_b, fgate_b,
           lstm_norm, w_gate, b_gate, w_branch, w_out, norm_ffn, w_ffn_in, w_ffn_out):
    batch, seq, d = x.shape
    depth = norm_mix.shape[0]
    nh = LSTM_HEADS
    nc = seq // LSTM_CHUNK
    cos, sin = _rope_tables(seq)
    x2 = x.reshape(batch * seq, d)
    for l in range(depth):
        w_main = w_in[l, :, :N_MAIN].astype(BF16)
        w_g = w_in[l, :, N_MAIN:].astype(BF16)
        sgub = jnp.repeat(sgu_b[l].T, LANES, axis=1)
        ya, qa, ka, va, lq, lk, lv, lo, gcol, grow = _inproj(
            x2, norm_mix[l][None], w_main, w_g, w_g.T, sgu_ln_g[l][None], sgu_ln_b[l][None],
            sgu_w[l].astype(BF16), sgub, q_norm[l][None], k_norm[l][None], cos, sin, seq)
        yb = _attention(qa, ka, va, batch, seq)

        gcol4 = gcol.reshape(batch, seq, 2, 2, nh).transpose(0, 4, 1, 2, 3).reshape(batch, nh, seq, 4)
        grow4 = grow.reshape(2, 2, nh, batch, nc, LSTM_CHUNK).transpose(3, 2, 0, 1, 4, 5)
        grow4 = grow4.reshape(batch, nh, 4, nc, LSTM_CHUNK)
        bias4 = jnp.stack([igate_b[l], fgate_b[l]], axis=0).transpose(2, 0, 1).reshape(nh, 4)
        cw = conv_w[l].reshape(CONV_K, 2 * nh, LANES).transpose(1, 0, 2)
        cb = conv_b[l].reshape(2 * nh, 1, LANES)
        yc = _mlstm(lq, lk, lv, lo, gcol4, grow4, cw, cb, bias4[:, None, :], bias4[:, :, None],
                    lstm_norm[l].reshape(nh, 1, LANES), batch, seq)

        x2 = _combine(x2, ya, yb, yc, norm_mix[l][None], w_gate[l].astype(BF16), b_gate[l][None],
                      w_branch[l].astype(BF16), w_out[l].astype(BF16))
        x2 = _ffn(x2, norm_ffn[l][None], w_ffn_in[l].astype(BF16), w_ffn_out[l].astype(BF16))
    return x2.reshape(batch, seq, d)
```

```python
import functools

import jax
import jax.numpy as jnp
import numpy as np
from jax import lax
from jax.experimental import pallas as pl
from jax.experimental.pallas import tpu as pltpu

D_MODEL = 1024
GRID_W = 64
BRANCH_W = 512
N_BRANCH = 3
EPS = 1e-6
SGU_CHUNK = 128
SGU_GROUPS = 4
ATT_HEADS = 4
ATT_KV_HEADS = 2
HEAD_DIM = 128
ROPE_THETA = 10000.0
LSTM_HEADS = 4
LSTM_CHUNK = 128
CONV_K = 5
FFN_HIDDEN = 2816
N_MAIN = 4096
N_GATE_COLS = 2 * 2 * LSTM_HEADS

LANES = 128
SUBLANES = 8
VMEM_LIMIT = 56 * 1024 * 1024

TOK_TILE = 512
ATT_Q_TILE = 256
CONV_ROWS = 256
CONV_PAD = SUBLANES

BF16 = jnp.bfloat16
F32 = jnp.float32


def _rms(x, g):
    return x * lax.rsqrt(jnp.mean(x * x, axis=-1, keepdims=True) + EPS) * g


def _dot(a, b):
    return jnp.dot(a, b, preferred_element_type=F32)


def _dot_nt(a, b):
    return lax.dot_general(a, b, (((1,), (1,)), ((), ())), preferred_element_type=F32)


def _inproj_kernel(x_ref, g_ref, w_ref, wgr_ref, lng_ref, lnb_ref, sguw_ref, sgub_ref,
                   qn_ref, kn_ref, cos_ref, sin_ref,
                   ya_ref, qa_ref, ka_ref, va_ref, lq_ref, lk_ref, lv_ref, lo_ref, grow_ref):
    t = x_ref.shape[0]
    xn = _rms(x_ref[...], g_ref[...]).astype(BF16)

    def seg(lo, hi):
        return _dot(xn, w_ref[:, lo:hi])

    u = jax.nn.gelu(seg(0, 512))
    gv = jax.nn.gelu(seg(512, 1024))
    mu = jnp.mean(gv, axis=-1, keepdims=True)
    vc = gv - mu
    v = vc * lax.rsqrt(jnp.mean(vc * vc, axis=-1, keepdims=True) + EPS) * lng_ref[...] + lnb_ref[...]
    vb = v.astype(BF16)
    for j in range(t // SGU_CHUNK):
        r0 = j * SGU_CHUNK
        for grp in range(SGU_GROUPS):
            c0 = grp * LANES
            s = _dot(sguw_ref[grp], vb[r0:r0 + SGU_CHUNK, c0:c0 + LANES])
            s = s + sgub_ref[:, c0:c0 + LANES]
            ya_ref[r0:r0 + SGU_CHUNK, c0:c0 + LANES] = (u[r0:r0 + SGU_CHUNK, c0:c0 + LANES] * s).astype(BF16)

    cos = cos_ref[...]
    sin = sin_ref[...]
    even = (lax.broadcasted_iota(jnp.int32, (t, LANES), 1) % 2) == 0

    def norm_rope(xh, gain):
        xh = _rms(xh, gain)
        partner = jnp.where(even, pltpu.roll(xh, LANES - 1, 1), pltpu.roll(xh, 1, 1))
        return (xh * cos + partner * sin).astype(BF16)

    aq = seg(1024, 1536)
    for h in range(ATT_HEADS):
        qa_ref[:, h * LANES:(h + 1) * LANES] = norm_rope(aq[:, h * LANES:(h + 1) * LANES], qn_ref[...])
    ak = seg(1536, 1792)
    for h in range(ATT_KV_HEADS):
        ka_ref[:, h * LANES:(h + 1) * LANES] = norm_rope(ak[:, h * LANES:(h + 1) * LANES], kn_ref[...])
    va_ref[...] = seg(1792, 2048).astype(BF16)

    lq_ref[...] = seg(2048, 2560)
    lk_ref[...] = seg(2560, 3072)
    lv_ref[...] = seg(3072, 3584).astype(BF16)
    lo_ref[...] = seg(3584, 4096)
    grow_ref[...] = _dot_nt(wgr_ref[...], xn)


def _inproj(x2, g, w_main, w_gr, lng, lnb, sguw, sgub, qn, kn, cos, sin, seq):
    n = x2.shape[0]
    t = TOK_TILE
    tiles_per_seq = seq // t
    const = lambda *shape: pl.BlockSpec(shape, lambda i: (0,) * len(shape))
    tok = lambda w: pl.BlockSpec((t, w), lambda i: (i, 0))
    rope = pl.BlockSpec((t, LANES), lambda i: (i % tiles_per_seq, 0))
    out_shapes = (
        jax.ShapeDtypeStruct((n, BRANCH_W), BF16),
        jax.ShapeDtypeStruct((n, 512), BF16),
        jax.ShapeDtypeStruct((n, 256), BF16),
        jax.ShapeDtypeStruct((n, 256), BF16),
        jax.ShapeDtypeStruct((n, BRANCH_W), F32),
        jax.ShapeDtypeStruct((n, BRANCH_W), F32),
        jax.ShapeDtypeStruct((n, BRANCH_W), BF16),
        jax.ShapeDtypeStruct((n, BRANCH_W), F32),
        jax.ShapeDtypeStruct((N_GATE_COLS, n), F32),
    )
    out_specs = (tok(512), tok(512), tok(256), tok(256), tok(512), tok(512), tok(512), tok(512),
                 pl.BlockSpec((N_GATE_COLS, t), lambda i: (0, i)))
    return pl.pallas_call(
        _inproj_kernel,
        grid=(n // t,),
        in_specs=[tok(D_MODEL), const(1, D_MODEL), const(D_MODEL, N_MAIN),
                  const(N_GATE_COLS, D_MODEL), const(1, BRANCH_W), const(1, BRANCH_W),
                  const(SGU_GROUPS, SGU_CHUNK, SGU_CHUNK), const(SGU_CHUNK, BRANCH_W),
                  const(1, LANES), const(1, LANES), rope, rope],
        out_specs=out_specs,
        out_shape=out_shapes,
        compiler_params=pltpu.CompilerParams(dimension_semantics=("arbitrary",),
                                             vmem_limit_bytes=VMEM_LIMIT),
        name="inproj",
    )(x2, g, w_main, w_gr, lng, lnb, sguw, sgub, qn, kn, cos, sin)


def _attn_kernel(q_ref, k_ref, v_ref, o_ref):
    scale = HEAD_DIM ** -0.5
    rep = ATT_HEADS // ATT_KV_HEADS
    for grp in range(ATT_KV_HEADS):
        kg = k_ref[:, grp * LANES:(grp + 1) * LANES]
        vg = v_ref[:, grp * LANES:(grp + 1) * LANES]
        for r in range(rep):
            c0 = (grp * rep + r) * LANES
            s = _dot_nt(q_ref[:, c0:c0 + LANES], kg) * scale
            p = jnp.exp(s - jnp.max(s, axis=-1, keepdims=True))
            denom = jnp.sum(p, axis=-1, keepdims=True)
            o = _dot(p.astype(BF16), vg)
            o_ref[:, c0:c0 + LANES] = (o / denom).astype(BF16)


def _attention(qa, ka, va, batch, seq):
    n = qa.shape[0]
    tq = ATT_Q_TILE
    qpb = seq // tq
    return pl.pallas_call(
        _attn_kernel,
        grid=(batch, qpb),
        in_specs=[pl.BlockSpec((tq, 512), lambda b, i: (b * qpb + i, 0)),
                  pl.BlockSpec((seq, 256), lambda b, i: (b, 0)),
                  pl.BlockSpec((seq, 256), lambda b, i: (b, 0))],
        out_specs=pl.BlockSpec((tq, 512), lambda b, i: (b * qpb + i, 0)),
        out_shape=jax.ShapeDtypeStruct((n, BRANCH_W), BF16),
        compiler_params=pltpu.CompilerParams(dimension_semantics=("arbitrary", "arbitrary"),
                                             vmem_limit_bytes=VMEM_LIMIT),
        name="gqa",
    )(qa, ka, va)


def _log_sigmoid(x):
    return jnp.minimum(x, 0.0) - jnp.log1p(jnp.exp(-jnp.abs(x)))


def _lane_scan(x, op, identity, reverse):
    lane = lax.broadcasted_iota(jnp.int32, x.shape, 1)
    k = 1
    while k < LANES:
        if reverse:
            x = op(x, jnp.where(lane < LANES - k, pltpu.roll(x, LANES - k, 1), identity))
        else:
            x = op(x, jnp.where(lane >= k, pltpu.roll(x, k, 1), identity))
        k *= 2
    return x


N_SPLIT = 3
N_COLQ = 2
P_WIDTH = LANES


def _split3(x):
    hi = x.astype(BF16)
    r = x - hi.astype(F32)
    mid = r.astype(BF16)
    lo = (r - mid.astype(F32)).astype(BF16)
    return hi, mid, lo


def _gate_kernel(raw_ref, bias_ref, a_ref, wk_ref, dec_ref, split_ref):
    nc, nr = raw_ref.shape[2], raw_ref.shape[3]
    L = LSTM_CHUNK
    for d in range(2):
        rev = d == 1
        last = 0 if rev else L - 1
        ig = (raw_ref[0, d] + bias_ref[0, d]).reshape(nc * nr, L)
        lf = _log_sigmoid(raw_ref[1, d] + bias_ref[1, d]).reshape(nc * nr, L)
        b = _lane_scan(lf, jnp.add, 0.0, rev)
        a = ig - b
        g = _lane_scan(a, jnp.maximum, -jnp.inf, rev)
        btot = jnp.broadcast_to(b[:, last:last + 1], b.shape)
        gmax = jnp.broadcast_to(g[:, last:last + 1], g.shape)
        m = jnp.zeros((nr, L), F32)
        m0_rows, m1_rows = [None] * nc, [None] * nc
        for c in (range(nc - 1, -1, -1) if rev else range(nc)):
            m0_rows[c] = m
            m = btot[c * nr:(c + 1) * nr] + jnp.maximum(m, gmax[c * nr:(c + 1) * nr])
            m1_rows[c] = m
        m0 = jnp.concatenate(m0_rows, axis=0)
        m1 = jnp.concatenate(m1_rows, axis=0)
        mx = jnp.maximum(m0, g)
        a_ref[d] = a - m0
        wk_ref[d] = jnp.exp(btot - b + ig - m1)
        dec_ref[d] = jnp.exp(btot + m0 - m1)
        for q, val in enumerate((m0 - mx, -(b + mx))):
            for i, part in enumerate(_split3(val)):
                split_ref[d, q * N_SPLIT + i] = part


def _gate_prep(raw, bias):
    _, _, nc, nr, L = raw.shape
    rows = jax.ShapeDtypeStruct((2, nc * nr, L), F32)
    return pl.pallas_call(
        _gate_kernel,
        out_shape=(rows, rows, rows, jax.ShapeDtypeStruct((2, N_COLQ * N_SPLIT, nc * nr, L), BF16)),
        compiler_params=pltpu.CompilerParams(vmem_limit_bytes=VMEM_LIMIT),
        name="mlstm_gates",
    )(raw, bias)


MLSTM_UNROLL = 16


def _mlstm_kernel(lq_ref, lk_ref, lv_ref, lo_ref, a_ref, wk_ref, dec_ref, p_ref, sel_ref,
                  cw_q_ref, cw_k_ref, cb_q_ref, cb_k_ref, norm_ref, o_ref,
                  xp_ref, q_s, kt_s, ktb_s, hf_s, hb_s, c_s):
    seq = lq_ref.shape[0]
    L = LSTM_CHUNK
    nc = seq // L

    zpad = jnp.zeros((CONV_PAD, LANES), F32)
    xp_ref[0:CONV_PAD, :] = zpad
    xp_ref[CONV_PAD + seq:CONV_PAD + seq + CONV_PAD, :] = zpad

    def conv_silu(src_ref, w_ref, b_ref, store):
        xp_ref[CONV_PAD:CONV_PAD + seq, :] = src_ref[...]
        for r0 in range(0, seq, CONV_ROWS):
            acc = jnp.zeros((CONV_ROWS, LANES), F32) + b_ref[0]
            for j in range(CONV_K):
                start = CONV_PAD + r0 + j - CONV_K // 2
                acc = acc + xp_ref[start:start + CONV_ROWS, :] * w_ref[0, j:j + 1, :]
            store(r0, jax.nn.silu(acc))

    def store_q(r0, val):
        q_s[r0:r0 + CONV_ROWS, :] = val.astype(BF16)

    def store_kt(r0, val):
        val = val * (HEAD_DIM ** -0.5)
        for j in range(CONV_ROWS // L):
            kt = val[j * L:(j + 1) * L, :].T
            kt_s[r0 // L + j] = kt
            ktb_s[r0 // L + j] = kt.astype(BF16)

    conv_silu(lq_ref, cw_q_ref, cb_q_ref, store_q)
    conv_silu(lk_ref, cw_k_ref, cb_k_ref, store_kt)

    row_i = lax.broadcasted_iota(jnp.int32, (L, L), 0)
    col_i = lax.broadcasted_iota(jnp.int32, (L, L), 1)
    masks = (col_i <= row_i, col_i >= row_i)
    ones = jnp.ones((L, LANES), BF16)
    c_s[...] = jnp.zeros(c_s.shape, F32)

    def chunk_step(d, c, h_ref):
        rows = pl.ds(pl.multiple_of(c * L, L), L)
        row = pl.ds(c, 1)
        qc = q_s[rows, :]
        vaug = jnp.concatenate([lv_ref[rows, :], ones], axis=1)
        caug = c_s[d]
        qk = _dot(qc, ktb_s[c])
        e = _dot(p_ref[0, rows, :], sel_ref[d])
        dmx = e[:, :LANES]
        w = jnp.where(masks[d], jnp.exp(a_ref[0, d, row, :] + dmx), 0.0)
        s = (qk * w).astype(BF16)
        kw = (kt_s[c] * wk_ref[0, d, row, :]).astype(BF16)
        r = _dot(jnp.concatenate([s, kw], axis=0), vaug)
        inter = _dot(qc, caug.astype(BF16))
        wi = jnp.exp(dmx)
        num = wi * inter[:, :LANES] + r[:L, :LANES]
        den = wi * inter[:, LANES:] + r[:L, LANES:]
        h_ref[rows, :] = num / jnp.maximum(jnp.abs(den), jnp.exp(e[:, LANES:]))
        dec = dec_ref[0, d, row, :]
        c_s[d] = jnp.concatenate([dec, dec], axis=1) * caug + r[L:, :]

    def pair(i, carry):
        chunk_step(0, i, hf_s)
        chunk_step(1, nc - 1 - i, hb_s)
        return carry

    lax.fori_loop(0, nc, pair, 0, unroll=MLSTM_UNROLL)

    h = _rms(hf_s[...] + hb_s[...], norm_ref[0])
    o_ref[...] = (h * jax.nn.sigmoid(lo_ref[...])).astype(BF16)


def _mlstm(lq, lk, lv, lo, a_rows, wk_rows, dec_rows, ptab, sel, cw, cb, norm, batch, seq):
    n = lq.shape[0]
    nh = LSTM_HEADS
    nc = seq // LSTM_CHUNK
    head_blk = lambda: pl.BlockSpec((seq, LANES), lambda b, h: (b, h))
    rows_blk = lambda: pl.BlockSpec((1, 2, nc, LSTM_CHUNK), lambda b, h: (b * nh + h, 0, 0, 0))
    return pl.pallas_call(
        _mlstm_kernel,
        grid=(batch, nh),
        in_specs=[head_blk(), head_blk(), head_blk(), head_blk(),
                  rows_blk(), rows_blk(), rows_blk(),
                  pl.BlockSpec((1, seq, P_WIDTH), lambda b, h: (b * nh + h, 0, 0)),
                  pl.BlockSpec((2, P_WIDTH, N_COLQ * LANES), lambda b, h: (0, 0, 0)),
                  pl.BlockSpec((1, CONV_K, LANES), lambda b, h: (h, 0, 0)),
                  pl.BlockSpec((1, CONV_K, LANES), lambda b, h: (nh + h, 0, 0)),
                  pl.BlockSpec((1, 1, LANES), lambda b, h: (h, 0, 0)),
                  pl.BlockSpec((1, 1, LANES), lambda b, h: (nh + h, 0, 0)),
                  pl.BlockSpec((1, 1, LANES), lambda b, h: (h, 0, 0))],
        out_specs=head_blk(),
        out_shape=jax.ShapeDtypeStruct((n, BRANCH_W), BF16),
        scratch_shapes=[pltpu.VMEM((seq + 2 * CONV_PAD, LANES), F32),
                        pltpu.VMEM((seq, LANES), BF16),
                        pltpu.VMEM((nc, HEAD_DIM, LSTM_CHUNK), F32),
                        pltpu.VMEM((nc, HEAD_DIM, LSTM_CHUNK), BF16),
                        pltpu.VMEM((seq, LANES), F32),
                        pltpu.VMEM((seq, LANES), F32),
                        pltpu.VMEM((2, HEAD_DIM, 2 * LANES), F32)],
        compiler_params=pltpu.CompilerParams(dimension_semantics=("arbitrary", "arbitrary"),
                                             vmem_limit_bytes=VMEM_LIMIT),
        name="mlstm",
    )(lq, lk, lv, lo, a_rows, wk_rows, dec_rows, ptab, sel, cw, cw, cb, cb, norm)


def _combine_kernel(x_ref, ya_ref, yb_ref, yc_ref, g_ref, wg_ref, bg_ref, wb_ref, wo_ref, o_ref):
    x = x_ref[...]
    xn = _rms(x, g_ref[...]).astype(BF16)
    mix = None
    for i, y_ref in enumerate((ya_ref, yb_ref, yc_ref)):
        c0 = i * D_MODEL
        gate = jax.nn.sigmoid(_dot(xn, wg_ref[:, c0:c0 + D_MODEL]) + bg_ref[:, c0:c0 + D_MODEL])
        term = gate * _dot(y_ref[...], wb_ref[i])
        mix = term if mix is None else mix + term
    o_ref[...] = x + _dot(mix.astype(BF16), wo_ref[...])


def _combine(x2, ya, yb, yc, g, wg, bg, wb, wo):
    n = x2.shape[0]
    t = TOK_TILE
    const = lambda *shape: pl.BlockSpec(shape, lambda i: (0,) * len(shape))
    tok = lambda w: pl.BlockSpec((t, w), lambda i: (i, 0))
    return pl.pallas_call(
        _combine_kernel,
        grid=(n // t,),
        in_specs=[tok(D_MODEL), tok(BRANCH_W), tok(BRANCH_W), tok(BRANCH_W), const(1, D_MODEL),
                  const(D_MODEL, N_BRANCH * D_MODEL), const(1, N_BRANCH * D_MODEL),
                  const(N_BRANCH, BRANCH_W, D_MODEL), const(D_MODEL, D_MODEL)],
        out_specs=tok(D_MODEL),
        out_shape=jax.ShapeDtypeStruct((n, D_MODEL), F32),
        compiler_params=pltpu.CompilerParams(dimension_semantics=("arbitrary",),
                                             vmem_limit_bytes=VMEM_LIMIT),
        name="combine",
    )(x2, ya, yb, yc, g, wg, bg, wb, wo)


FFN_COL_CHUNK = 512


def _ffn_kernel(x_ref, g_ref, wi_ref, wo_ref, o_ref):
    x = x_ref[...]
    xn = _rms(x, g_ref[...]).astype(BF16)
    acc = x
    for c0 in range(0, FFN_HIDDEN, FFN_COL_CHUNK):
        c1 = min(c0 + FFN_COL_CHUNK, FFN_HIDDEN)
        a = jax.nn.silu(_dot(xn, wi_ref[:, c0:c1])) * _dot(xn, wi_ref[:, FFN_HIDDEN + c0:FFN_HIDDEN + c1])
        acc = acc + _dot(a.astype(BF16), wo_ref[c0:c1, :])
    o_ref[...] = acc


def _ffn(x2, g, wi, wo):
    n = x2.shape[0]
    t = TOK_TILE
    const = lambda *shape: pl.BlockSpec(shape, lambda i: (0,) * len(shape))
    tok = lambda w: pl.BlockSpec((t, w), lambda i: (i, 0))
    return pl.pallas_call(
        _ffn_kernel,
        grid=(n // t,),
        in_specs=[tok(D_MODEL), const(1, D_MODEL), const(D_MODEL, 2 * FFN_HIDDEN), const(FFN_HIDDEN, D_MODEL)],
        out_specs=tok(D_MODEL),
        out_shape=jax.ShapeDtypeStruct((n, D_MODEL), F32),
        compiler_params=pltpu.CompilerParams(dimension_semantics=("arbitrary",),
                                             vmem_limit_bytes=VMEM_LIMIT),
        name="ffn",
    )(x2, g, wi, wo)


def _rope_tables(seq):
    rows = seq // GRID_W
    row = jnp.repeat(jnp.arange(rows, dtype=F32), GRID_W)
    col = jnp.tile(jnp.arange(GRID_W, dtype=F32), rows)
    axis_dim = HEAD_DIM // 2
    freqs = ROPE_THETA ** (-jnp.arange(axis_dim // 2, dtype=F32) * 2.0 / axis_dim)
    ang = jnp.concatenate([row[:, None] * freqs[None], col[:, None] * freqs[None]], axis=-1)
    cos, sin = jnp.cos(ang), jnp.sin(ang)
    cos_full = jnp.repeat(cos, 2, axis=-1)
    sin_signed = jnp.stack([-sin, sin], axis=-1).reshape(seq, HEAD_DIM)
    return cos_full, sin_signed


def _sel_matrices():
    sel = np.zeros((2, P_WIDTH, N_COLQ * LANES), np.float32)
    for d in range(2):
        for q in range(N_COLQ):
            k0 = (d * N_COLQ + q) * N_SPLIT
            sel[d, k0:k0 + N_SPLIT, q * LANES:(q + 1) * LANES] = 1.0
    return jnp.asarray(sel, BF16)


def kernel(x, norm_mix, w_in, sgu_ln_g, sgu_ln_b, sgu_w, sgu_b, q_norm, k_norm, conv_w, conv_b, igate_b, fgate_b,
           lstm_norm, w_gate, b_gate, w_branch, w_out, norm_ffn, w_ffn_in, w_ffn_out):
    batch, seq, d = x.shape
    depth = norm_mix.shape[0]
    nh = LSTM_HEADS
    L = LSTM_CHUNK
    nc = seq // L
    nr = batch * nh
    cos, sin = _rope_tables(seq)
    sel = _sel_matrices()
    x2 = x.reshape(batch * seq, d)
    for l in range(depth):
        w_main = w_in[l, :, :N_MAIN].astype(BF16)
        w_g = w_in[l, :, N_MAIN:].astype(BF16)
        sgub = jnp.repeat(sgu_b[l].T, LANES, axis=1)
        ya, qa, ka, va, lq, lk, lv, lo, grow = _inproj(
            x2, norm_mix[l][None], w_main, w_g.T, sgu_ln_g[l][None], sgu_ln_b[l][None],
            sgu_w[l].astype(BF16), sgub, q_norm[l][None], k_norm[l][None], cos, sin, seq)
        yb = _attention(qa, ka, va, batch, seq)

        raw = grow.reshape(2, 2, nh, batch, nc, L).transpose(0, 1, 4, 3, 2, 5).reshape(2, 2, nc, nr, L)
        gbias = jnp.stack([igate_b[l], fgate_b[l]], axis=0)
        gbias = jnp.broadcast_to(gbias[:, :, None, :, None], (2, 2, batch, nh, L)).reshape(2, 2, nr, L)
        a_rows, wk_rows, dec_rows, splits = _gate_prep(raw, gbias)
        per_head = lambda r: r.reshape(2, nc, nr, L).transpose(2, 0, 1, 3)
        ptab = splits.reshape(2, N_COLQ * N_SPLIT, nc, nr, L).transpose(3, 2, 4, 0, 1)
        ptab = ptab.reshape(nr, seq, 2 * N_COLQ * N_SPLIT)
        ptab = jnp.pad(ptab, ((0, 0), (0, 0), (0, P_WIDTH - 2 * N_COLQ * N_SPLIT)))
        cw = conv_w[l].reshape(CONV_K, 2 * nh, LANES).transpose(1, 0, 2)
        cb = conv_b[l].reshape(2 * nh, 1, LANES)
        yc = _mlstm(lq, lk, lv, lo, per_head(a_rows), per_head(wk_rows), per_head(dec_rows), ptab, sel,
                    cw, cb, lstm_norm[l].reshape(nh, 1, LANES), batch, seq)

        x2 = _combine(x2, ya, yb, yc, norm_mix[l][None], w_gate[l].astype(BF16), b_gate[l][None],
                      w_branch[l].astype(BF16), w_out[l].astype(BF16))
        x2 = _ffn(x2, norm_ffn[l][None], w_ffn_in[l].astype(BF16), w_ffn_out[l].astype(BF16))
    return x2.reshape(batch, seq, d)
```

```python
import functools

import jax
import jax.numpy as jnp
import numpy as np
from jax import lax
from jax.experimental import pallas as pl
from jax.experimental.pallas import tpu as pltpu

D_MODEL = 1024
GRID_W = 64
BRANCH_W = 512
N_BRANCH = 3
EPS = 1e-6
SGU_CHUNK = 128
SGU_GROUPS = 4
ATT_HEADS = 4
ATT_KV_HEADS = 2
HEAD_DIM = 128
ROPE_THETA = 10000.0
LSTM_HEADS = 4
LSTM_CHUNK = 128
CONV_K = 5
FFN_HIDDEN = 2816
N_MAIN = 4096
N_GATE_COLS = 2 * 2 * LSTM_HEADS

LANES = 128
SUBLANES = 8
VMEM_LIMIT = 56 * 1024 * 1024

TOK_TILE = 512
ATT_Q_TILE = 512
ATT_KEY_BLOCK = 512
CONV_ROWS = 256
CONV_PAD = SUBLANES

LOG2_E = 1.4426950408889634

BF16 = jnp.bfloat16
F32 = jnp.float32


def _rms(x, g):
    return x * lax.rsqrt(jnp.mean(x * x, axis=-1, keepdims=True) + EPS) * g


def _dot(a, b):
    return jnp.dot(a, b, preferred_element_type=F32)


def _dot_nt(a, b):
    return lax.dot_general(a, b, (((1,), (1,)), ((), ())), preferred_element_type=F32)


def _inproj_kernel(x_ref, g_ref, w_ref, wgr_ref, lng_ref, lnb_ref, sguw_ref, sgub_ref,
                   qn_ref, kn_ref, cos_ref, sin_ref,
                   ya_ref, qa_ref, ka_ref, va_ref, lq_ref, lk_ref, lv_ref, lo_ref, grow_ref):
    t = x_ref.shape[0]
    xn = _rms(x_ref[...], g_ref[...]).astype(BF16)

    def seg(lo, hi):
        return _dot(xn, w_ref[:, lo:hi])

    u = jax.nn.gelu(seg(0, 512))
    gv = jax.nn.gelu(seg(512, 1024))
    mu = jnp.mean(gv, axis=-1, keepdims=True)
    vc = gv - mu
    v = vc * lax.rsqrt(jnp.mean(vc * vc, axis=-1, keepdims=True) + EPS) * lng_ref[...] + lnb_ref[...]
    vb = v.astype(BF16)
    for j in range(t // SGU_CHUNK):
        r0 = j * SGU_CHUNK
        for grp in range(SGU_GROUPS):
            c0 = grp * LANES
            s = _dot(sguw_ref[grp], vb[r0:r0 + SGU_CHUNK, c0:c0 + LANES])
            s = s + sgub_ref[:, c0:c0 + LANES]
            ya_ref[r0:r0 + SGU_CHUNK, c0:c0 + LANES] = (u[r0:r0 + SGU_CHUNK, c0:c0 + LANES] * s).astype(BF16)

    cos = cos_ref[...]
    sin = sin_ref[...]
    even = (lax.broadcasted_iota(jnp.int32, (t, LANES), 1) % 2) == 0

    def norm_rope(xh, gain):
        xh = _rms(xh, gain)
        partner = jnp.where(even, pltpu.roll(xh, LANES - 1, 1), pltpu.roll(xh, 1, 1))
        return (xh * cos + partner * sin).astype(BF16)

    aq = seg(1024, 1536)
    for h in range(ATT_HEADS):
        qa_ref[:, h * LANES:(h + 1) * LANES] = norm_rope(aq[:, h * LANES:(h + 1) * LANES], qn_ref[...])
    ak = seg(1536, 1792)
    for h in range(ATT_KV_HEADS):
        ka_ref[:, h * LANES:(h + 1) * LANES] = norm_rope(ak[:, h * LANES:(h + 1) * LANES], kn_ref[...])
    va_ref[...] = seg(1792, 2048).astype(BF16)

    lq_ref[...] = seg(2048, 2560)
    lk_ref[...] = seg(2560, 3072)
    lv_ref[...] = seg(3072, 3584).astype(BF16)
    lo_ref[...] = seg(3584, 4096)
    grow_ref[...] = _dot_nt(wgr_ref[...], xn)


def _inproj(x2, g, w_main, w_gr, lng, lnb, sguw, sgub, qn, kn, cos, sin, seq):
    n = x2.shape[0]
    t = TOK_TILE
    tiles_per_seq = seq // t
    const = lambda *shape: pl.BlockSpec(shape, lambda i: (0,) * len(shape))
    tok = lambda w: pl.BlockSpec((t, w), lambda i: (i, 0))
    rope = pl.BlockSpec((t, LANES), lambda i: (i % tiles_per_seq, 0))
    out_shapes = (
        jax.ShapeDtypeStruct((n, BRANCH_W), BF16),
        jax.ShapeDtypeStruct((n, 512), BF16),
        jax.ShapeDtypeStruct((n, 256), BF16),
        jax.ShapeDtypeStruct((n, 256), BF16),
        jax.ShapeDtypeStruct((n, BRANCH_W), F32),
        jax.ShapeDtypeStruct((n, BRANCH_W), F32),
        jax.ShapeDtypeStruct((n, BRANCH_W), BF16),
        jax.ShapeDtypeStruct((n, BRANCH_W), F32),
        jax.ShapeDtypeStruct((N_GATE_COLS, n), F32),
    )
    out_specs = (tok(512), tok(512), tok(256), tok(256), tok(512), tok(512), tok(512), tok(512),
                 pl.BlockSpec((N_GATE_COLS, t), lambda i: (0, i)))
    return pl.pallas_call(
        _inproj_kernel,
        grid=(n // t,),
        in_specs=[tok(D_MODEL), const(1, D_MODEL), const(D_MODEL, N_MAIN),
                  const(N_GATE_COLS, D_MODEL), const(1, BRANCH_W), const(1, BRANCH_W),
                  const(SGU_GROUPS, SGU_CHUNK, SGU_CHUNK), const(SGU_CHUNK, BRANCH_W),
                  const(1, LANES), const(1, LANES), rope, rope],
        out_specs=out_specs,
        out_shape=out_shapes,
        compiler_params=pltpu.CompilerParams(dimension_semantics=("arbitrary",),
                                             vmem_limit_bytes=VMEM_LIMIT),
        name="inproj",
    )(x2, g, w_main, w_gr, lng, lnb, sguw, sgub, qn, kn, cos, sin)


def _attn_kernel(q_ref, k_ref, v_ref, o_ref):
    exp2_scale = (HEAD_DIM ** -0.5) * LOG2_E
    rep = ATT_HEADS // ATT_KV_HEADS
    tq = q_ref.shape[0]
    kb = ATT_KEY_BLOCK
    ones = jnp.ones((kb, LANES), BF16)
    qs, ms, accs = [], [], []
    for grp in range(ATT_KV_HEADS):
        c0 = grp * rep * LANES
        qs.append(jnp.concatenate([q_ref[:, c0 + r * LANES:c0 + (r + 1) * LANES] for r in range(rep)], axis=0))
    for j in range(k_ref.shape[0] // kb):
        for grp in range(ATT_KV_HEADS):
            kblk = k_ref[j * kb:(j + 1) * kb, grp * LANES:(grp + 1) * LANES]
            vaug = jnp.concatenate([v_ref[j * kb:(j + 1) * kb, grp * LANES:(grp + 1) * LANES], ones], axis=1)
            s = _dot_nt(qs[grp], kblk)
            bmax = jnp.max(s, axis=-1, keepdims=True)
            if j == 0:
                ms.append(bmax)
                accs.append(_dot(jnp.exp2((s - bmax) * exp2_scale).astype(BF16), vaug))
            else:
                m_new = jnp.maximum(ms[grp], bmax)
                alpha = jnp.exp2((ms[grp] - m_new) * exp2_scale)
                p = jnp.exp2((s - m_new) * exp2_scale).astype(BF16)
                accs[grp] = alpha * accs[grp] + _dot(p, vaug)
                ms[grp] = m_new
    for grp in range(ATT_KV_HEADS):
        c0 = grp * rep * LANES
        o = accs[grp][:, :LANES] / accs[grp][:, LANES:]
        for r in range(rep):
            o_ref[:, c0 + r * LANES:c0 + (r + 1) * LANES] = o[r * tq:(r + 1) * tq].astype(BF16)


def _attention(qa, ka, va, batch, seq):
    n = qa.shape[0]
    tq = ATT_Q_TILE
    qpb = seq // tq
    return pl.pallas_call(
        _attn_kernel,
        grid=(batch, qpb),
        in_specs=[pl.BlockSpec((tq, 512), lambda b, i: (b * qpb + i, 0)),
                  pl.BlockSpec((seq, 256), lambda b, i: (b, 0)),
                  pl.BlockSpec((seq, 256), lambda b, i: (b, 0))],
        out_specs=pl.BlockSpec((tq, 512), lambda b, i: (b * qpb + i, 0)),
        out_shape=jax.ShapeDtypeStruct((n, BRANCH_W), BF16),
        compiler_params=pltpu.CompilerParams(dimension_semantics=("arbitrary", "arbitrary"),
                                             vmem_limit_bytes=VMEM_LIMIT),
        name="gqa",
    )(qa, ka, va)


def _log_sigmoid(x):
    return jnp.minimum(x, 0.0) - jnp.log1p(jnp.exp(-jnp.abs(x)))


def _lane_scan(x, op, identity, reverse):
    lane = lax.broadcasted_iota(jnp.int32, x.shape, 1)
    k = 1
    while k < LANES:
        if reverse:
            x = op(x, jnp.where(lane < LANES - k, pltpu.roll(x, LANES - k, 1), identity))
        else:
            x = op(x, jnp.where(lane >= k, pltpu.roll(x, k, 1), identity))
        k *= 2
    return x


N_SPLIT = 3
N_COLQ = 2
P_WIDTH = LANES


def _split3(x):
    hi = x.astype(BF16)
    r = x - hi.astype(F32)
    mid = r.astype(BF16)
    lo = (r - mid.astype(F32)).astype(BF16)
    return hi, mid, lo


def _gate_kernel(raw_ref, bias_ref, a_ref, wk_ref, dec_ref, split_ref):
    nc, nr = raw_ref.shape[2], raw_ref.shape[3]
    L = LSTM_CHUNK
    for d in range(2):
        rev = d == 1
        last = 0 if rev else L - 1
        ig = (raw_ref[0, d] + bias_ref[0, d]).reshape(nc * nr, L)
        lf = _log_sigmoid(raw_ref[1, d] + bias_ref[1, d]).reshape(nc * nr, L)
        b = _lane_scan(lf, jnp.add, 0.0, rev)
        a = ig - b
        g = _lane_scan(a, jnp.maximum, -jnp.inf, rev)
        btot = jnp.broadcast_to(b[:, last:last + 1], b.shape)
        gmax = jnp.broadcast_to(g[:, last:last + 1], g.shape)
        m = jnp.zeros((nr, L), F32)
        m0_rows, m1_rows = [None] * nc, [None] * nc
        for c in (range(nc - 1, -1, -1) if rev else range(nc)):
            m0_rows[c] = m
            m = btot[c * nr:(c + 1) * nr] + jnp.maximum(m, gmax[c * nr:(c + 1) * nr])
            m1_rows[c] = m
        m0 = jnp.concatenate(m0_rows, axis=0)
        m1 = jnp.concatenate(m1_rows, axis=0)
        mx = jnp.maximum(m0, g)
        a_ref[d] = a - m0
        wk_ref[d] = jnp.exp(btot - b + ig - m1)
        dec_ref[d] = jnp.exp(btot + m0 - m1)
        for q, val in enumerate((m0 - mx, -(b + mx))):
            for i, part in enumerate(_split3(val)):
                split_ref[d, q * N_SPLIT + i] = part


def _gate_prep(raw, bias):
    _, _, nc, nr, L = raw.shape
    rows = jax.ShapeDtypeStruct((2, nc * nr, L), F32)
    return pl.pallas_call(
        _gate_kernel,
        out_shape=(rows, rows, rows, jax.ShapeDtypeStruct((2, N_COLQ * N_SPLIT, nc * nr, L), BF16)),
        compiler_params=pltpu.CompilerParams(vmem_limit_bytes=VMEM_LIMIT),
        name="mlstm_gates",
    )(raw, bias)


MLSTM_UNROLL = 16


def _mlstm_kernel(lq_ref, lk_ref, lv_ref, lo_ref, a_ref, wk_ref, dec_ref, p_ref, sel_ref,
                  cw_q_ref, cw_k_ref, cb_q_ref, cb_k_ref, norm_ref, o_ref,
                  xp_ref, q_s, kt_s, ktb_s, hf_s, hb_s, c_s):
    seq = lq_ref.shape[0]
    L = LSTM_CHUNK
    nc = seq // L

    zpad = jnp.zeros((CONV_PAD, LANES), F32)
    xp_ref[0:CONV_PAD, :] = zpad
    xp_ref[CONV_PAD + seq:CONV_PAD + seq + CONV_PAD, :] = zpad

    def conv_silu(src_ref, w_ref, b_ref, store):
        xp_ref[CONV_PAD:CONV_PAD + seq, :] = src_ref[...]
        for r0 in range(0, seq, CONV_ROWS):
            acc = jnp.zeros((CONV_ROWS, LANES), F32) + b_ref[0]
            for j in range(CONV_K):
                start = CONV_PAD + r0 + j - CONV_K // 2
                acc = acc + xp_ref[start:start + CONV_ROWS, :] * w_ref[0, j:j + 1, :]
            store(r0, jax.nn.silu(acc))

    def store_q(r0, val):
        q_s[r0:r0 + CONV_ROWS, :] = val.astype(BF16)

    def store_kt(r0, val):
        val = val * (HEAD_DIM ** -0.5)
        for j in range(CONV_ROWS // L):
            kt = val[j * L:(j + 1) * L, :].T
            kt_s[r0 // L + j] = kt
            ktb_s[r0 // L + j] = kt.astype(BF16)

    conv_silu(lq_ref, cw_q_ref, cb_q_ref, store_q)
    conv_silu(lk_ref, cw_k_ref, cb_k_ref, store_kt)

    row_i = lax.broadcasted_iota(jnp.int32, (L, L), 0)
    col_i = lax.broadcasted_iota(jnp.int32, (L, L), 1)
    masks = (col_i <= row_i, col_i >= row_i)
    ones = jnp.ones((L, LANES), BF16)
    c_s[...] = jnp.zeros(c_s.shape, F32)

    def chunk_step(d, c, h_ref):
        rows = pl.ds(pl.multiple_of(c * L, L), L)
        row = pl.ds(c, 1)
        qc = q_s[rows, :]
        vaug = jnp.concatenate([lv_ref[rows, :], ones], axis=1)
        caug = c_s[d]
        qk = _dot(qc, ktb_s[c])
        e = _dot(p_ref[0, rows, :], sel_ref[d])
        dmx = e[:, :LANES]
        w = jnp.where(masks[d], jnp.exp(a_ref[0, d, row, :] + dmx), 0.0)
        s = (qk * w).astype(BF16)
        kw = (kt_s[c] * wk_ref[0, d, row, :]).astype(BF16)
        r = _dot(jnp.concatenate([s, kw], axis=0), vaug)
        inter = _dot(qc, caug.astype(BF16))
        wi = jnp.exp(dmx)
        num = wi * inter[:, :LANES] + r[:L, :LANES]
        den = wi * inter[:, LANES:] + r[:L, LANES:]
        h_ref[rows, :] = num / jnp.maximum(jnp.abs(den), jnp.exp(e[:, LANES:]))
        dec = dec_ref[0, d, row, :]
        c_s[d] = jnp.concatenate([dec, dec], axis=1) * caug + r[L:, :]

    def pair(i, carry):
        chunk_step(0, i, hf_s)
        chunk_step(1, nc - 1 - i, hb_s)
        return carry

    lax.fori_loop(0, nc, pair, 0, unroll=MLSTM_UNROLL)

    h = _rms(hf_s[...] + hb_s[...], norm_ref[0])
    o_ref[...] = (h * jax.nn.sigmoid(lo_ref[...])).astype(BF16)


def _mlstm(lq, lk, lv, lo, a_rows, wk_rows, dec_rows, ptab, sel, cw, cb, norm, batch, seq):
    n = lq.shape[0]
    nh = LSTM_HEADS
    nc = seq // LSTM_CHUNK
    head_blk = lambda: pl.BlockSpec((seq, LANES), lambda b, h: (b, h))
    rows_blk = lambda: pl.BlockSpec((1, 2, nc, LSTM_CHUNK), lambda b, h: (b * nh + h, 0, 0, 0))
    return pl.pallas_call(
        _mlstm_kernel,
        grid=(batch, nh),
        in_specs=[head_blk(), head_blk(), head_blk(), head_blk(),
                  rows_blk(), rows_blk(), rows_blk(),
                  pl.BlockSpec((1, seq, P_WIDTH), lambda b, h: (b * nh + h, 0, 0)),
                  pl.BlockSpec((2, P_WIDTH, N_COLQ * LANES), lambda b, h: (0, 0, 0)),
                  pl.BlockSpec((1, CONV_K, LANES), lambda b, h: (h, 0, 0)),
                  pl.BlockSpec((1, CONV_K, LANES), lambda b, h: (nh + h, 0, 0)),
                  pl.BlockSpec((1, 1, LANES), lambda b, h: (h, 0, 0)),
                  pl.BlockSpec((1, 1, LANES), lambda b, h: (nh + h, 0, 0)),
                  pl.BlockSpec((1, 1, LANES), lambda b, h: (h, 0, 0))],
        out_specs=head_blk(),
        out_shape=jax.ShapeDtypeStruct((n, BRANCH_W), BF16),
        scratch_shapes=[pltpu.VMEM((seq + 2 * CONV_PAD, LANES), F32),
                        pltpu.VMEM((seq, LANES), BF16),
                        pltpu.VMEM((nc, HEAD_DIM, LSTM_CHUNK), F32),
                        pltpu.VMEM((nc, HEAD_DIM, LSTM_CHUNK), BF16),
                        pltpu.VMEM((seq, LANES), F32),
                        pltpu.VMEM((seq, LANES), F32),
                        pltpu.VMEM((2, HEAD_DIM, 2 * LANES), F32)],
        compiler_params=pltpu.CompilerParams(dimension_semantics=("arbitrary", "arbitrary"),
                                             vmem_limit_bytes=VMEM_LIMIT),
        name="mlstm",
    )(lq, lk, lv, lo, a_rows, wk_rows, dec_rows, ptab, sel, cw, cw, cb, cb, norm)


def _combine_kernel(x_ref, ya_ref, yb_ref, yc_ref, g_ref, wg_ref, bg_ref, wb_ref, wo_ref, o_ref):
    x = x_ref[...]
    xn = _rms(x, g_ref[...]).astype(BF16)
    mix = None
    for i, y_ref in enumerate((ya_ref, yb_ref, yc_ref)):
        c0 = i * D_MODEL
        gate = jax.nn.sigmoid(_dot(xn, wg_ref[:, c0:c0 + D_MODEL]) + bg_ref[:, c0:c0 + D_MODEL])
        term = gate * _dot(y_ref[...], wb_ref[i])
        mix = term if mix is None else mix + term
    o_ref[...] = x + _dot(mix.astype(BF16), wo_ref[...])


def _combine(x2, ya, yb, yc, g, wg, bg, wb, wo):
    n = x2.shape[0]
    t = TOK_TILE
    const = lambda *shape: pl.BlockSpec(shape, lambda i: (0,) * len(shape))
    tok = lambda w: pl.BlockSpec((t, w), lambda i: (i, 0))
    return pl.pallas_call(
        _combine_kernel,
        grid=(n // t,),
        in_specs=[tok(D_MODEL), tok(BRANCH_W), tok(BRANCH_W), tok(BRANCH_W), const(1, D_MODEL),
                  const(D_MODEL, N_BRANCH * D_MODEL), const(1, N_BRANCH * D_MODEL),
                  const(N_BRANCH, BRANCH_W, D_MODEL), const(D_MODEL, D_MODEL)],
        out_specs=tok(D_MODEL),
        out_shape=jax.ShapeDtypeStruct((n, D_MODEL), F32),
        compiler_params=pltpu.CompilerParams(dimension_semantics=("arbitrary",),
                                             vmem_limit_bytes=VMEM_LIMIT),
        name="combine",
    )(x2, ya, yb, yc, g, wg, bg, wb, wo)


FFN_COL_CHUNK = 512


def _ffn_kernel(x_ref, g_ref, wi_ref, wo_ref, o_ref):
    x = x_ref[...]
    xn = _rms(x, g_ref[...]).astype(BF16)
    acc = x
    for c0 in range(0, FFN_HIDDEN, FFN_COL_CHUNK):
        c1 = min(c0 + FFN_COL_CHUNK, FFN_HIDDEN)
        a = jax.nn.silu(_dot(xn, wi_ref[:, c0:c1])) * _dot(xn, wi_ref[:, FFN_HIDDEN + c0:FFN_HIDDEN + c1])
        acc = acc + _dot(a.astype(BF16), wo_ref[c0:c1, :])
    o_ref[...] = acc


def _ffn(x2, g, wi, wo):
    n = x2.shape[0]
    t = TOK_TILE
    const = lambda *shape: pl.BlockSpec(shape, lambda i: (0,) * len(shape))
    tok = lambda w: pl.BlockSpec((t, w), lambda i: (i, 0))
    return pl.pallas_call(
        _ffn_kernel,
        grid=(n // t,),
        in_specs=[tok(D_MODEL), const(1, D_MODEL), const(D_MODEL, 2 * FFN_HIDDEN), const(FFN_HIDDEN, D_MODEL)],
        out_specs=tok(D_MODEL),
        out_shape=jax.ShapeDtypeStruct((n, D_MODEL), F32),
        compiler_params=pltpu.CompilerParams(dimension_semantics=("arbitrary",),
                                             vmem_limit_bytes=VMEM_LIMIT),
        name="ffn",
    )(x2, g, wi, wo)


def _rope_tables(seq):
    rows = seq // GRID_W
    row = jnp.repeat(jnp.arange(rows, dtype=F32), GRID_W)
    col = jnp.tile(jnp.arange(GRID_W, dtype=F32), rows)
    axis_dim = HEAD_DIM // 2
    freqs = ROPE_THETA ** (-jnp.arange(axis_dim // 2, dtype=F32) * 2.0 / axis_dim)
    ang = jnp.concatenate([row[:, None] * freqs[None], col[:, None] * freqs[None]], axis=-1)
    cos, sin = jnp.cos(ang), jnp.sin(ang)
    cos_full = jnp.repeat(cos, 2, axis=-1)
    sin_signed = jnp.stack([-sin, sin], axis=-1).reshape(seq, HEAD_DIM)
    return cos_full, sin_signed


def _sel_matrices():
    sel = np.zeros((2, P_WIDTH, N_COLQ * LANES), np.float32)
    for d in range(2):
        for q in range(N_COLQ):
            k0 = (d * N_COLQ + q) * N_SPLIT
            sel[d, k0:k0 + N_SPLIT, q * LANES:(q + 1) * LANES] = 1.0
    return jnp.asarray(sel, BF16)


def kernel(x, norm_mix, w_in, sgu_ln_g, sgu_ln_b, sgu_w, sgu_b, q_norm, k_norm, conv_w, conv_b, igate_b, fgate_b,
           lstm_norm, w_gate, b_gate, w_branch, w_out, norm_ffn, w_ffn_in, w_ffn_out):
    batch, seq, d = x.shape
    depth = norm_mix.shape[0]
    nh = LSTM_HEADS
    L = LSTM_CHUNK
    nc = seq // L
    nr = batch * nh
    cos, sin = _rope_tables(seq)
    sel = _sel_matrices()
    x2 = x.reshape(batch * seq, d)
    for l in range(depth):
        w_main = w_in[l, :, :N_MAIN].astype(BF16)
        w_g = w_in[l, :, N_MAIN:].astype(BF16)
        sgub = jnp.repeat(sgu_b[l].T, LANES, axis=1)
        ya, qa, ka, va, lq, lk, lv, lo, grow = _inproj(
            x2, norm_mix[l][None], w_main, w_g.T, sgu_ln_g[l][None], sgu_ln_b[l][None],
            sgu_w[l].astype(BF16), sgub, q_norm[l][None], k_norm[l][None], cos, sin, seq)
        yb = _attention(qa, ka, va, batch, seq)

        raw = grow.reshape(2, 2, nh, batch, nc, L).transpose(0, 1, 4, 3, 2, 5).reshape(2, 2, nc, nr, L)
        gbias = jnp.stack([igate_b[l], fgate_b[l]], axis=0)
        gbias = jnp.broadcast_to(gbias[:, :, None, :, None], (2, 2, batch, nh, L)).reshape(2, 2, nr, L)
        a_rows, wk_rows, dec_rows, splits = _gate_prep(raw, gbias)
        per_head = lambda r: r.reshape(2, nc, nr, L).transpose(2, 0, 1, 3)
        ptab = splits.reshape(2, N_COLQ * N_SPLIT, nc, nr, L).transpose(3, 2, 4, 0, 1)
        ptab = ptab.reshape(nr, seq, 2 * N_COLQ * N_SPLIT)
        ptab = jnp.pad(ptab, ((0, 0), (0, 0), (0, P_WIDTH - 2 * N_COLQ * N_SPLIT)))
        cw = conv_w[l].reshape(CONV_K, 2 * nh, LANES).transpose(1, 0, 2)
        cb = conv_b[l].reshape(2 * nh, 1, LANES)
        yc = _mlstm(lq, lk, lv, lo, per_head(a_rows), per_head(wk_rows), per_head(dec_rows), ptab, sel,
                    cw, cb, lstm_norm[l].reshape(nh, 1, LANES), batch, seq)

        x2 = _combine(x2, ya, yb, yc, norm_mix[l][None], w_gate[l].astype(BF16), b_gate[l][None],
                      w_branch[l].astype(BF16), w_out[l].astype(BF16))
        x2 = _ffn(x2, norm_ffn[l][None], w_ffn_in[l].astype(BF16), w_ffn_out[l].astype(BF16))
    return x2.reshape(batch, seq, d)
```

```python
import jax
import jax.numpy as jnp
import numpy as np
from jax import lax
from jax.experimental import pallas as pl
from jax.experimental.pallas import tpu as pltpu

D_MODEL = 1024
GRID_W = 64
BRANCH_W = 512
N_BRANCH = 3
EPS = 1e-6
SGU_CHUNK = 128
SGU_GROUPS = 4
ATT_HEADS = 4
ATT_KV_HEADS = 2
HEAD_DIM = 128
ROPE_THETA = 10000.0
LSTM_HEADS = 4
LSTM_CHUNK = 128
CONV_K = 5
FFN_HIDDEN = 2816
N_MAIN = 4096
N_GATE_COLS = 2 * 2 * LSTM_HEADS

LANES = 128
SUBLANES = 8
VMEM_LIMIT = 56 * 1024 * 1024

TOK_TILE = 512
ATT_Q_TILE = 512
ATT_KEY_BLOCK = 512
CONV_ROWS = 256
CONV_PAD = SUBLANES

LOG2_E = 1.4426950408889634

BF16 = jnp.bfloat16
F32 = jnp.float32


def _rms(x, g):
    return x * lax.rsqrt(jnp.mean(x * x, axis=-1, keepdims=True) + EPS) * g


def _dot(a, b):
    return jnp.dot(a, b, preferred_element_type=F32)


def _dot_nt(a, b):
    return lax.dot_general(a, b, (((1,), (1,)), ((), ())), preferred_element_type=F32)


def _dot_tn(a, b):
    return lax.dot_general(a, b, (((0,), (0,)), ((), ())), preferred_element_type=F32)


def _inproj_kernel(x_ref, g_ref, w_ref, wgr_ref, lng_ref, lnb_ref, sguw_ref, sgub_ref,
                   qn_ref, kn_ref, cos_ref, sin_ref,
                   ya_ref, qa_ref, ka_ref, va_ref, lq_ref, lk_ref, lv_ref, lo_ref, grow_ref):
    t = x_ref.shape[0]
    xn = _rms(x_ref[...], g_ref[...]).astype(BF16)

    def seg(lo, hi):
        return _dot(xn, w_ref[:, lo:hi])

    u = jax.nn.gelu(seg(0, 512))
    gv = jax.nn.gelu(seg(512, 1024))
    mu = jnp.mean(gv, axis=-1, keepdims=True)
    vc = gv - mu
    v = vc * lax.rsqrt(jnp.mean(vc * vc, axis=-1, keepdims=True) + EPS) * lng_ref[...] + lnb_ref[...]
    vb = v.astype(BF16)
    for j in range(t // SGU_CHUNK):
        r0 = j * SGU_CHUNK
        for grp in range(SGU_GROUPS):
            c0 = grp * LANES
            s = _dot(sguw_ref[grp], vb[r0:r0 + SGU_CHUNK, c0:c0 + LANES])
            s = s + sgub_ref[:, c0:c0 + LANES]
            ya_ref[r0:r0 + SGU_CHUNK, c0:c0 + LANES] = (u[r0:r0 + SGU_CHUNK, c0:c0 + LANES] * s).astype(BF16)

    cos = cos_ref[...]
    sin = sin_ref[...]
    even = (lax.broadcasted_iota(jnp.int32, (t, LANES), 1) % 2) == 0

    def norm_rope(xh, gain):
        xh = _rms(xh, gain)
        partner = jnp.where(even, pltpu.roll(xh, LANES - 1, 1), pltpu.roll(xh, 1, 1))
        return (xh * cos + partner * sin).astype(BF16)

    aq = seg(1024, 1536)
    for h in range(ATT_HEADS):
        qa_ref[:, h * LANES:(h + 1) * LANES] = norm_rope(aq[:, h * LANES:(h + 1) * LANES], qn_ref[...])
    ak = seg(1536, 1792)
    for h in range(ATT_KV_HEADS):
        ka_ref[:, h * LANES:(h + 1) * LANES] = norm_rope(ak[:, h * LANES:(h + 1) * LANES], kn_ref[...])
    va_ref[...] = seg(1792, 2048).astype(BF16)

    lq_ref[...] = seg(2048, 2560)
    lk_ref[...] = seg(2560, 3072)
    lv_ref[...] = seg(3072, 3584).astype(BF16)
    lo_ref[...] = seg(3584, 4096)
    grow_ref[...] = _dot_nt(wgr_ref[...], xn)


def _inproj(x2, g, w_main, w_gr, lng, lnb, sguw, sgub, qn, kn, cos, sin, seq):
    n = x2.shape[0]
    t = TOK_TILE
    tiles_per_seq = seq // t
    const = lambda *shape: pl.BlockSpec(shape, lambda i: (0,) * len(shape))
    tok = lambda w: pl.BlockSpec((t, w), lambda i: (i, 0))
    rope = pl.BlockSpec((t, LANES), lambda i: (i % tiles_per_seq, 0))
    out_shapes = (
        jax.ShapeDtypeStruct((n, BRANCH_W), BF16),
        jax.ShapeDtypeStruct((n, 512), BF16),
        jax.ShapeDtypeStruct((n, 256), BF16),
        jax.ShapeDtypeStruct((n, 256), BF16),
        jax.ShapeDtypeStruct((n, BRANCH_W), F32),
        jax.ShapeDtypeStruct((n, BRANCH_W), F32),
        jax.ShapeDtypeStruct((n, BRANCH_W), BF16),
        jax.ShapeDtypeStruct((n, BRANCH_W), F32),
        jax.ShapeDtypeStruct((N_GATE_COLS, n), F32),
    )
    out_specs = (tok(512), tok(512), tok(256), tok(256), tok(512), tok(512), tok(512), tok(512),
                 pl.BlockSpec((N_GATE_COLS, t), lambda i: (0, i)))
    return pl.pallas_call(
        _inproj_kernel,
        grid=(n // t,),
        in_specs=[tok(D_MODEL), const(1, D_MODEL), const(D_MODEL, N_MAIN),
                  const(N_GATE_COLS, D_MODEL), const(1, BRANCH_W), const(1, BRANCH_W),
                  const(SGU_GROUPS, SGU_CHUNK, SGU_CHUNK), const(SGU_CHUNK, BRANCH_W),
                  const(1, LANES), const(1, LANES), rope, rope],
        out_specs=out_specs,
        out_shape=out_shapes,
        compiler_params=pltpu.CompilerParams(dimension_semantics=("arbitrary",),
                                             vmem_limit_bytes=VMEM_LIMIT),
        name="inproj",
    )(x2, g, w_main, w_gr, lng, lnb, sguw, sgub, qn, kn, cos, sin)


def _attn_kernel(q_ref, k_ref, v_ref, o_ref):
    exp2_scale = (HEAD_DIM ** -0.5) * LOG2_E
    rep = ATT_HEADS // ATT_KV_HEADS
    tq = q_ref.shape[0]
    kb = ATT_KEY_BLOCK
    ones = jnp.ones((kb, LANES), BF16)
    qs, ms, accs = [], [], []
    for grp in range(ATT_KV_HEADS):
        c0 = grp * rep * LANES
        qs.append(jnp.concatenate([q_ref[:, c0 + r * LANES:c0 + (r + 1) * LANES] for r in range(rep)], axis=0))
    for j in range(k_ref.shape[0] // kb):
        for grp in range(ATT_KV_HEADS):
            kblk = k_ref[j * kb:(j + 1) * kb, grp * LANES:(grp + 1) * LANES]
            vaug = jnp.concatenate([v_ref[j * kb:(j + 1) * kb, grp * LANES:(grp + 1) * LANES], ones], axis=1)
            s = _dot_nt(qs[grp], kblk)
            bmax = jnp.max(s, axis=-1, keepdims=True)
            if j == 0:
                ms.append(bmax)
                accs.append(_dot(jnp.exp2((s - bmax) * exp2_scale).astype(BF16), vaug))
            else:
                m_new = jnp.maximum(ms[grp], bmax)
                alpha = jnp.exp2((ms[grp] - m_new) * exp2_scale)
                p = jnp.exp2((s - m_new) * exp2_scale).astype(BF16)
                accs[grp] = alpha * accs[grp] + _dot(p, vaug)
                ms[grp] = m_new
    for grp in range(ATT_KV_HEADS):
        c0 = grp * rep * LANES
        o = accs[grp][:, :LANES] / accs[grp][:, LANES:]
        for r in range(rep):
            o_ref[:, c0 + r * LANES:c0 + (r + 1) * LANES] = o[r * tq:(r + 1) * tq].astype(BF16)


def _attention(qa, ka, va, batch, seq):
    n = qa.shape[0]
    tq = ATT_Q_TILE
    qpb = seq // tq
    return pl.pallas_call(
        _attn_kernel,
        grid=(batch, qpb),
        in_specs=[pl.BlockSpec((tq, 512), lambda b, i: (b * qpb + i, 0)),
                  pl.BlockSpec((seq, 256), lambda b, i: (b, 0)),
                  pl.BlockSpec((seq, 256), lambda b, i: (b, 0))],
        out_specs=pl.BlockSpec((tq, 512), lambda b, i: (b * qpb + i, 0)),
        out_shape=jax.ShapeDtypeStruct((n, BRANCH_W), BF16),
        compiler_params=pltpu.CompilerParams(dimension_semantics=("arbitrary", "arbitrary"),
                                             vmem_limit_bytes=VMEM_LIMIT),
        name="gqa",
    )(qa, ka, va)


def _log_sigmoid(x):
    return jnp.minimum(x, 0.0) - jnp.log1p(jnp.exp(-jnp.abs(x)))


def _lane_scan(x, op, identity, reverse):
    lane = lax.broadcasted_iota(jnp.int32, x.shape, 1)
    k = 1
    while k < LANES:
        if reverse:
            x = op(x, jnp.where(lane < LANES - k, pltpu.roll(x, LANES - k, 1), identity))
        else:
            x = op(x, jnp.where(lane >= k, pltpu.roll(x, k, 1), identity))
        k *= 2
    return x


N_SPLIT = 3
N_COLQ = 2
P_ROWS = 16


def _split3(x):
    hi = x.astype(BF16)
    r = x - hi.astype(F32)
    mid = r.astype(BF16)
    lo = (r - mid.astype(F32)).astype(BF16)
    return hi, mid, lo


def _gate_kernel(raw_ref, bias_ref, a_ref, wk_ref, dec_ref, split_ref):
    nc, nr = raw_ref.shape[2], raw_ref.shape[3]
    L = LSTM_CHUNK
    for d in range(2):
        rev = d == 1
        last = 0 if rev else L - 1
        ig = (raw_ref[0, d] + bias_ref[0, d]).reshape(nc * nr, L)
        lf = _log_sigmoid(raw_ref[1, d] + bias_ref[1, d]).reshape(nc * nr, L)
        b = _lane_scan(lf, jnp.add, 0.0, rev)
        a = ig - b
        g = _lane_scan(a, jnp.maximum, -jnp.inf, rev)
        btot = jnp.broadcast_to(b[:, last:last + 1], b.shape)
        gmax = jnp.broadcast_to(g[:, last:last + 1], g.shape)
        m = jnp.zeros((nr, L), F32)
        m0_rows, m1_rows = [None] * nc, [None] * nc
        for c in (range(nc - 1, -1, -1) if rev else range(nc)):
            m0_rows[c] = m
            m = btot[c * nr:(c + 1) * nr] + jnp.maximum(m, gmax[c * nr:(c + 1) * nr])
            m1_rows[c] = m
        m0 = jnp.concatenate(m0_rows, axis=0)
        m1 = jnp.concatenate(m1_rows, axis=0)
        mx = jnp.maximum(m0, g)
        a_ref[d] = a - m0
        wk_ref[d] = jnp.exp(btot - b + ig - m1)
        dec_ref[d] = jnp.exp(btot + m0 - m1)
        for q, val in enumerate((m0 - mx, -(b + mx))):
            for i, part in enumerate(_split3(val)):
                split_ref[d, q * N_SPLIT + i] = part
        for k in range(N_COLQ * N_SPLIT, P_ROWS):
            split_ref[d, k] = jnp.zeros((nc * nr, L), BF16)


def _gate_prep(raw, bias):
    _, _, nc, nr, L = raw.shape
    rows = jax.ShapeDtypeStruct((2, nc * nr, L), F32)
    return pl.pallas_call(
        _gate_kernel,
        out_shape=(rows, rows, rows, jax.ShapeDtypeStruct((2, P_ROWS, nc * nr, L), BF16)),
        compiler_params=pltpu.CompilerParams(vmem_limit_bytes=VMEM_LIMIT),
        name="mlstm_gates",
    )(raw, bias)


def _mlstm_kernel(lq_ref, lk_ref, lv_ref, a_ref, wk_ref, dec_ref, p_ref, sel_ref,
                  cw_q_ref, cw_k_ref, cb_q_ref, cb_k_ref, o_ref, xq_s, xk_s, q_s, kt_s, ktb_s, c_s):
    seq = lq_ref.shape[0]
    L = LSTM_CHUNK
    nc = seq // L

    zpad = jnp.zeros((CONV_PAD, LANES), F32)
    for x_s, src_ref in ((xq_s, lq_ref), (xk_s, lk_ref)):
        x_s[0:CONV_PAD, :] = zpad
        x_s[CONV_PAD + seq:CONV_PAD + seq + CONV_PAD, :] = zpad
        x_s[CONV_PAD:CONV_PAD + seq, :] = src_ref[...]

    def conv_silu(x_s, w_ref, b_ref, r0):
        acc = jnp.zeros((CONV_ROWS, LANES), F32) + b_ref[0]
        for j in range(CONV_K):
            start = CONV_PAD + r0 + j - CONV_K // 2
            acc = acc + x_s[start:start + CONV_ROWS, :] * w_ref[0, j:j + 1, :]
        return jax.nn.silu(acc)

    n_blocks = seq // CONV_ROWS
    for i in range(n_blocks // 2):
        for blk in (i, n_blocks - 1 - i):
            r0 = blk * CONV_ROWS
            q_s[r0:r0 + CONV_ROWS, :] = conv_silu(xq_s, cw_q_ref, cb_q_ref, r0)
            k = conv_silu(xk_s, cw_k_ref, cb_k_ref, r0) * (HEAD_DIM ** -0.5)
            for j in range(CONV_ROWS // L):
                kt = k[j * L:(j + 1) * L, :].T
                kt_s[r0 // L + j] = kt
                ktb_s[r0 // L + j] = kt.astype(BF16)

    row_i = lax.broadcasted_iota(jnp.int32, (L, L), 0)
    col_i = lax.broadcasted_iota(jnp.int32, (L, L), 1)
    masks = (col_i <= row_i, col_i >= row_i)
    ones = jnp.ones((L, LANES), BF16)
    zeros = jnp.zeros((HEAD_DIM, HEAD_DIM), BF16)
    c_s[...] = jnp.zeros(c_s.shape, F32)

    def chunk_step(d, c, first):
        rows = pl.ds(c * L, L)
        row = pl.ds(c, 1)
        vaug = jnp.concatenate([lv_ref[rows, :], ones], axis=1)
        caug = c_s[d]
        qc = q_s[rows, :]
        qk = _dot(qc.astype(BF16), ktb_s[c])
        e = _dot_tn(p_ref[0, d, c], sel_ref[...])
        dmx = e[:, :LANES]
        w = jnp.where(masks[d], jnp.exp(a_ref[0, d, row, :] + dmx), 0.0)
        s = (qk * w).astype(BF16)
        qw = (qc * jnp.exp(dmx)).astype(BF16)
        kw = (kt_s[c] * wk_ref[0, d, row, :]).astype(BF16)
        lhs = jnp.concatenate([jnp.concatenate([s, qw], axis=1), jnp.concatenate([kw, zeros], axis=1)], axis=0)
        rhs = jnp.concatenate([vaug, caug.astype(BF16)], axis=0)
        r = _dot(lhs, rhs)
        h = r[:L, :LANES] / jnp.maximum(jnp.abs(r[:L, LANES:]), jnp.exp(e[:, LANES:]))
        o_ref[rows, :] = h if first else o_ref[rows, :] + h
        dec = dec_ref[0, d, row, :]
        c_s[d] = jnp.concatenate([dec, dec], axis=1) * caug + r[L:, :]

    for i in range(nc):
        chunk_step(0, i, first=i < nc - 1 - i)
        chunk_step(1, nc - 1 - i, first=nc - 1 - i > i)


def _mlstm(lq, lk, lv, a_rows, wk_rows, dec_rows, ptab, sel, cw, cb, batch, seq):
    n = lq.shape[0]
    nh = LSTM_HEADS
    nc = seq // LSTM_CHUNK
    head_blk = lambda: pl.BlockSpec((seq, LANES), lambda b, h: (b, h))
    rows_blk = lambda: pl.BlockSpec((1, 2, nc, LSTM_CHUNK), lambda b, h: (b * nh + h, 0, 0, 0))
    return pl.pallas_call(
        _mlstm_kernel,
        grid=(batch, nh),
        in_specs=[head_blk(), head_blk(), head_blk(),
                  rows_blk(), rows_blk(), rows_blk(),
                  pl.BlockSpec((1, 2, nc, P_ROWS, LSTM_CHUNK), lambda b, h: (b * nh + h, 0, 0, 0, 0)),
                  pl.BlockSpec((P_ROWS, N_COLQ * LANES), lambda b, h: (0, 0)),
                  pl.BlockSpec((1, CONV_K, LANES), lambda b, h: (h, 0, 0)),
                  pl.BlockSpec((1, CONV_K, LANES), lambda b, h: (nh + h, 0, 0)),
                  pl.BlockSpec((1, 1, LANES), lambda b, h: (h, 0, 0)),
                  pl.BlockSpec((1, 1, LANES), lambda b, h: (nh + h, 0, 0))],
        out_specs=head_blk(),
        out_shape=jax.ShapeDtypeStruct((n, BRANCH_W), F32),
        scratch_shapes=[pltpu.VMEM((seq + 2 * CONV_PAD, LANES), F32),
                        pltpu.VMEM((seq + 2 * CONV_PAD, LANES), F32),
                        pltpu.VMEM((seq, LANES), F32),
                        pltpu.VMEM((nc, HEAD_DIM, LSTM_CHUNK), F32),
                        pltpu.VMEM((nc, HEAD_DIM, LSTM_CHUNK), BF16),
                        pltpu.VMEM((2, HEAD_DIM, 2 * LANES), F32)],
        compiler_params=pltpu.CompilerParams(dimension_semantics=("arbitrary", "arbitrary"),
                                             vmem_limit_bytes=VMEM_LIMIT),
        name="mlstm",
    )(lq, lk, lv, a_rows, wk_rows, dec_rows, ptab, sel, cw, cw, cb, cb)


def _combine_kernel(x_ref, ya_ref, yb_ref, hc_ref, lo_ref, g_ref, hn_ref, wg_ref, bg_ref, wb_ref, wo_ref, o_ref):
    x = x_ref[...]
    xn = _rms(x, g_ref[...]).astype(BF16)
    yc = jnp.concatenate(
        [_rms(hc_ref[:, h * LANES:(h + 1) * LANES], hn_ref[:, h * LANES:(h + 1) * LANES]) for h in range(LSTM_HEADS)],
        axis=1)
    yc = (yc * jax.nn.sigmoid(lo_ref[...])).astype(BF16)
    mix = None
    for i, y in enumerate((ya_ref[...], yb_ref[...], yc)):
        c0 = i * D_MODEL
        gate = jax.nn.sigmoid(_dot(xn, wg_ref[:, c0:c0 + D_MODEL]) + bg_ref[:, c0:c0 + D_MODEL])
        term = gate * _dot(y, wb_ref[i])
        mix = term if mix is None else mix + term
    o_ref[...] = x + _dot(mix.astype(BF16), wo_ref[...])


def _combine(x2, ya, yb, hc, lo, g, hn, wg, bg, wb, wo):
    n = x2.shape[0]
    t = TOK_TILE
    const = lambda *shape: pl.BlockSpec(shape, lambda i: (0,) * len(shape))
    tok = lambda w: pl.BlockSpec((t, w), lambda i: (i, 0))
    return pl.pallas_call(
        _combine_kernel,
        grid=(n // t,),
        in_specs=[tok(D_MODEL), tok(BRANCH_W), tok(BRANCH_W), tok(BRANCH_W), tok(BRANCH_W), const(1, D_MODEL),
                  const(1, BRANCH_W), const(D_MODEL, N_BRANCH * D_MODEL), const(1, N_BRANCH * D_MODEL),
                  const(N_BRANCH, BRANCH_W, D_MODEL), const(D_MODEL, D_MODEL)],
        out_specs=tok(D_MODEL),
        out_shape=jax.ShapeDtypeStruct((n, D_MODEL), F32),
        compiler_params=pltpu.CompilerParams(dimension_semantics=("arbitrary",),
                                             vmem_limit_bytes=VMEM_LIMIT),
        name="combine",
    )(x2, ya, yb, hc, lo, g, hn, wg, bg, wb, wo)


FFN_COL_CHUNK = 512


def _ffn_kernel(x_ref, g_ref, wi_ref, wo_ref, o_ref):
    x = x_ref[...]
    xn = _rms(x, g_ref[...]).astype(BF16)
    acc = x
    for c0 in range(0, FFN_HIDDEN, FFN_COL_CHUNK):
        c1 = min(c0 + FFN_COL_CHUNK, FFN_HIDDEN)
        a = jax.nn.silu(_dot(xn, wi_ref[:, c0:c1])) * _dot(xn, wi_ref[:, FFN_HIDDEN + c0:FFN_HIDDEN + c1])
        acc = acc + _dot(a.astype(BF16), wo_ref[c0:c1, :])
    o_ref[...] = acc


def _ffn(x2, g, wi, wo):
    n = x2.shape[0]
    t = TOK_TILE
    const = lambda *shape: pl.BlockSpec(shape, lambda i: (0,) * len(shape))
    tok = lambda w: pl.BlockSpec((t, w), lambda i: (i, 0))
    return pl.pallas_call(
        _ffn_kernel,
        grid=(n // t,),
        in_specs=[tok(D_MODEL), const(1, D_MODEL), const(D_MODEL, 2 * FFN_HIDDEN), const(FFN_HIDDEN, D_MODEL)],
        out_specs=tok(D_MODEL),
        out_shape=jax.ShapeDtypeStruct((n, D_MODEL), F32),
        compiler_params=pltpu.CompilerParams(dimension_semantics=("arbitrary",),
                                             vmem_limit_bytes=VMEM_LIMIT),
        name="ffn",
    )(x2, g, wi, wo)


def _rope_tables(seq):
    rows = seq // GRID_W
    row = jnp.repeat(jnp.arange(rows, dtype=F32), GRID_W)
    col = jnp.tile(jnp.arange(GRID_W, dtype=F32), rows)
    axis_dim = HEAD_DIM // 2
    freqs = ROPE_THETA ** (-jnp.arange(axis_dim // 2, dtype=F32) * 2.0 / axis_dim)
    ang = jnp.concatenate([row[:, None] * freqs[None], col[:, None] * freqs[None]], axis=-1)
    cos, sin = jnp.cos(ang), jnp.sin(ang)
    cos_full = jnp.repeat(cos, 2, axis=-1)
    sin_signed = jnp.stack([-sin, sin], axis=-1).reshape(seq, HEAD_DIM)
    return cos_full, sin_signed


def _sel_matrix():
    sel = np.zeros((P_ROWS, N_COLQ * LANES), np.float32)
    for q in range(N_COLQ):
        sel[q * N_SPLIT:(q + 1) * N_SPLIT, q * LANES:(q + 1) * LANES] = 1.0
    return jnp.asarray(sel, BF16)


def kernel(x, norm_mix, w_in, sgu_ln_g, sgu_ln_b, sgu_w, sgu_b, q_norm, k_norm, conv_w, conv_b, igate_b, fgate_b,
           lstm_norm, w_gate, b_gate, w_branch, w_out, norm_ffn, w_ffn_in, w_ffn_out):
    batch, seq, d = x.shape
    depth = norm_mix.shape[0]
    nh = LSTM_HEADS
    L = LSTM_CHUNK
    nc = seq // L
    nr = batch * nh
    cos, sin = _rope_tables(seq)
    sel = _sel_matrix()
    x2 = x.reshape(batch * seq, d)
    for l in range(depth):
        w_main = w_in[l, :, :N_MAIN].astype(BF16)
        w_g = w_in[l, :, N_MAIN:].astype(BF16)
        sgub = jnp.repeat(sgu_b[l].T, LANES, axis=1)
        ya, qa, ka, va, lq, lk, lv, lo, grow = _inproj(
            x2, norm_mix[l][None], w_main, w_g.T, sgu_ln_g[l][None], sgu_ln_b[l][None],
            sgu_w[l].astype(BF16), sgub, q_norm[l][None], k_norm[l][None], cos, sin, seq)
        yb = _attention(qa, ka, va, batch, seq)

        raw = grow.reshape(2, 2, nh, batch, nc, L).transpose(0, 1, 4, 3, 2, 5).reshape(2, 2, nc, nr, L)
        gbias = jnp.stack([igate_b[l], fgate_b[l]], axis=0)
        gbias = jnp.broadcast_to(gbias[:, :, None, :, None], (2, 2, batch, nh, L)).reshape(2, 2, nr, L)
        a_rows, wk_rows, dec_rows, splits = _gate_prep(raw, gbias)
        per_head = lambda r: r.reshape(2, nc, nr, L).transpose(2, 0, 1, 3)
        ptab = splits.reshape(2, P_ROWS, nc, nr, L).transpose(3, 0, 2, 1, 4)
        cw = conv_w[l].reshape(CONV_K, 2 * nh, LANES).transpose(1, 0, 2)
        cb = conv_b[l].reshape(2 * nh, 1, LANES)
        hc = _mlstm(lq, lk, lv, per_head(a_rows), per_head(wk_rows), per_head(dec_rows), ptab, sel,
                    cw, cb, batch, seq)

        x2 = _combine(x2, ya, yb, hc, lo, norm_mix[l][None], lstm_norm[l][None], w_gate[l].astype(BF16),
                      b_gate[l][None], w_branch[l].astype(BF16), w_out[l].astype(BF16))
        x2 = _ffn(x2, norm_ffn[l][None], w_ffn_in[l].astype(BF16), w_ffn_out[l].astype(BF16))
    return x2.reshape(batch, seq, d)
```

```python
import jax
import jax.numpy as jnp
import numpy as np
from jax import lax
from jax.experimental import pallas as pl
from jax.experimental.pallas import tpu as pltpu

D_MODEL = 1024
GRID_W = 64
BRANCH_W = 512
N_BRANCH = 3
EPS = 1e-6
SGU_CHUNK = 128
SGU_GROUPS = 4
ATT_HEADS = 4
ATT_KV_HEADS = 2
HEAD_DIM = 128
ROPE_THETA = 10000.0
LSTM_HEADS = 4
LSTM_CHUNK = 128
CONV_K = 5
FFN_HIDDEN = 2816
N_MAIN = 4096
N_GATE_COLS = 2 * 2 * LSTM_HEADS

LANES = 128
SUBLANES = 8
VMEM_LIMIT = 56 * 1024 * 1024

TOK_TILE = 512
ATT_Q_TILE = 512
ATT_KEY_BLOCK = 512
CONV_ROWS = 256
CONV_PAD = SUBLANES

LOG2_E = 1.4426950408889634

BF16 = jnp.bfloat16
F32 = jnp.float32


def _rms(x, g):
    return x * lax.rsqrt(jnp.mean(x * x, axis=-1, keepdims=True) + EPS) * g


def _dot(a, b):
    return jnp.dot(a, b, preferred_element_type=F32)


def _dot_nt(a, b):
    return lax.dot_general(a, b, (((1,), (1,)), ((), ())), preferred_element_type=F32)


def _dot_tn(a, b):
    return lax.dot_general(a, b, (((0,), (0,)), ((), ())), preferred_element_type=F32)


def _inproj_kernel(x_ref, g_ref, w_ref, wgr_ref, lng_ref, lnb_ref, sguw_ref, sgub_ref,
                   qn_ref, kn_ref, cos_ref, sin_ref,
                   ya_ref, qa_ref, ka_ref, va_ref, lq_ref, lk_ref, lv_ref, lo_ref, grow_ref):
    t = x_ref.shape[0]
    xn = _rms(x_ref[...], g_ref[...]).astype(BF16)

    def seg(lo, hi):
        return _dot(xn, w_ref[:, lo:hi])

    gv = jax.nn.gelu(seg(512, 1024))
    mu = jnp.mean(gv, axis=-1, keepdims=True)
    vc = gv - mu
    v = vc * lax.rsqrt(jnp.mean(vc * vc, axis=-1, keepdims=True) + EPS) * lng_ref[...] + lnb_ref[...]
    vb = v.astype(BF16)
    u = jax.nn.gelu(seg(0, 512))

    cos = cos_ref[...]
    sin = sin_ref[...]
    even = (lax.broadcasted_iota(jnp.int32, (t, LANES), 1) % 2) == 0

    def norm_rope(xh, gain):
        xh = _rms(xh, gain)
        partner = jnp.where(even, pltpu.roll(xh, LANES - 1, 1), pltpu.roll(xh, 1, 1))
        return (xh * cos + partner * sin).astype(BF16)

    aq = seg(1024, 1536)
    for h in range(ATT_HEADS):
        qa_ref[:, h * LANES:(h + 1) * LANES] = norm_rope(aq[:, h * LANES:(h + 1) * LANES], qn_ref[...])
    ak = seg(1536, 1792)
    for h in range(ATT_KV_HEADS):
        ka_ref[:, h * LANES:(h + 1) * LANES] = norm_rope(ak[:, h * LANES:(h + 1) * LANES], kn_ref[...])
    va_ref[...] = seg(1792, 2048).astype(BF16)

    lq_ref[...] = seg(2048, 2560)
    lk_ref[...] = seg(2560, 3072)
    lv_ref[...] = seg(3072, 3584).astype(BF16)
    lo_ref[...] = seg(3584, 4096)
    grow_ref[...] = _dot_nt(wgr_ref[...], xn)

    n_chunks = t // SGU_CHUNK
    for grp in range(SGU_GROUPS):
        c0 = grp * LANES
        vcat = jnp.concatenate([vb[j * SGU_CHUNK:(j + 1) * SGU_CHUNK, c0:c0 + LANES] for j in range(n_chunks)], axis=1)
        s = _dot(sguw_ref[grp], vcat)
        for j in range(n_chunks):
            r0 = j * SGU_CHUNK
            sj = s[:, j * LANES:(j + 1) * LANES] + sgub_ref[:, c0:c0 + LANES]
            ya_ref[r0:r0 + SGU_CHUNK, c0:c0 + LANES] = (u[r0:r0 + SGU_CHUNK, c0:c0 + LANES] * sj).astype(BF16)


def _inproj(x2, g, w_main, w_gr, lng, lnb, sguw, sgub, qn, kn, cos, sin, seq):
    n = x2.shape[0]
    t = TOK_TILE
    tiles_per_seq = seq // t
    const = lambda *shape: pl.BlockSpec(shape, lambda i: (0,) * len(shape))
    tok = lambda w: pl.BlockSpec((t, w), lambda i: (i, 0))
    rope = pl.BlockSpec((t, LANES), lambda i: (i % tiles_per_seq, 0))
    out_shapes = (
        jax.ShapeDtypeStruct((n, BRANCH_W), BF16),
        jax.ShapeDtypeStruct((n, 512), BF16),
        jax.ShapeDtypeStruct((n, 256), BF16),
        jax.ShapeDtypeStruct((n, 256), BF16),
        jax.ShapeDtypeStruct((n, BRANCH_W), F32),
        jax.ShapeDtypeStruct((n, BRANCH_W), F32),
        jax.ShapeDtypeStruct((n, BRANCH_W), BF16),
        jax.ShapeDtypeStruct((n, BRANCH_W), F32),
        jax.ShapeDtypeStruct((N_GATE_COLS, n), F32),
    )
    out_specs = (tok(512), tok(512), tok(256), tok(256), tok(512), tok(512), tok(512), tok(512),
                 pl.BlockSpec((N_GATE_COLS, t), lambda i: (0, i)))
    return pl.pallas_call(
        _inproj_kernel,
        grid=(n // t,),
        in_specs=[tok(D_MODEL), const(1, D_MODEL), const(D_MODEL, N_MAIN),
                  const(N_GATE_COLS, D_MODEL), const(1, BRANCH_W), const(1, BRANCH_W),
                  const(SGU_GROUPS, SGU_CHUNK, SGU_CHUNK), const(SGU_CHUNK, BRANCH_W),
                  const(1, LANES), const(1, LANES), rope, rope],
        out_specs=out_specs,
        out_shape=out_shapes,
        compiler_params=pltpu.CompilerParams(dimension_semantics=("arbitrary",),
                                             vmem_limit_bytes=VMEM_LIMIT),
        name="inproj",
    )(x2, g, w_main, w_gr, lng, lnb, sguw, sgub, qn, kn, cos, sin)


def _attn_kernel(q_ref, k_ref, v_ref, o_ref):
    exp2_scale = (HEAD_DIM ** -0.5) * LOG2_E
    rep = ATT_HEADS // ATT_KV_HEADS
    tq = q_ref.shape[0]
    kb = ATT_KEY_BLOCK
    ones = jnp.ones((kb, LANES), BF16)
    qs, ms, accs = [], [], []
    for grp in range(ATT_KV_HEADS):
        c0 = grp * rep * LANES
        qs.append(jnp.concatenate([q_ref[:, c0 + r * LANES:c0 + (r + 1) * LANES] for r in range(rep)], axis=0))
    for j in range(k_ref.shape[0] // kb):
        for grp in range(ATT_KV_HEADS):
            kblk = k_ref[j * kb:(j + 1) * kb, grp * LANES:(grp + 1) * LANES]
            vaug = jnp.concatenate([v_ref[j * kb:(j + 1) * kb, grp * LANES:(grp + 1) * LANES], ones], axis=1)
            s = _dot_nt(qs[grp], kblk)
            bmax = jnp.max(s, axis=-1, keepdims=True)
            if j == 0:
                ms.append(bmax)
                accs.append(_dot(jnp.exp2((s - bmax) * exp2_scale).astype(BF16), vaug))
            else:
                m_new = jnp.maximum(ms[grp], bmax)
                alpha = jnp.exp2((ms[grp] - m_new) * exp2_scale)
                p = jnp.exp2((s - m_new) * exp2_scale).astype(BF16)
                accs[grp] = alpha * accs[grp] + _dot(p, vaug)
                ms[grp] = m_new
    for grp in range(ATT_KV_HEADS):
        c0 = grp * rep * LANES
        o = accs[grp][:, :LANES] / accs[grp][:, LANES:]
        for r in range(rep):
            o_ref[:, c0 + r * LANES:c0 + (r + 1) * LANES] = o[r * tq:(r + 1) * tq].astype(BF16)


def _attention(qa, ka, va, batch, seq):
    n = qa.shape[0]
    tq = ATT_Q_TILE
    qpb = seq // tq
    return pl.pallas_call(
        _attn_kernel,
        grid=(batch, qpb),
        in_specs=[pl.BlockSpec((tq, 512), lambda b, i: (b * qpb + i, 0)),
                  pl.BlockSpec((seq, 256), lambda b, i: (b, 0)),
                  pl.BlockSpec((seq, 256), lambda b, i: (b, 0))],
        out_specs=pl.BlockSpec((tq, 512), lambda b, i: (b * qpb + i, 0)),
        out_shape=jax.ShapeDtypeStruct((n, BRANCH_W), BF16),
        compiler_params=pltpu.CompilerParams(dimension_semantics=("arbitrary", "arbitrary"),
                                             vmem_limit_bytes=VMEM_LIMIT),
        name="gqa",
    )(qa, ka, va)


def _log_sigmoid(x):
    return jnp.minimum(x, 0.0) - jnp.log1p(jnp.exp(-jnp.abs(x)))


def _lane_scan(x, op, identity, reverse):
    lane = lax.broadcasted_iota(jnp.int32, x.shape, 1)
    k = 1
    while k < LANES:
        if reverse:
            x = op(x, jnp.where(lane < LANES - k, pltpu.roll(x, LANES - k, 1), identity))
        else:
            x = op(x, jnp.where(lane >= k, pltpu.roll(x, k, 1), identity))
        k *= 2
    return x


N_SPLIT = 3
N_COLQ = 2
P_ROWS = 16


def _split3(x):
    hi = x.astype(BF16)
    r = x - hi.astype(F32)
    mid = r.astype(BF16)
    lo = (r - mid.astype(F32)).astype(BF16)
    return hi, mid, lo


def _gate_kernel(raw_ref, bias_ref, a_ref, wk_ref, dec_ref, split_ref):
    nc, nr = raw_ref.shape[2], raw_ref.shape[3]
    L = LSTM_CHUNK
    for d in range(2):
        rev = d == 1
        last = 0 if rev else L - 1
        ig = (raw_ref[0, d] + bias_ref[0, d]).reshape(nc * nr, L)
        lf = _log_sigmoid(raw_ref[1, d] + bias_ref[1, d]).reshape(nc * nr, L)
        b = _lane_scan(lf, jnp.add, 0.0, rev)
        a = ig - b
        g = _lane_scan(a, jnp.maximum, -jnp.inf, rev)
        btot = jnp.broadcast_to(b[:, last:last + 1], b.shape)
        gmax = jnp.broadcast_to(g[:, last:last + 1], g.shape)
        m = jnp.zeros((nr, L), F32)
        m0_rows, m1_rows = [None] * nc, [None] * nc
        for c in (range(nc - 1, -1, -1) if rev else range(nc)):
            m0_rows[c] = m
            m = btot[c * nr:(c + 1) * nr] + jnp.maximum(m, gmax[c * nr:(c + 1) * nr])
            m1_rows[c] = m
        m0 = jnp.concatenate(m0_rows, axis=0)
        m1 = jnp.concatenate(m1_rows, axis=0)
        mx = jnp.maximum(m0, g)
        a_ref[d] = a - m0
        wk_ref[d] = jnp.exp(btot - b + ig - m1)
        dec_ref[d] = jnp.exp(btot + m0 - m1)
        for q, val in enumerate((m0 - mx, -(b + mx))):
            for i, part in enumerate(_split3(val)):
                split_ref[d, q * N_SPLIT + i] = part
        for k in range(N_COLQ * N_SPLIT, P_ROWS):
            split_ref[d, k] = jnp.zeros((nc * nr, L), BF16)


def _gate_prep(raw, bias):
    _, _, nc, nr, L = raw.shape
    rows = jax.ShapeDtypeStruct((2, nc * nr, L), F32)
    return pl.pallas_call(
        _gate_kernel,
        out_shape=(rows, rows, rows, jax.ShapeDtypeStruct((2, P_ROWS, nc * nr, L), BF16)),
        compiler_params=pltpu.CompilerParams(vmem_limit_bytes=VMEM_LIMIT),
        name="mlstm_gates",
    )(raw, bias)


def _mlstm_kernel(lq_ref, lk_ref, lv_ref, a_ref, wk_ref, dec_ref, p_ref, sel_ref,
                  cw_q_ref, cw_k_ref, cb_q_ref, cb_k_ref, o_ref, xq_s, xk_s, q_s, kt_s, ktb_s, c_s):
    seq = lq_ref.shape[0]
    L = LSTM_CHUNK
    nc = seq // L

    zpad = jnp.zeros((CONV_PAD, LANES), F32)
    for x_s, src_ref in ((xq_s, lq_ref), (xk_s, lk_ref)):
        x_s[0:CONV_PAD, :] = zpad
        x_s[CONV_PAD + seq:CONV_PAD + seq + CONV_PAD, :] = zpad
        x_s[CONV_PAD:CONV_PAD + seq, :] = src_ref[...]

    def conv_silu(x_s, w_ref, b_ref, r0):
        acc = jnp.zeros((CONV_ROWS, LANES), F32) + b_ref[0]
        for j in range(CONV_K):
            start = CONV_PAD + r0 + j - CONV_K // 2
            acc = acc + x_s[start:start + CONV_ROWS, :] * w_ref[0, j:j + 1, :]
        return jax.nn.silu(acc)

    n_blocks = seq // CONV_ROWS
    for i in range(n_blocks // 2):
        for blk in (i, n_blocks - 1 - i):
            r0 = blk * CONV_ROWS
            q_s[r0:r0 + CONV_ROWS, :] = conv_silu(xq_s, cw_q_ref, cb_q_ref, r0)
            k = conv_silu(xk_s, cw_k_ref, cb_k_ref, r0) * (HEAD_DIM ** -0.5)
            for j in range(CONV_ROWS // L):
                kt = k[j * L:(j + 1) * L, :].T
                kt_s[r0 // L + j] = kt
                ktb_s[r0 // L + j] = kt.astype(BF16)

    row_i = lax.broadcasted_iota(jnp.int32, (L, L), 0)
    col_i = lax.broadcasted_iota(jnp.int32, (L, L), 1)
    masks = (col_i <= row_i, col_i >= row_i)
    ones = jnp.ones((L, LANES), BF16)
    zeros = jnp.zeros((HEAD_DIM, HEAD_DIM), BF16)
    c_s[...] = jnp.zeros(c_s.shape, F32)

    def chunk_step(d, c, first):
        rows = pl.ds(c * L, L)
        row = pl.ds(c, 1)
        vaug = jnp.concatenate([lv_ref[rows, :], ones], axis=1)
        caug = c_s[d]
        qc = q_s[rows, :]
        qk = _dot(qc.astype(BF16), ktb_s[c])
        e = _dot_tn(p_ref[0, d, c], sel_ref[...])
        dmx = e[:, :LANES]
        w = jnp.where(masks[d], jnp.exp(a_ref[0, d, row, :] + dmx), 0.0)
        s = (qk * w).astype(BF16)
        qw = (qc * jnp.exp(dmx)).astype(BF16)
        kw = (kt_s[c] * wk_ref[0, d, row, :]).astype(BF16)
        lhs = jnp.concatenate([jnp.concatenate([s, qw], axis=1), jnp.concatenate([kw, zeros], axis=1)], axis=0)
        rhs = jnp.concatenate([vaug, caug.astype(BF16)], axis=0)
        r = _dot(lhs, rhs)
        h = r[:L, :LANES] / jnp.maximum(jnp.abs(r[:L, LANES:]), jnp.exp(e[:, LANES:]))
        o_ref[rows, :] = h if first else o_ref[rows, :] + h
        dec = dec_ref[0, d, row, :]
        c_s[d] = jnp.concatenate([dec, dec], axis=1) * caug + r[L:, :]

    for i in range(nc):
        chunk_step(0, i, first=i < nc - 1 - i)
        chunk_step(1, nc - 1 - i, first=nc - 1 - i > i)


def _mlstm(lq, lk, lv, a_rows, wk_rows, dec_rows, ptab, sel, cw, cb, batch, seq):
    n = lq.shape[0]
    nh = LSTM_HEADS
    nc = seq // LSTM_CHUNK
    head_blk = lambda: pl.BlockSpec((seq, LANES), lambda b, h: (b, h))
    rows_blk = lambda: pl.BlockSpec((1, 2, nc, LSTM_CHUNK), lambda b, h: (b * nh + h, 0, 0, 0))
    return pl.pallas_call(
        _mlstm_kernel,
        grid=(batch, nh),
        in_specs=[head_blk(), head_blk(), head_blk(),
                  rows_blk(), rows_blk(), rows_blk(),
                  pl.BlockSpec((1, 2, nc, P_ROWS, LSTM_CHUNK), lambda b, h: (b * nh + h, 0, 0, 0, 0)),
                  pl.BlockSpec((P_ROWS, N_COLQ * LANES), lambda b, h: (0, 0)),
                  pl.BlockSpec((1, CONV_K, LANES), lambda b, h: (h, 0, 0)),
                  pl.BlockSpec((1, CONV_K, LANES), lambda b, h: (nh + h, 0, 0)),
                  pl.BlockSpec((1, 1, LANES), lambda b, h: (h, 0, 0)),
                  pl.BlockSpec((1, 1, LANES), lambda b, h: (nh + h, 0, 0))],
        out_specs=head_blk(),
        out_shape=jax.ShapeDtypeStruct((n, BRANCH_W), F32),
        scratch_shapes=[pltpu.VMEM((seq + 2 * CONV_PAD, LANES), F32),
                        pltpu.VMEM((seq + 2 * CONV_PAD, LANES), F32),
                        pltpu.VMEM((seq, LANES), F32),
                        pltpu.VMEM((nc, HEAD_DIM, LSTM_CHUNK), F32),
                        pltpu.VMEM((nc, HEAD_DIM, LSTM_CHUNK), BF16),
                        pltpu.VMEM((2, HEAD_DIM, 2 * LANES), F32)],
        compiler_params=pltpu.CompilerParams(dimension_semantics=("arbitrary", "arbitrary"),
                                             vmem_limit_bytes=VMEM_LIMIT),
        name="mlstm",
    )(lq, lk, lv, a_rows, wk_rows, dec_rows, ptab, sel, cw, cw, cb, cb)


def _combine_kernel(x_ref, ya_ref, yb_ref, hc_ref, lo_ref, g_ref, hn_ref, wg_ref, bg_ref, wb_ref, wo_ref, o_ref):
    x = x_ref[...]
    xn = _rms(x, g_ref[...]).astype(BF16)
    yc = jnp.concatenate(
        [_rms(hc_ref[:, h * LANES:(h + 1) * LANES], hn_ref[:, h * LANES:(h + 1) * LANES]) for h in range(LSTM_HEADS)],
        axis=1)
    yc = (yc * jax.nn.sigmoid(lo_ref[...])).astype(BF16)
    projs = [_dot(ya_ref[...], wb_ref[0]), _dot(yb_ref[...], wb_ref[1])]
    mix = None
    for i in range(N_BRANCH):
        c0 = i * D_MODEL
        gate = jax.nn.sigmoid(_dot(xn, wg_ref[:, c0:c0 + D_MODEL]) + bg_ref[:, c0:c0 + D_MODEL])
        term = gate * (projs[i] if i < len(projs) else _dot(yc, wb_ref[i]))
        mix = term if mix is None else mix + term
    o_ref[...] = x + _dot(mix.astype(BF16), wo_ref[...])


def _combine(x2, ya, yb, hc, lo, g, hn, wg, bg, wb, wo):
    n = x2.shape[0]
    t = TOK_TILE
    const = lambda *shape: pl.BlockSpec(shape, lambda i: (0,) * len(shape))
    tok = lambda w: pl.BlockSpec((t, w), lambda i: (i, 0))
    return pl.pallas_call(
        _combine_kernel,
        grid=(n // t,),
        in_specs=[tok(D_MODEL), tok(BRANCH_W), tok(BRANCH_W), tok(BRANCH_W), tok(BRANCH_W), const(1, D_MODEL),
                  const(1, BRANCH_W), const(D_MODEL, N_BRANCH * D_MODEL), const(1, N_BRANCH * D_MODEL),
                  const(N_BRANCH, BRANCH_W, D_MODEL), const(D_MODEL, D_MODEL)],
        out_specs=tok(D_MODEL),
        out_shape=jax.ShapeDtypeStruct((n, D_MODEL), F32),
        compiler_params=pltpu.CompilerParams(dimension_semantics=("arbitrary",),
                                             vmem_limit_bytes=VMEM_LIMIT),
        name="combine",
    )(x2, ya, yb, hc, lo, g, hn, wg, bg, wb, wo)


FFN_COL_CHUNK = 256


def _ffn_kernel(x_ref, g_ref, wi_ref, wo_ref, o_ref):
    x = x_ref[...]
    xn = _rms(x, g_ref[...]).astype(BF16)
    acc = x
    for c0 in range(0, FFN_HIDDEN, FFN_COL_CHUNK):
        c1 = min(c0 + FFN_COL_CHUNK, FFN_HIDDEN)
        a = jax.nn.silu(_dot(xn, wi_ref[:, c0:c1])) * _dot(xn, wi_ref[:, FFN_HIDDEN + c0:FFN_HIDDEN + c1])
        acc = acc + _dot(a.astype(BF16), wo_ref[c0:c1, :])
    o_ref[...] = acc


def _ffn(x2, g, wi, wo):
    n = x2.shape[0]
    t = TOK_TILE
    const = lambda *shape: pl.BlockSpec(shape, lambda i: (0,) * len(shape))
    tok = lambda w: pl.BlockSpec((t, w), lambda i: (i, 0))
    return pl.pallas_call(
        _ffn_kernel,
        grid=(n // t,),
        in_specs=[tok(D_MODEL), const(1, D_MODEL), const(D_MODEL, 2 * FFN_HIDDEN), const(FFN_HIDDEN, D_MODEL)],
        out_specs=tok(D_MODEL),
        out_shape=jax.ShapeDtypeStruct((n, D_MODEL), F32),
        compiler_params=pltpu.CompilerParams(dimension_semantics=("arbitrary",),
                                             vmem_limit_bytes=VMEM_LIMIT),
        name="ffn",
    )(x2, g, wi, wo)


def _rope_tables(seq):
    rows = seq // GRID_W
    row = jnp.repeat(jnp.arange(rows, dtype=F32), GRID_W)
    col = jnp.tile(jnp.arange(GRID_W, dtype=F32), rows)
    axis_dim = HEAD_DIM // 2
    freqs = ROPE_THETA ** (-jnp.arange(axis_dim // 2, dtype=F32) * 2.0 / axis_dim)
    ang = jnp.concatenate([row[:, None] * freqs[None], col[:, None] * freqs[None]], axis=-1)
    cos, sin = jnp.cos(ang), jnp.sin(ang)
    cos_full = jnp.repeat(cos, 2, axis=-1)
    sin_signed = jnp.stack([-sin, sin], axis=-1).reshape(seq, HEAD_DIM)
    return cos_full, sin_signed


def _sel_matrix():
    sel = np.zeros((P_ROWS, N_COLQ * LANES), np.float32)
    for q in range(N_COLQ):
        sel[q * N_SPLIT:(q + 1) * N_SPLIT, q * LANES:(q + 1) * LANES] = 1.0
    return jnp.asarray(sel, BF16)


def kernel(x, norm_mix, w_in, sgu_ln_g, sgu_ln_b, sgu_w, sgu_b, q_norm, k_norm, conv_w, conv_b, igate_b, fgate_b,
           lstm_norm, w_gate, b_gate, w_branch, w_out, norm_ffn, w_ffn_in, w_ffn_out):
    batch, seq, d = x.shape
    depth = norm_mix.shape[0]
    nh = LSTM_HEADS
    L = LSTM_CHUNK
    nc = seq // L
    nr = batch * nh
    cos, sin = _rope_tables(seq)
    sel = _sel_matrix()
    x2 = x.reshape(batch * seq, d)
    for l in range(depth):
        w_main = w_in[l, :, :N_MAIN].astype(BF16)
        w_g = w_in[l, :, N_MAIN:].astype(BF16)
        sgub = jnp.repeat(sgu_b[l].T, LANES, axis=1)
        ya, qa, ka, va, lq, lk, lv, lo, grow = _inproj(
            x2, norm_mix[l][None], w_main, w_g.T, sgu_ln_g[l][None], sgu_ln_b[l][None],
            sgu_w[l].astype(BF16), sgub, q_norm[l][None], k_norm[l][None], cos, sin, seq)
        yb = _attention(qa, ka, va, batch, seq)

        raw = grow.reshape(2, 2, nh, batch, nc, L).transpose(0, 1, 4, 3, 2, 5).reshape(2, 2, nc, nr, L)
        gbias = jnp.stack([igate_b[l], fgate_b[l]], axis=0)
        gbias = jnp.broadcast_to(gbias[:, :, None, :, None], (2, 2, batch, nh, L)).reshape(2, 2, nr, L)
        a_rows, wk_rows, dec_rows, splits = _gate_prep(raw, gbias)
        per_head = lambda r: r.reshape(2, nc, nr, L).transpose(2, 0, 1, 3)
        ptab = splits.reshape(2, P_ROWS, nc, nr, L).transpose(3, 0, 2, 1, 4)
        cw = conv_w[l].reshape(CONV_K, 2 * nh, LANES).transpose(1, 0, 2)
        cb = conv_b[l].reshape(2 * nh, 1, LANES)
        hc = _mlstm(lq, lk, lv, per_head(a_rows), per_head(wk_rows), per_head(dec_rows), ptab, sel,
                    cw, cb, batch, seq)

        x2 = _combine(x2, ya, yb, hc, lo, norm_mix[l][None], lstm_norm[l][None], w_gate[l].astype(BF16),
                      b_gate[l][None], w_branch[l].astype(BF16), w_out[l].astype(BF16))
        x2 = _ffn(x2, norm_ffn[l][None], w_ffn_in[l].astype(BF16), w_ffn_out[l].astype(BF16))
    return x2.reshape(batch, seq, d)
```

```python
import functools

import jax
import jax.numpy as jnp
import numpy as np
from jax import lax
from jax.experimental import pallas as pl
from jax.experimental.pallas import tpu as pltpu

D_MODEL = 1024
GRID_W = 64
BRANCH_W = 512
N_BRANCH = 3
EPS = 1e-6
SGU_CHUNK = 128
SGU_GROUPS = 4
ATT_HEADS = 4
ATT_KV_HEADS = 2
HEAD_DIM = 128
ROPE_THETA = 10000.0
LSTM_HEADS = 4
LSTM_CHUNK = 128
CONV_K = 5
FFN_HIDDEN = 2816
N_MAIN = 4096
N_GATE_COLS = 2 * 2 * LSTM_HEADS

LANES = 128
SUBLANES = 8
VMEM_LIMIT = 56 * 1024 * 1024

TOK_TILE = 512
ATT_Q_TILE = 512
ATT_KEY_BLOCK = 512
CONV_ROWS = 256
CONV_PAD = SUBLANES

LOG2_E = 1.4426950408889634

BF16 = jnp.bfloat16
F32 = jnp.float32


def _rms(x, g):
    return x * lax.rsqrt(jnp.mean(x * x, axis=-1, keepdims=True) + EPS) * g


def _dot(a, b):
    return jnp.dot(a, b, preferred_element_type=F32)


def _dot_nt(a, b):
    return lax.dot_general(a, b, (((1,), (1,)), ((), ())), preferred_element_type=F32)


def _dot_tn(a, b):
    return lax.dot_general(a, b, (((0,), (0,)), ((), ())), preferred_element_type=F32)


W_STAGE_BYTES = 2 * 1024 * 1024


def _stage_rows(k, n):
    rows = max(16, min(k, W_STAGE_BYTES // (4 * n)) // 16 * 16)
    while k % rows:
        rows -= 16
    return rows


def _cast_weight(w_hbm, w_vmem, stage, sem):
    k = w_vmem.shape[0]
    rows = stage.shape[1]

    def copy(c, slot):
        return pltpu.make_async_copy(w_hbm.at[pl.ds(c * rows, rows)], stage.at[slot], sem.at[slot])

    copy(0, 0).start()
    for c in range(k // rows):
        slot = c % 2
        if c + 1 < k // rows:
            copy(c + 1, 1 - slot).start()
        copy(c, slot).wait()
        w_vmem[c * rows:(c + 1) * rows, :] = stage[slot].astype(BF16)


def _inproj_kernel(x_ref, g_ref, w_hbm, wgr_ref, lng_ref, lnb_ref, sguw_ref, sgub_ref,
                   qn_ref, kn_ref, cos_ref, sin_ref,
                   ya_ref, qa_ref, ka_ref, va_ref, lq_ref, lk_ref, lv_ref, lo_ref, grow_ref,
                   w_ref, w_stage, w_sem, *, layer):
    t = x_ref.shape[0]

    @pl.when(pl.program_id(0) == 0)
    def _():
        _cast_weight(w_hbm.at[layer, :, pl.ds(0, N_MAIN)], w_ref, w_stage, w_sem)

    xn = _rms(x_ref[...], g_ref[...]).astype(BF16)

    def seg(lo, hi):
        return _dot(xn, w_ref[:, lo:hi])

    gv = jax.nn.gelu(seg(512, 1024))
    mu = jnp.mean(gv, axis=-1, keepdims=True)
    vc = gv - mu
    v = vc * lax.rsqrt(jnp.mean(vc * vc, axis=-1, keepdims=True) + EPS) * lng_ref[...] + lnb_ref[...]
    vb = v.astype(BF16)
    u = jax.nn.gelu(seg(0, 512))

    cos = cos_ref[...]
    sin = sin_ref[...]
    even = (lax.broadcasted_iota(jnp.int32, (t, LANES), 1) % 2) == 0

    def norm_rope(xh, gain):
        xh = _rms(xh, gain)
        partner = jnp.where(even, pltpu.roll(xh, LANES - 1, 1), pltpu.roll(xh, 1, 1))
        return (xh * cos + partner * sin).astype(BF16)

    aq = seg(1024, 1536)
    for h in range(ATT_HEADS):
        qa_ref[:, h * LANES:(h + 1) * LANES] = norm_rope(aq[:, h * LANES:(h + 1) * LANES], qn_ref[...])
    ak = seg(1536, 1792)
    for h in range(ATT_KV_HEADS):
        ka_ref[:, h * LANES:(h + 1) * LANES] = norm_rope(ak[:, h * LANES:(h + 1) * LANES], kn_ref[...])
    va_ref[...] = seg(1792, 2048).astype(BF16)

    lq_ref[...] = seg(2048, 2560)
    lk_ref[...] = seg(2560, 3072)
    lv_ref[...] = seg(3072, 3584).astype(BF16)
    lo_ref[...] = seg(3584, 4096)
    grow_ref[...] = _dot_nt(wgr_ref[...], xn)

    n_chunks = t // SGU_CHUNK
    for grp in range(SGU_GROUPS):
        c0 = grp * LANES
        vcat = jnp.concatenate([vb[j * SGU_CHUNK:(j + 1) * SGU_CHUNK, c0:c0 + LANES] for j in range(n_chunks)], axis=1)
        s = _dot(sguw_ref[grp], vcat)
        for j in range(n_chunks):
            r0 = j * SGU_CHUNK
            sj = s[:, j * LANES:(j + 1) * LANES] + sgub_ref[:, c0:c0 + LANES]
            ya_ref[r0:r0 + SGU_CHUNK, c0:c0 + LANES] = (u[r0:r0 + SGU_CHUNK, c0:c0 + LANES] * sj).astype(BF16)


def _inproj(x2, g, w_in, layer, w_gr, lng, lnb, sguw, sgub, qn, kn, cos, sin, seq):
    n = x2.shape[0]
    t = TOK_TILE
    tiles_per_seq = seq // t
    const = lambda *shape: pl.BlockSpec(shape, lambda i: (0,) * len(shape))
    tok = lambda w: pl.BlockSpec((t, w), lambda i: (i, 0))
    rope = pl.BlockSpec((t, LANES), lambda i: (i % tiles_per_seq, 0))
    out_shapes = (
        jax.ShapeDtypeStruct((n, BRANCH_W), BF16),
        jax.ShapeDtypeStruct((n, 512), BF16),
        jax.ShapeDtypeStruct((n, 256), BF16),
        jax.ShapeDtypeStruct((n, 256), BF16),
        jax.ShapeDtypeStruct((n, BRANCH_W), F32),
        jax.ShapeDtypeStruct((n, BRANCH_W), F32),
        jax.ShapeDtypeStruct((n, BRANCH_W), BF16),
        jax.ShapeDtypeStruct((n, BRANCH_W), F32),
        jax.ShapeDtypeStruct((N_GATE_COLS, n), F32),
    )
    out_specs = (tok(512), tok(512), tok(256), tok(256), tok(512), tok(512), tok(512), tok(512),
                 pl.BlockSpec((N_GATE_COLS, t), lambda i: (0, i)))
    rows = _stage_rows(D_MODEL, N_MAIN)
    return pl.pallas_call(
        functools.partial(_inproj_kernel, layer=layer),
        grid=(n // t,),
        in_specs=[tok(D_MODEL), const(1, D_MODEL), pl.BlockSpec(memory_space=pl.ANY),
                  const(N_GATE_COLS, D_MODEL), const(1, BRANCH_W), const(1, BRANCH_W),
                  const(SGU_GROUPS, SGU_CHUNK, SGU_CHUNK), const(SGU_CHUNK, BRANCH_W),
                  const(1, LANES), const(1, LANES), rope, rope],
        out_specs=out_specs,
        out_shape=out_shapes,
        scratch_shapes=[pltpu.VMEM((D_MODEL, N_MAIN), BF16), pltpu.VMEM((2, rows, N_MAIN), F32),
                        pltpu.SemaphoreType.DMA((2,))],
        compiler_params=pltpu.CompilerParams(dimension_semantics=("arbitrary",),
                                             vmem_limit_bytes=VMEM_LIMIT),
        name="inproj",
    )(x2, g, w_in, w_gr, lng, lnb, sguw, sgub, qn, kn, cos, sin)


def _attn_kernel(q_ref, k_ref, v_ref, o_ref):
    exp2_scale = (HEAD_DIM ** -0.5) * LOG2_E
    rep = ATT_HEADS // ATT_KV_HEADS
    tq = q_ref.shape[0]
    kb = ATT_KEY_BLOCK
    ones = jnp.ones((kb, LANES), BF16)
    qs, ms, accs = [], [], []
    for grp in range(ATT_KV_HEADS):
        c0 = grp * rep * LANES
        qs.append(jnp.concatenate([q_ref[:, c0 + r * LANES:c0 + (r + 1) * LANES] for r in range(rep)], axis=0))
    for j in range(k_ref.shape[0] // kb):
        for grp in range(ATT_KV_HEADS):
            kblk = k_ref[j * kb:(j + 1) * kb, grp * LANES:(grp + 1) * LANES]
            vaug = jnp.concatenate([v_ref[j * kb:(j + 1) * kb, grp * LANES:(grp + 1) * LANES], ones], axis=1)
            s = _dot_nt(qs[grp], kblk)
            bmax = jnp.max(s, axis=-1, keepdims=True)
            if j == 0:
                ms.append(bmax)
                accs.append(_dot(jnp.exp2((s - bmax) * exp2_scale).astype(BF16), vaug))
            else:
                m_new = jnp.maximum(ms[grp], bmax)
                alpha = jnp.exp2((ms[grp] - m_new) * exp2_scale)
                p = jnp.exp2((s - m_new) * exp2_scale).astype(BF16)
                accs[grp] = alpha * accs[grp] + _dot(p, vaug)
                ms[grp] = m_new
    for grp in range(ATT_KV_HEADS):
        c0 = grp * rep * LANES
        o = accs[grp][:, :LANES] / accs[grp][:, LANES:]
        for r in range(rep):
            o_ref[:, c0 + r * LANES:c0 + (r + 1) * LANES] = o[r * tq:(r + 1) * tq].astype(BF16)


def _attention(qa, ka, va, batch, seq):
    n = qa.shape[0]
    tq = ATT_Q_TILE
    qpb = seq // tq
    return pl.pallas_call(
        _attn_kernel,
        grid=(batch, qpb),
        in_specs=[pl.BlockSpec((tq, 512), lambda b, i: (b * qpb + i, 0)),
                  pl.BlockSpec((seq, 256), lambda b, i: (b, 0)),
                  pl.BlockSpec((seq, 256), lambda b, i: (b, 0))],
        out_specs=pl.BlockSpec((tq, 512), lambda b, i: (b * qpb + i, 0)),
        out_shape=jax.ShapeDtypeStruct((n, BRANCH_W), BF16),
        compiler_params=pltpu.CompilerParams(dimension_semantics=("arbitrary", "arbitrary"),
                                             vmem_limit_bytes=VMEM_LIMIT),
        name="gqa",
    )(qa, ka, va)


def _log_sigmoid(x):
    return jnp.minimum(x, 0.0) - jnp.log1p(jnp.exp(-jnp.abs(x)))


def _lane_scan(x, op, identity, reverse):
    lane = lax.broadcasted_iota(jnp.int32, x.shape, 1)
    k = 1
    while k < LANES:
        if reverse:
            x = op(x, jnp.where(lane < LANES - k, pltpu.roll(x, LANES - k, 1), identity))
        else:
            x = op(x, jnp.where(lane >= k, pltpu.roll(x, k, 1), identity))
        k *= 2
    return x


N_SPLIT = 3
N_COLQ = 2
P_ROWS = 16


def _split3(x):
    hi = x.astype(BF16)
    r = x - hi.astype(F32)
    mid = r.astype(BF16)
    lo = (r - mid.astype(F32)).astype(BF16)
    return hi, mid, lo


def _gate_kernel(raw_ref, bias_ref, a_ref, wk_ref, dec_ref, split_ref):
    nc, nr = raw_ref.shape[2], raw_ref.shape[3]
    L = LSTM_CHUNK
    for d in range(2):
        rev = d == 1
        last = 0 if rev else L - 1
        ig = (raw_ref[0, d] + bias_ref[0, d]).reshape(nc * nr, L)
        lf = _log_sigmoid(raw_ref[1, d] + bias_ref[1, d]).reshape(nc * nr, L)
        b = _lane_scan(lf, jnp.add, 0.0, rev)
        a = ig - b
        g = _lane_scan(a, jnp.maximum, -jnp.inf, rev)
        btot = jnp.broadcast_to(b[:, last:last + 1], b.shape)
        gmax = jnp.broadcast_to(g[:, last:last + 1], g.shape)
        m = jnp.zeros((nr, L), F32)
        m0_rows, m1_rows = [None] * nc, [None] * nc
        for c in (range(nc - 1, -1, -1) if rev else range(nc)):
            m0_rows[c] = m
            m = btot[c * nr:(c + 1) * nr] + jnp.maximum(m, gmax[c * nr:(c + 1) * nr])
            m1_rows[c] = m
        m0 = jnp.concatenate(m0_rows, axis=0)
        m1 = jnp.concatenate(m1_rows, axis=0)
        mx = jnp.maximum(m0, g)
        a_ref[d] = a - m0
        wk_ref[d] = jnp.exp(btot - b + ig - m1)
        dec_ref[d] = jnp.exp(btot + m0 - m1)
        for q, val in enumerate((m0 - mx, -(b + mx))):
            for i, part in enumerate(_split3(val)):
                split_ref[d, q * N_SPLIT + i] = part
        for k in range(N_COLQ * N_SPLIT, P_ROWS):
            split_ref[d, k] = jnp.zeros((nc * nr, L), BF16)


def _gate_prep(raw, bias):
    _, _, nc, nr, L = raw.shape
    rows = jax.ShapeDtypeStruct((2, nc * nr, L), F32)
    return pl.pallas_call(
        _gate_kernel,
        out_shape=(rows, rows, rows, jax.ShapeDtypeStruct((2, P_ROWS, nc * nr, L), BF16)),
        compiler_params=pltpu.CompilerParams(vmem_limit_bytes=VMEM_LIMIT),
        name="mlstm_gates",
    )(raw, bias)


def _mlstm_kernel(lq_ref, lk_ref, lv_ref, a_ref, wk_ref, dec_ref, p_ref, sel_ref,
                  cw_q_ref, cw_k_ref, cb_q_ref, cb_k_ref, o_ref, xq_s, xk_s, q_s, kt_s, ktb_s, c_s):
    seq = lq_ref.shape[0]
    L = LSTM_CHUNK
    nc = seq // L

    zpad = jnp.zeros((CONV_PAD, LANES), F32)
    for x_s, src_ref in ((xq_s, lq_ref), (xk_s, lk_ref)):
        x_s[0:CONV_PAD, :] = zpad
        x_s[CONV_PAD + seq:CONV_PAD + seq + CONV_PAD, :] = zpad
        x_s[CONV_PAD:CONV_PAD + seq, :] = src_ref[...]

    def conv_silu(x_s, w_ref, b_ref, r0):
        acc = jnp.zeros((CONV_ROWS, LANES), F32) + b_ref[0]
        for j in range(CONV_K):
            start = CONV_PAD + r0 + j - CONV_K // 2
            acc = acc + x_s[start:start + CONV_ROWS, :] * w_ref[0, j:j + 1, :]
        return jax.nn.silu(acc)

    n_blocks = seq // CONV_ROWS
    for i in range(n_blocks // 2):
        for blk in (i, n_blocks - 1 - i):
            r0 = blk * CONV_ROWS
            q_s[r0:r0 + CONV_ROWS, :] = conv_silu(xq_s, cw_q_ref, cb_q_ref, r0)
            k = conv_silu(xk_s, cw_k_ref, cb_k_ref, r0) * (HEAD_DIM ** -0.5)
            for j in range(CONV_ROWS // L):
                kt = k[j * L:(j + 1) * L, :].T
                kt_s[r0 // L + j] = kt
                ktb_s[r0 // L + j] = kt.astype(BF16)

    row_i = lax.broadcasted_iota(jnp.int32, (L, L), 0)
    col_i = lax.broadcasted_iota(jnp.int32, (L, L), 1)
    masks = (col_i <= row_i, col_i >= row_i)
    ones = jnp.ones((L, LANES), BF16)
    zeros = jnp.zeros((HEAD_DIM, HEAD_DIM), BF16)
    c_s[...] = jnp.zeros(c_s.shape, F32)

    def chunk_step(d, c, first):
        rows = pl.ds(c * L, L)
        row = pl.ds(c, 1)
        vaug = jnp.concatenate([lv_ref[rows, :], ones], axis=1)
        caug = c_s[d]
        qc = q_s[rows, :]
        qk = _dot(qc.astype(BF16), ktb_s[c])
        e = _dot_tn(p_ref[0, d, c], sel_ref[...])
        dmx = e[:, :LANES]
        w = jnp.where(masks[d], jnp.exp(a_ref[0, d, row, :] + dmx), 0.0)
        s = (qk * w).astype(BF16)
        qw = (qc * jnp.exp(dmx)).astype(BF16)
        kw = (kt_s[c] * wk_ref[0, d, row, :]).astype(BF16)
        lhs = jnp.concatenate([jnp.concatenate([s, qw], axis=1), jnp.concatenate([kw, zeros], axis=1)], axis=0)
        rhs = jnp.concatenate([vaug, caug.astype(BF16)], axis=0)
        r = _dot(lhs, rhs)
        h = r[:L, :LANES] / jnp.maximum(jnp.abs(r[:L, LANES:]), jnp.exp(e[:, LANES:]))
        o_ref[rows, :] = h if first else o_ref[rows, :] + h
        dec = dec_ref[0, d, row, :]
        c_s[d] = jnp.concatenate([dec, dec], axis=1) * caug + r[L:, :]

    for i in range(nc):
        chunk_step(0, i, first=i < nc - 1 - i)
        chunk_step(1, nc - 1 - i, first=nc - 1 - i > i)


def _mlstm(lq, lk, lv, a_rows, wk_rows, dec_rows, ptab, sel, cw, cb, batch, seq):
    n = lq.shape[0]
    nh = LSTM_HEADS
    nc = seq // LSTM_CHUNK
    head_blk = lambda: pl.BlockSpec((seq, LANES), lambda b, h: (b, h))
    rows_blk = lambda: pl.BlockSpec((1, 2, nc, LSTM_CHUNK), lambda b, h: (b * nh + h, 0, 0, 0))
    return pl.pallas_call(
        _mlstm_kernel,
        grid=(batch, nh),
        in_specs=[head_blk(), head_blk(), head_blk(),
                  rows_blk(), rows_blk(), rows_blk(),
                  pl.BlockSpec((1, 2, nc, P_ROWS, LSTM_CHUNK), lambda b, h: (b * nh + h, 0, 0, 0, 0)),
                  pl.BlockSpec((P_ROWS, N_COLQ * LANES), lambda b, h: (0, 0)),
                  pl.BlockSpec((1, CONV_K, LANES), lambda b, h: (h, 0, 0)),
                  pl.BlockSpec((1, CONV_K, LANES), lambda b, h: (nh + h, 0, 0)),
                  pl.BlockSpec((1, 1, LANES), lambda b, h: (h, 0, 0)),
                  pl.BlockSpec((1, 1, LANES), lambda b, h: (nh + h, 0, 0))],
        out_specs=head_blk(),
        out_shape=jax.ShapeDtypeStruct((n, BRANCH_W), F32),
        scratch_shapes=[pltpu.VMEM((seq + 2 * CONV_PAD, LANES), F32),
                        pltpu.VMEM((seq + 2 * CONV_PAD, LANES), F32),
                        pltpu.VMEM((seq, LANES), F32),
                        pltpu.VMEM((nc, HEAD_DIM, LSTM_CHUNK), F32),
                        pltpu.VMEM((nc, HEAD_DIM, LSTM_CHUNK), BF16),
                        pltpu.VMEM((2, HEAD_DIM, 2 * LANES), F32)],
        compiler_params=pltpu.CompilerParams(dimension_semantics=("arbitrary", "arbitrary"),
                                             vmem_limit_bytes=VMEM_LIMIT),
        name="mlstm",
    )(lq, lk, lv, a_rows, wk_rows, dec_rows, ptab, sel, cw, cw, cb, cb)


def _combine_kernel(x_ref, ya_ref, yb_ref, hc_ref, lo_ref, g_ref, hn_ref, wg_hbm, bg_ref, wb_hbm, wo_hbm, o_ref,
                    wg_ref, wb_ref, wo_ref, wg_stage, wb_stage, wo_stage, w_sem, *, layer):
    @pl.when(pl.program_id(0) == 0)
    def _():
        _cast_weight(wg_hbm.at[layer], wg_ref, wg_stage, w_sem.at[0])
        for i in range(N_BRANCH):
            _cast_weight(wb_hbm.at[layer, i], wb_ref.at[i], wb_stage, w_sem.at[1])
        _cast_weight(wo_hbm.at[layer], wo_ref, wo_stage, w_sem.at[2])

    x = x_ref[...]
    xn = _rms(x, g_ref[...]).astype(BF16)
    yc = jnp.concatenate(
        [_rms(hc_ref[:, h * LANES:(h + 1) * LANES], hn_ref[:, h * LANES:(h + 1) * LANES]) for h in range(LSTM_HEADS)],
        axis=1)
    yc = (yc * jax.nn.sigmoid(lo_ref[...])).astype(BF16)
    projs = [_dot(ya_ref[...], wb_ref[0]), _dot(yb_ref[...], wb_ref[1])]
    mix = None
    for i in range(N_BRANCH):
        c0 = i * D_MODEL
        gate = jax.nn.sigmoid(_dot(xn, wg_ref[:, c0:c0 + D_MODEL]) + bg_ref[:, c0:c0 + D_MODEL])
        term = gate * (projs[i] if i < len(projs) else _dot(yc, wb_ref[i]))
        mix = term if mix is None else mix + term
    o_ref[...] = x + _dot(mix.astype(BF16), wo_ref[...])


def _combine(x2, ya, yb, hc, lo, g, hn, wg, bg, wb, wo, layer):
    n = x2.shape[0]
    t = TOK_TILE
    const = lambda *shape: pl.BlockSpec(shape, lambda i: (0,) * len(shape))
    tok = lambda w: pl.BlockSpec((t, w), lambda i: (i, 0))
    hbm = lambda: pl.BlockSpec(memory_space=pl.ANY)
    stage = lambda k, n_: pltpu.VMEM((2, _stage_rows(k, n_), n_), F32)
    return pl.pallas_call(
        functools.partial(_combine_kernel, layer=layer),
        grid=(n // t,),
        in_specs=[tok(D_MODEL), tok(BRANCH_W), tok(BRANCH_W), tok(BRANCH_W), tok(BRANCH_W), const(1, D_MODEL),
                  const(1, BRANCH_W), hbm(), const(1, N_BRANCH * D_MODEL), hbm(), hbm()],
        out_specs=tok(D_MODEL),
        out_shape=jax.ShapeDtypeStruct((n, D_MODEL), F32),
        scratch_shapes=[pltpu.VMEM((D_MODEL, N_BRANCH * D_MODEL), BF16),
                        pltpu.VMEM((N_BRANCH, BRANCH_W, D_MODEL), BF16),
                        pltpu.VMEM((D_MODEL, D_MODEL), BF16),
                        stage(D_MODEL, N_BRANCH * D_MODEL), stage(BRANCH_W, D_MODEL), stage(D_MODEL, D_MODEL),
                        pltpu.SemaphoreType.DMA((3, 2))],
        compiler_params=pltpu.CompilerParams(dimension_semantics=("arbitrary",),
                                             vmem_limit_bytes=VMEM_LIMIT),
        name="combine",
    )(x2, ya, yb, hc, lo, g, hn, wg, bg, wb, wo)


FFN_COL_CHUNK = 256


def _ffn_kernel(x_ref, g_ref, wi_hbm, wo_hbm, o_ref, wi_ref, wo_ref, wi_stage, wo_stage, w_sem, *, layer):
    @pl.when(pl.program_id(0) == 0)
    def _():
        _cast_weight(wi_hbm.at[layer], wi_ref, wi_stage, w_sem.at[0])
        _cast_weight(wo_hbm.at[layer], wo_ref, wo_stage, w_sem.at[1])

    x = x_ref[...]
    xn = _rms(x, g_ref[...]).astype(BF16)
    acc = x
    for c0 in range(0, FFN_HIDDEN, FFN_COL_CHUNK):
        c1 = min(c0 + FFN_COL_CHUNK, FFN_HIDDEN)
        a = jax.nn.silu(_dot(xn, wi_ref[:, c0:c1])) * _dot(xn, wi_ref[:, FFN_HIDDEN + c0:FFN_HIDDEN + c1])
        acc = acc + _dot(a.astype(BF16), wo_ref[c0:c1, :])
    o_ref[...] = acc


def _ffn(x2, g, wi, wo, layer):
    n = x2.shape[0]
    t = TOK_TILE
    const = lambda *shape: pl.BlockSpec(shape, lambda i: (0,) * len(shape))
    tok = lambda w: pl.BlockSpec((t, w), lambda i: (i, 0))
    hbm = lambda: pl.BlockSpec(memory_space=pl.ANY)
    stage = lambda k, n_: pltpu.VMEM((2, _stage_rows(k, n_), n_), F32)
    return pl.pallas_call(
        functools.partial(_ffn_kernel, layer=layer),
        grid=(n // t,),
        in_specs=[tok(D_MODEL), const(1, D_MODEL), hbm(), hbm()],
        out_specs=tok(D_MODEL),
        out_shape=jax.ShapeDtypeStruct((n, D_MODEL), F32),
        scratch_shapes=[pltpu.VMEM((D_MODEL, 2 * FFN_HIDDEN), BF16), pltpu.VMEM((FFN_HIDDEN, D_MODEL), BF16),
                        stage(D_MODEL, 2 * FFN_HIDDEN), stage(FFN_HIDDEN, D_MODEL),
                        pltpu.SemaphoreType.DMA((2, 2))],
        compiler_params=pltpu.CompilerParams(dimension_semantics=("arbitrary",),
                                             vmem_limit_bytes=VMEM_LIMIT),
        name="ffn",
    )(x2, g, wi, wo)


def _rope_tables(seq):
    rows = seq // GRID_W
    row = jnp.repeat(jnp.arange(rows, dtype=F32), GRID_W)
    col = jnp.tile(jnp.arange(GRID_W, dtype=F32), rows)
    axis_dim = HEAD_DIM // 2
    freqs = ROPE_THETA ** (-jnp.arange(axis_dim // 2, dtype=F32) * 2.0 / axis_dim)
    ang = jnp.concatenate([row[:, None] * freqs[None], col[:, None] * freqs[None]], axis=-1)
    cos, sin = jnp.cos(ang), jnp.sin(ang)
    cos_full = jnp.repeat(cos, 2, axis=-1)
    sin_signed = jnp.stack([-sin, sin], axis=-1).reshape(seq, HEAD_DIM)
    return cos_full, sin_signed


def _sel_matrix():
    sel = np.zeros((P_ROWS, N_COLQ * LANES), np.float32)
    for q in range(N_COLQ):
        sel[q * N_SPLIT:(q + 1) * N_SPLIT, q * LANES:(q + 1) * LANES] = 1.0
    return jnp.asarray(sel, BF16)


def kernel(x, norm_mix, w_in, sgu_ln_g, sgu_ln_b, sgu_w, sgu_b, q_norm, k_norm, conv_w, conv_b, igate_b, fgate_b,
           lstm_norm, w_gate, b_gate, w_branch, w_out, norm_ffn, w_ffn_in, w_ffn_out):
    batch, seq, d = x.shape
    depth = norm_mix.shape[0]
    nh = LSTM_HEADS
    L = LSTM_CHUNK
    nc = seq // L
    nr = batch * nh
    cos, sin = _rope_tables(seq)
    sel = _sel_matrix()
    x2 = x.reshape(batch * seq, d)
    for l in range(depth):
        w_g = w_in[l, :, N_MAIN:].astype(BF16)
        sgub = jnp.repeat(sgu_b[l].T, LANES, axis=1)
        ya, qa, ka, va, lq, lk, lv, lo, grow = _inproj(
            x2, norm_mix[l][None], w_in, l, w_g.T, sgu_ln_g[l][None], sgu_ln_b[l][None],
            sgu_w[l].astype(BF16), sgub, q_norm[l][None], k_norm[l][None], cos, sin, seq)
        yb = _attention(qa, ka, va, batch, seq)

        raw = grow.reshape(2, 2, nh, batch, nc, L).transpose(0, 1, 4, 3, 2, 5).reshape(2, 2, nc, nr, L)
        gbias = jnp.stack([igate_b[l], fgate_b[l]], axis=0)
        gbias = jnp.broadcast_to(gbias[:, :, None, :, None], (2, 2, batch, nh, L)).reshape(2, 2, nr, L)
        a_rows, wk_rows, dec_rows, splits = _gate_prep(raw, gbias)
        per_head = lambda r: r.reshape(2, nc, nr, L).transpose(2, 0, 1, 3)
        ptab = splits.reshape(2, P_ROWS, nc, nr, L).transpose(3, 0, 2, 1, 4)
        cw = conv_w[l].reshape(CONV_K, 2 * nh, LANES).transpose(1, 0, 2)
        cb = conv_b[l].reshape(2 * nh, 1, LANES)
        hc = _mlstm(lq, lk, lv, per_head(a_rows), per_head(wk_rows), per_head(dec_rows), ptab, sel,
                    cw, cb, batch, seq)

        x2 = _combine(x2, ya, yb, hc, lo, norm_mix[l][None], lstm_norm[l][None], w_gate,
                      b_gate[l][None], w_branch, w_out, l)
        x2 = _ffn(x2, norm_ffn[l][None], w_ffn_in, w_ffn_out, l)
    return x2.reshape(batch, seq, d)
```

```python
import functools

import jax
import jax.numpy as jnp
import numpy as np
from jax import lax
from jax.experimental import pallas as pl
from jax.experimental.pallas import tpu as pltpu

D_MODEL = 1024
GRID_W = 64
BRANCH_W = 512
N_BRANCH = 3
EPS = 1e-6
SGU_CHUNK = 128
SGU_GROUPS = 4
ATT_HEADS = 4
ATT_KV_HEADS = 2
HEAD_DIM = 128
ROPE_THETA = 10000.0
LSTM_HEADS = 4
LSTM_CHUNK = 128
CONV_K = 5
FFN_HIDDEN = 2816
N_MAIN = 4096

LANES = 128
SUBLANES = 8
VMEM_LIMIT = 56 * 1024 * 1024

TOK_TILE = 512
ATT_Q_TILE = 512
ATT_KEY_BLOCK = 512
CONV_ROWS = 256
CONV_PAD = SUBLANES

LOG2_E = 1.4426950408889634

BF16 = jnp.bfloat16
F32 = jnp.float32


def _rms(x, g):
    return x * lax.rsqrt(jnp.mean(x * x, axis=-1, keepdims=True) + EPS) * g


def _dot(a, b):
    return jnp.dot(a, b, preferred_element_type=F32)


def _dot_nt(a, b):
    return lax.dot_general(a, b, (((1,), (1,)), ((), ())), preferred_element_type=F32)


def _dot_tn(a, b):
    return lax.dot_general(a, b, (((0,), (0,)), ((), ())), preferred_element_type=F32)


W_STAGE_BYTES = 2 * 1024 * 1024


def _stage_rows(k, n):
    rows = max(16, min(k, W_STAGE_BYTES // (4 * n)) // 16 * 16)
    while k % rows:
        rows -= 16
    return rows


def _cast_weight(w_hbm, w_vmem, stage, sem):
    k = w_vmem.shape[0]
    rows = stage.shape[1]

    def copy(c, slot):
        return pltpu.make_async_copy(w_hbm.at[pl.ds(c * rows, rows)], stage.at[slot], sem.at[slot])

    copy(0, 0).start()
    for c in range(k // rows):
        slot = c % 2
        if c + 1 < k // rows:
            copy(c + 1, 1 - slot).start()
        copy(c, slot).wait()
        w_vmem[c * rows:(c + 1) * rows, :] = stage[slot].astype(BF16)


def _inproj_kernel(x_ref, g_ref, w_ref, wgr_ref, lng_ref, lnb_ref, sguw_ref, sgub_ref,
                   qn_ref, kn_ref, cos_ref, sin_ref,
                   ya_ref, qa_ref, ka_ref, va_ref, lq_ref, lk_ref, lv_ref, lo_ref, grow_ref):
    t = x_ref.shape[0]
    xn = _rms(x_ref[...], g_ref[...]).astype(BF16)

    def seg(lo, hi):
        return _dot(xn, w_ref[:, lo:hi])

    gv = jax.nn.gelu(seg(512, 1024))
    mu = jnp.mean(gv, axis=-1, keepdims=True)
    vc = gv - mu
    v = vc * lax.rsqrt(jnp.mean(vc * vc, axis=-1, keepdims=True) + EPS) * lng_ref[...] + lnb_ref[...]
    vb = v.astype(BF16)
    u = jax.nn.gelu(seg(0, 512))

    cos = cos_ref[...]
    sin = sin_ref[...]
    even = (lax.broadcasted_iota(jnp.int32, (t, LANES), 1) % 2) == 0

    def norm_rope(xh, gain):
        xh = _rms(xh, gain)
        partner = jnp.where(even, pltpu.roll(xh, LANES - 1, 1), pltpu.roll(xh, 1, 1))
        return (xh * cos + partner * sin).astype(BF16)

    aq = seg(1024, 1536)
    for h in range(ATT_HEADS):
        qa_ref[:, h * LANES:(h + 1) * LANES] = norm_rope(aq[:, h * LANES:(h + 1) * LANES], qn_ref[...])
    ak = seg(1536, 1792)
    for h in range(ATT_KV_HEADS):
        ka_ref[:, h * LANES:(h + 1) * LANES] = norm_rope(ak[:, h * LANES:(h + 1) * LANES], kn_ref[...])
    va_ref[...] = seg(1792, 2048).astype(BF16)

    lq_ref[...] = seg(2048, 2560)
    lk_ref[...] = seg(2560, 3072)
    lv_ref[...] = seg(3072, 3584).astype(BF16)
    lo_ref[...] = seg(3584, 4096)
    gt = _dot_nt(wgr_ref[...].astype(BF16), xn)
    for kd in range(4):
        for j in range(t // LSTM_CHUNK):
            grow_ref[kd // 2, kd % 2, j, 0] = gt[kd * SUBLANES:(kd + 1) * SUBLANES, j * LANES:(j + 1) * LANES]

    n_chunks = t // SGU_CHUNK
    for grp in range(SGU_GROUPS):
        c0 = grp * LANES
        vcat = jnp.concatenate([vb[j * SGU_CHUNK:(j + 1) * SGU_CHUNK, c0:c0 + LANES] for j in range(n_chunks)], axis=1)
        s = _dot(sguw_ref[grp], vcat)
        for j in range(n_chunks):
            r0 = j * SGU_CHUNK
            sj = s[:, j * LANES:(j + 1) * LANES] + sgub_ref[:, c0:c0 + LANES]
            ya_ref[r0:r0 + SGU_CHUNK, c0:c0 + LANES] = (u[r0:r0 + SGU_CHUNK, c0:c0 + LANES] * sj).astype(BF16)


def _inproj(x2, g, w_main, w_gr, lng, lnb, sguw, sgub, qn, kn, cos, sin, seq):
    n = x2.shape[0]
    t = TOK_TILE
    tiles_per_seq = seq // t
    const = lambda *shape: pl.BlockSpec(shape, lambda i: (0,) * len(shape))
    tok = lambda w: pl.BlockSpec((t, w), lambda i: (i, 0))
    rope = pl.BlockSpec((t, LANES), lambda i: (i % tiles_per_seq, 0))
    out_shapes = (
        jax.ShapeDtypeStruct((n, BRANCH_W), BF16),
        jax.ShapeDtypeStruct((n, 512), BF16),
        jax.ShapeDtypeStruct((n, 256), BF16),
        jax.ShapeDtypeStruct((n, 256), BF16),
        jax.ShapeDtypeStruct((n, BRANCH_W), F32),
        jax.ShapeDtypeStruct((n, BRANCH_W), F32),
        jax.ShapeDtypeStruct((n, BRANCH_W), BF16),
        jax.ShapeDtypeStruct((n, BRANCH_W), F32),
        jax.ShapeDtypeStruct((2, 2, seq // LSTM_CHUNK, n // seq, SUBLANES, LSTM_CHUNK), F32),
    )
    out_specs = (tok(512), tok(512), tok(256), tok(256), tok(512), tok(512), tok(512), tok(512),
                 pl.BlockSpec((2, 2, t // LSTM_CHUNK, 1, SUBLANES, LSTM_CHUNK),
                              lambda i: (0, 0, i % tiles_per_seq, i // tiles_per_seq, 0, 0)))
    return pl.pallas_call(
        _inproj_kernel,
        grid=(n // t,),
        in_specs=[tok(D_MODEL), const(1, D_MODEL), const(D_MODEL, N_MAIN),
                  const(4 * SUBLANES, D_MODEL), const(1, BRANCH_W), const(1, BRANCH_W),
                  const(SGU_GROUPS, SGU_CHUNK, SGU_CHUNK), const(SGU_CHUNK, BRANCH_W),
                  const(1, LANES), const(1, LANES), rope, rope],
        out_specs=out_specs,
        out_shape=out_shapes,
        compiler_params=pltpu.CompilerParams(dimension_semantics=("arbitrary",),
                                             vmem_limit_bytes=VMEM_LIMIT),
        name="inproj",
    )(x2, g, w_main, w_gr, lng, lnb, sguw, sgub, qn, kn, cos, sin)


def _attn_kernel(q_ref, k_ref, v_ref, o_ref):
    exp2_scale = (HEAD_DIM ** -0.5) * LOG2_E
    rep = ATT_HEADS // ATT_KV_HEADS
    tq = q_ref.shape[0]
    kb = ATT_KEY_BLOCK
    ones = jnp.ones((kb, LANES), BF16)
    qs, ms, accs = [], [], []
    for grp in range(ATT_KV_HEADS):
        c0 = grp * rep * LANES
        qs.append(jnp.concatenate([q_ref[:, c0 + r * LANES:c0 + (r + 1) * LANES] for r in range(rep)], axis=0))
    for j in range(k_ref.shape[0] // kb):
        for grp in range(ATT_KV_HEADS):
            kblk = k_ref[j * kb:(j + 1) * kb, grp * LANES:(grp + 1) * LANES]
            vaug = jnp.concatenate([v_ref[j * kb:(j + 1) * kb, grp * LANES:(grp + 1) * LANES], ones], axis=1)
            s = _dot_nt(qs[grp], kblk)
            bmax = jnp.max(s, axis=-1, keepdims=True)
            if j == 0:
                ms.append(bmax)
                accs.append(_dot(jnp.exp2((s - bmax) * exp2_scale).astype(BF16), vaug))
            else:
                m_new = jnp.maximum(ms[grp], bmax)
                alpha = jnp.exp2((ms[grp] - m_new) * exp2_scale)
                p = jnp.exp2((s - m_new) * exp2_scale).astype(BF16)
                accs[grp] = alpha * accs[grp] + _dot(p, vaug)
                ms[grp] = m_new
    for grp in range(ATT_KV_HEADS):
        c0 = grp * rep * LANES
        o = accs[grp][:, :LANES] / accs[grp][:, LANES:]
        for r in range(rep):
            o_ref[:, c0 + r * LANES:c0 + (r + 1) * LANES] = o[r * tq:(r + 1) * tq].astype(BF16)


def _attention(qa, ka, va, batch, seq):
    n = qa.shape[0]
    tq = ATT_Q_TILE
    qpb = seq // tq
    return pl.pallas_call(
        _attn_kernel,
        grid=(batch, qpb),
        in_specs=[pl.BlockSpec((tq, 512), lambda b, i: (b * qpb + i, 0)),
                  pl.BlockSpec((seq, 256), lambda b, i: (b, 0)),
                  pl.BlockSpec((seq, 256), lambda b, i: (b, 0))],
        out_specs=pl.BlockSpec((tq, 512), lambda b, i: (b * qpb + i, 0)),
        out_shape=jax.ShapeDtypeStruct((n, BRANCH_W), BF16),
        compiler_params=pltpu.CompilerParams(dimension_semantics=("arbitrary", "arbitrary"),
                                             vmem_limit_bytes=VMEM_LIMIT),
        name="gqa",
    )(qa, ka, va)


def _log_sigmoid(x):
    return jnp.minimum(x, 0.0) - jnp.log1p(jnp.exp(-jnp.abs(x)))


def _lane_scan(x, op, identity, reverse):
    lane = lax.broadcasted_iota(jnp.int32, x.shape, 1)
    k = 1
    while k < LANES:
        if reverse:
            x = op(x, jnp.where(lane < LANES - k, pltpu.roll(x, LANES - k, 1), identity))
        else:
            x = op(x, jnp.where(lane >= k, pltpu.roll(x, k, 1), identity))
        k *= 2
    return x


N_SPLIT = 3
N_COLQ = 2
N_PIECES = N_COLQ * N_SPLIT
N_TAB = N_PIECES + 3
P_ROWS = 16


def _split3(x):
    hi = x.astype(BF16)
    r = x - hi.astype(F32)
    mid = r.astype(BF16)
    lo = (r - mid.astype(F32)).astype(BF16)
    return [hi, mid, lo]


def _gate_kernel(raw_ref, bias_ref, tab_ref, rm_s, bt_s, gm_s, m0_s, m1_s):
    nr = bias_ref.shape[2]
    nc = raw_ref.shape[2] // nr
    L = LSTM_CHUNK
    for d in range(2):
        rev = d == 1
        last = 0 if rev else L - 1
        for k in range(2):
            for r in range(nr):
                rm_s[k, r * nc:(r + 1) * nc, :] = raw_ref[k, d, pl.ds(r, nc, stride=nr), :] + bias_ref[k, d, r:r + 1, :]
        ig = rm_s[0]
        lf = _log_sigmoid(rm_s[1])
        b = _lane_scan(lf, jnp.add, 0.0, rev)
        a = ig - b
        g = _lane_scan(a, jnp.maximum, -jnp.inf, rev)
        bt_s[...] = jnp.broadcast_to(b[:, last:last + 1], b.shape)
        gm_s[...] = jnp.broadcast_to(g[:, last:last + 1], g.shape)
        m = jnp.zeros((nr, L), F32)
        for c in (range(nc - 1, -1, -1) if rev else range(nc)):
            rows = pl.ds(c, nr, stride=nc)
            m0_s[rows, :] = m
            m = bt_s[rows, :] + jnp.maximum(m, gm_s[rows, :])
            m1_s[rows, :] = m
        m0 = m0_s[...]
        m1 = m1_s[...]
        btot = bt_s[...]
        mx = jnp.maximum(m0, g)
        pieces = _split3(m0 - mx) + _split3(-(b + mx))
        for k, part in enumerate(pieces):
            tab_ref[d, k] = part.astype(F32)
        tab_ref[d, N_PIECES] = a - m0
        tab_ref[d, N_PIECES + 1] = jnp.exp(btot - b + ig - m1)
        tab_ref[d, N_PIECES + 2] = jnp.exp(btot + m0 - m1)


def _gate_prep(raw, bias):
    rows, L = raw.shape[2], raw.shape[3]
    buf = lambda *lead: pltpu.VMEM((*lead, rows, L), F32)
    return pl.pallas_call(
        _gate_kernel,
        out_shape=jax.ShapeDtypeStruct((2, N_TAB, rows, L), F32),
        scratch_shapes=[buf(2), buf(), buf(), buf(), buf()],
        compiler_params=pltpu.CompilerParams(vmem_limit_bytes=VMEM_LIMIT),
        name="mlstm_gates",
    )(raw, bias)


def _mlstm_kernel(lq_ref, lk_ref, lv_ref, tab_ref, sel_ref,
                  cw_q_ref, cw_k_ref, cb_q_ref, cb_k_ref, o_ref, xq_s, xk_s, q_s, kt_s, ktb_s, c_s):
    seq = lq_ref.shape[0]
    L = LSTM_CHUNK
    nc = seq // L

    zpad = jnp.zeros((CONV_PAD, LANES), F32)
    for x_s, src_ref in ((xq_s, lq_ref), (xk_s, lk_ref)):
        x_s[0:CONV_PAD, :] = zpad
        x_s[CONV_PAD + seq:CONV_PAD + seq + CONV_PAD, :] = zpad
        x_s[CONV_PAD:CONV_PAD + seq, :] = src_ref[...]

    def conv_silu(x_s, w_ref, b_ref, r0):
        acc = jnp.zeros((CONV_ROWS, LANES), F32) + b_ref[0]
        for j in range(CONV_K):
            start = CONV_PAD + r0 + j - CONV_K // 2
            acc = acc + x_s[start:start + CONV_ROWS, :] * w_ref[0, j:j + 1, :]
        return jax.nn.silu(acc)

    n_blocks = seq // CONV_ROWS
    for i in range(n_blocks // 2):
        for blk in (i, n_blocks - 1 - i):
            r0 = blk * CONV_ROWS
            q_s[r0:r0 + CONV_ROWS, :] = conv_silu(xq_s, cw_q_ref, cb_q_ref, r0)
            k = conv_silu(xk_s, cw_k_ref, cb_k_ref, r0) * (HEAD_DIM ** -0.5)
            for j in range(CONV_ROWS // L):
                kt = k[j * L:(j + 1) * L, :].T
                kt_s[r0 // L + j] = kt
                ktb_s[r0 // L + j] = kt.astype(BF16)

    row_i = lax.broadcasted_iota(jnp.int32, (L, L), 0)
    col_i = lax.broadcasted_iota(jnp.int32, (L, L), 1)
    masks = (col_i <= row_i, col_i >= row_i)
    ones = jnp.ones((L, LANES), BF16)
    zeros = jnp.zeros((HEAD_DIM, HEAD_DIM), BF16)
    pad_rows = jnp.zeros((P_ROWS - N_PIECES, L), F32)
    c_s[...] = jnp.zeros(c_s.shape, F32)

    def chunk_step(d, c, first):
        rows = pl.ds(c * L, L)
        row = pl.ds(c, 1)
        vaug = jnp.concatenate([lv_ref[rows, :], ones], axis=1)
        caug = c_s[d]
        qc = q_s[rows, :]
        qk = _dot(qc.astype(BF16), ktb_s[c])
        tab = [tab_ref[d, k, row, :] for k in range(N_TAB)]
        pieces = jnp.concatenate(tab[:N_PIECES] + [pad_rows], axis=0).astype(BF16)
        e = _dot_tn(pieces, sel_ref[...])
        dmx = e[:, :LANES]
        w = jnp.where(masks[d], jnp.exp(tab[N_PIECES] + dmx), 0.0)
        s = (qk * w).astype(BF16)
        qw = (qc * jnp.exp(dmx)).astype(BF16)
        kw = (kt_s[c] * tab[N_PIECES + 1]).astype(BF16)
        lhs = jnp.concatenate([jnp.concatenate([s, qw], axis=1), jnp.concatenate([kw, zeros], axis=1)], axis=0)
        rhs = jnp.concatenate([vaug, caug.astype(BF16)], axis=0)
        r = _dot(lhs, rhs)
        h = r[:L, :LANES] / jnp.maximum(jnp.abs(r[:L, LANES:]), jnp.exp(e[:, LANES:]))
        o_ref[rows, :] = h if first else o_ref[rows, :] + h
        dec = tab[N_PIECES + 2]
        c_s[d] = jnp.concatenate([dec, dec], axis=1) * caug + r[L:, :]

    for i in range(nc):
        chunk_step(0, i, first=i < nc - 1 - i)
        chunk_step(1, nc - 1 - i, first=nc - 1 - i > i)


def _mlstm(lq, lk, lv, tab, sel, cw, cb, batch, seq):
    n = lq.shape[0]
    nh = LSTM_HEADS
    nc = seq // LSTM_CHUNK
    head_blk = lambda: pl.BlockSpec((seq, LANES), lambda b, h: (b, h))
    return pl.pallas_call(
        _mlstm_kernel,
        grid=(batch, nh),
        in_specs=[head_blk(), head_blk(), head_blk(),
                  pl.BlockSpec((2, N_TAB, nc, LSTM_CHUNK), lambda b, h: (0, 0, b * SUBLANES + h, 0)),
                  pl.BlockSpec((P_ROWS, N_COLQ * LANES), lambda b, h: (0, 0)),
                  pl.BlockSpec((1, CONV_K, LANES), lambda b, h: (h, 0, 0)),
                  pl.BlockSpec((1, CONV_K, LANES), lambda b, h: (nh + h, 0, 0)),
                  pl.BlockSpec((1, 1, LANES), lambda b, h: (h, 0, 0)),
                  pl.BlockSpec((1, 1, LANES), lambda b, h: (nh + h, 0, 0))],
        out_specs=head_blk(),
        out_shape=jax.ShapeDtypeStruct((n, BRANCH_W), F32),
        scratch_shapes=[pltpu.VMEM((seq + 2 * CONV_PAD, LANES), F32),
                        pltpu.VMEM((seq + 2 * CONV_PAD, LANES), F32),
                        pltpu.VMEM((seq, LANES), F32),
                        pltpu.VMEM((nc, HEAD_DIM, LSTM_CHUNK), F32),
                        pltpu.VMEM((nc, HEAD_DIM, LSTM_CHUNK), BF16),
                        pltpu.VMEM((2, HEAD_DIM, 2 * LANES), F32)],
        compiler_params=pltpu.CompilerParams(dimension_semantics=("arbitrary", "arbitrary"),
                                             vmem_limit_bytes=VMEM_LIMIT),
        name="mlstm",
    )(lq, lk, lv, tab, sel, cw, cw, cb, cb)


def _combine_kernel(x_ref, ya_ref, yb_ref, hc_ref, lo_ref, g_ref, hn_ref, wg_hbm, bg_ref, wb_hbm, wo_hbm, o_ref,
                    wg_ref, wb_ref, wo_ref, wg_stage, wb_stage, wo_stage, w_sem, *, layer):
    @pl.when(pl.program_id(0) == 0)
    def _():
        _cast_weight(wg_hbm.at[layer], wg_ref, wg_stage, w_sem.at[0])
        for i in range(N_BRANCH):
            _cast_weight(wb_hbm.at[layer, i], wb_ref.at[i], wb_stage, w_sem.at[1])
        _cast_weight(wo_hbm.at[layer], wo_ref, wo_stage, w_sem.at[2])

    x = x_ref[...]
    xn = _rms(x, g_ref[...]).astype(BF16)
    yc = jnp.concatenate(
        [_rms(hc_ref[:, h * LANES:(h + 1) * LANES], hn_ref[:, h * LANES:(h + 1) * LANES]) for h in range(LSTM_HEADS)],
        axis=1)
    yc = (yc * jax.nn.sigmoid(lo_ref[...])).astype(BF16)
    projs = [_dot(ya_ref[...], wb_ref[0]), _dot(yb_ref[...], wb_ref[1])]
    mix = None
    for i in range(N_BRANCH):
        c0 = i * D_MODEL
        gate = jax.nn.sigmoid(_dot(xn, wg_ref[:, c0:c0 + D_MODEL]) + bg_ref[:, c0:c0 + D_MODEL])
        term = gate * (projs[i] if i < len(projs) else _dot(yc, wb_ref[i]))
        mix = term if mix is None else mix + term
    o_ref[...] = x + _dot(mix.astype(BF16), wo_ref[...])


def _combine(x2, ya, yb, hc, lo, g, hn, wg, bg, wb, wo, layer):
    n = x2.shape[0]
    t = TOK_TILE
    const = lambda *shape: pl.BlockSpec(shape, lambda i: (0,) * len(shape))
    tok = lambda w: pl.BlockSpec((t, w), lambda i: (i, 0))
    hbm = lambda: pl.BlockSpec(memory_space=pl.ANY)
    stage = lambda k, n_: pltpu.VMEM((2, _stage_rows(k, n_), n_), F32)
    return pl.pallas_call(
        functools.partial(_combine_kernel, layer=layer),
        grid=(n // t,),
        in_specs=[tok(D_MODEL), tok(BRANCH_W), tok(BRANCH_W), tok(BRANCH_W), tok(BRANCH_W), const(1, D_MODEL),
                  const(1, BRANCH_W), hbm(), const(1, N_BRANCH * D_MODEL), hbm(), hbm()],
        out_specs=tok(D_MODEL),
        out_shape=jax.ShapeDtypeStruct((n, D_MODEL), F32),
        scratch_shapes=[pltpu.VMEM((D_MODEL, N_BRANCH * D_MODEL), BF16),
                        pltpu.VMEM((N_BRANCH, BRANCH_W, D_MODEL), BF16),
                        pltpu.VMEM((D_MODEL, D_MODEL), BF16),
                        stage(D_MODEL, N_BRANCH * D_MODEL), stage(BRANCH_W, D_MODEL), stage(D_MODEL, D_MODEL),
                        pltpu.SemaphoreType.DMA((3, 2))],
        compiler_params=pltpu.CompilerParams(dimension_semantics=("arbitrary",),
                                             vmem_limit_bytes=VMEM_LIMIT),
        name="combine",
    )(x2, ya, yb, hc, lo, g, hn, wg, bg, wb, wo)


FFN_COL_CHUNK = 256


def _ffn_kernel(x_ref, g_ref, wi_hbm, wo_hbm, o_ref, wi_ref, wo_ref, wi_stage, wo_stage, w_sem, *, layer):
    @pl.when(pl.program_id(0) == 0)
    def _():
        _cast_weight(wi_hbm.at[layer], wi_ref, wi_stage, w_sem.at[0])
        _cast_weight(wo_hbm.at[layer], wo_ref, wo_stage, w_sem.at[1])

    x = x_ref[...]
    xn = _rms(x, g_ref[...]).astype(BF16)
    acc = x
    for c0 in range(0, FFN_HIDDEN, FFN_COL_CHUNK):
        c1 = min(c0 + FFN_COL_CHUNK, FFN_HIDDEN)
        a = jax.nn.silu(_dot(xn, wi_ref[:, c0:c1])) * _dot(xn, wi_ref[:, FFN_HIDDEN + c0:FFN_HIDDEN + c1])
        acc = acc + _dot(a.astype(BF16), wo_ref[c0:c1, :])
    o_ref[...] = acc


def _ffn(x2, g, wi, wo, layer):
    n = x2.shape[0]
    t = TOK_TILE
    const = lambda *shape: pl.BlockSpec(shape, lambda i: (0,) * len(shape))
    tok = lambda w: pl.BlockSpec((t, w), lambda i: (i, 0))
    hbm = lambda: pl.BlockSpec(memory_space=pl.ANY)
    stage = lambda k, n_: pltpu.VMEM((2, _stage_rows(k, n_), n_), F32)
    return pl.pallas_call(
        functools.partial(_ffn_kernel, layer=layer),
        grid=(n // t,),
        in_specs=[tok(D_MODEL), const(1, D_MODEL), hbm(), hbm()],
        out_specs=tok(D_MODEL),
        out_shape=jax.ShapeDtypeStruct((n, D_MODEL), F32),
        scratch_shapes=[pltpu.VMEM((D_MODEL, 2 * FFN_HIDDEN), BF16), pltpu.VMEM((FFN_HIDDEN, D_MODEL), BF16),
                        stage(D_MODEL, 2 * FFN_HIDDEN), stage(FFN_HIDDEN, D_MODEL),
                        pltpu.SemaphoreType.DMA((2, 2))],
        compiler_params=pltpu.CompilerParams(dimension_semantics=("arbitrary",),
                                             vmem_limit_bytes=VMEM_LIMIT),
        name="ffn",
    )(x2, g, wi, wo)


def _rope_tables(seq):
    rows = seq // GRID_W
    row = jnp.repeat(jnp.arange(rows, dtype=F32), GRID_W)
    col = jnp.tile(jnp.arange(GRID_W, dtype=F32), rows)
    axis_dim = HEAD_DIM // 2
    freqs = ROPE_THETA ** (-jnp.arange(axis_dim // 2, dtype=F32) * 2.0 / axis_dim)
    ang = jnp.concatenate([row[:, None] * freqs[None], col[:, None] * freqs[None]], axis=-1)
    cos, sin = jnp.cos(ang), jnp.sin(ang)
    cos_full = jnp.repeat(cos, 2, axis=-1)
    sin_signed = jnp.stack([-sin, sin], axis=-1).reshape(seq, HEAD_DIM)
    return cos_full, sin_signed


def _sel_matrix():
    sel = np.zeros((P_ROWS, N_COLQ * LANES), np.float32)
    for q in range(N_COLQ):
        sel[q * N_SPLIT:(q + 1) * N_SPLIT, q * LANES:(q + 1) * LANES] = 1.0
    return jnp.asarray(sel, BF16)


def kernel(x, norm_mix, w_in, sgu_ln_g, sgu_ln_b, sgu_w, sgu_b, q_norm, k_norm, conv_w, conv_b, igate_b, fgate_b,
           lstm_norm, w_gate, b_gate, w_branch, w_out, norm_ffn, w_ffn_in, w_ffn_out):
    batch, seq, d = x.shape
    depth = norm_mix.shape[0]
    nh = LSTM_HEADS
    L = LSTM_CHUNK
    nc = seq // L
    cos, sin = _rope_tables(seq)
    sel = _sel_matrix()
    x2 = x.reshape(batch * seq, d)
    for l in range(depth):
        w_main = w_in[l, :, :N_MAIN].astype(BF16)
        w_g = jnp.pad(w_in[l, :, N_MAIN:].T.reshape(4, nh, d), ((0, 0), (0, SUBLANES - nh), (0, 0))).reshape(4 * SUBLANES, d)
        sgub = jnp.repeat(sgu_b[l].T, LANES, axis=1)
        ya, qa, ka, va, lq, lk, lv, lo, graw = _inproj(
            x2, norm_mix[l][None], w_main, w_g, sgu_ln_g[l][None], sgu_ln_b[l][None],
            sgu_w[l].astype(BF16), sgub, q_norm[l][None], k_norm[l][None], cos, sin, seq)
        yb = _attention(qa, ka, va, batch, seq)

        gbias = jnp.pad(jnp.stack([igate_b[l], fgate_b[l]], axis=0), ((0, 0), (0, 0), (0, SUBLANES - nh)))
        gbias = jnp.broadcast_to(gbias[:, :, None, :, None], (2, 2, batch, SUBLANES, L))
        tab = _gate_prep(graw.reshape(2, 2, nc * batch * SUBLANES, L), gbias.reshape(2, 2, batch * SUBLANES, L))
        cw = conv_w[l].reshape(CONV_K, 2 * nh, LANES).transpose(1, 0, 2)
        cb = conv_b[l].reshape(2 * nh, 1, LANES)
        hc = _mlstm(lq, lk, lv, tab, sel, cw, cb, batch, seq)

        x2 = _combine(x2, ya, yb, hc, lo, norm_mix[l][None], lstm_norm[l][None], w_gate,
                      b_gate[l][None], w_branch, w_out, l)
        x2 = _ffn(x2, norm_ffn[l][None], w_ffn_in, w_ffn_out, l)
    return x2.reshape(batch, seq, d)
```

```python
import functools

import jax
import jax.numpy as jnp
import numpy as np
from jax import lax
from jax.experimental import pallas as pl
from jax.experimental.pallas import tpu as pltpu

D_MODEL = 1024
GRID_W = 64
BRANCH_W = 512
N_BRANCH = 3
EPS = 1e-6
SGU_CHUNK = 128
SGU_GROUPS = 4
ATT_HEADS = 4
ATT_KV_HEADS = 2
HEAD_DIM = 128
ROPE_THETA = 10000.0
LSTM_HEADS = 4
LSTM_CHUNK = 128
CONV_K = 5
FFN_HIDDEN = 2816
N_MAIN = 4096

LANES = 128
SUBLANES = 8
VMEM_LIMIT = 56 * 1024 * 1024

TOK_TILE = 512
ATT_Q_TILE = 512
ATT_KEY_BLOCK = 512
CONV_ROWS = 256
CONV_PAD = SUBLANES

LOG2_E = 1.4426950408889634

BF16 = jnp.bfloat16
F32 = jnp.float32


def _rms(x, g):
    return x * lax.rsqrt(jnp.mean(x * x, axis=-1, keepdims=True) + EPS) * g


def _dot(a, b):
    return jnp.dot(a, b, preferred_element_type=F32)


def _dot_nt(a, b):
    return lax.dot_general(a, b, (((1,), (1,)), ((), ())), preferred_element_type=F32)


def _dot_tn(a, b):
    return lax.dot_general(a, b, (((0,), (0,)), ((), ())), preferred_element_type=F32)


W_STAGE_BYTES = 2 * 1024 * 1024


def _stage_rows(k, n):
    rows = max(16, min(k, W_STAGE_BYTES // (4 * n)) // 16 * 16)
    while k % rows:
        rows -= 16
    return rows


def _cast_weight(w_hbm, w_vmem, stage, sem):
    k = w_vmem.shape[0]
    rows = stage.shape[1]

    def copy(c, slot):
        return pltpu.make_async_copy(w_hbm.at[pl.ds(c * rows, rows)], stage.at[slot], sem.at[slot])

    copy(0, 0).start()
    for c in range(k // rows):
        slot = c % 2
        if c + 1 < k // rows:
            copy(c + 1, 1 - slot).start()
        copy(c, slot).wait()
        w_vmem[c * rows:(c + 1) * rows, :] = stage[slot].astype(BF16)


def _inproj_kernel(x_ref, g_ref, w_ref, wgr_ref, lng_ref, lnb_ref, sguw_ref, sgub_ref,
                   qn_ref, kn_ref, cos_ref, sin_ref,
                   ya_ref, qa_ref, ka_ref, va_ref, lq_ref, lk_ref, lv_ref, lo_ref, grow_ref):
    t = x_ref.shape[0]
    xn = _rms(x_ref[...], g_ref[...]).astype(BF16)

    def seg(lo, hi):
        return _dot(xn, w_ref[:, lo:hi])

    gv = jax.nn.gelu(seg(512, 1024))
    mu = jnp.mean(gv, axis=-1, keepdims=True)
    vc = gv - mu
    v = vc * lax.rsqrt(jnp.mean(vc * vc, axis=-1, keepdims=True) + EPS) * lng_ref[...] + lnb_ref[...]
    vb = v.astype(BF16)
    u = jax.nn.gelu(seg(0, 512))

    cos = cos_ref[...]
    sin = sin_ref[...]
    even = (lax.broadcasted_iota(jnp.int32, (t, LANES), 1) % 2) == 0

    def norm_rope(xh, gain):
        xh = _rms(xh, gain)
        partner = jnp.where(even, pltpu.roll(xh, LANES - 1, 1), pltpu.roll(xh, 1, 1))
        return (xh * cos + partner * sin).astype(BF16)

    aq = seg(1024, 1536)
    for h in range(ATT_HEADS):
        qa_ref[:, h * LANES:(h + 1) * LANES] = norm_rope(aq[:, h * LANES:(h + 1) * LANES], qn_ref[...])
    ak = seg(1536, 1792)
    for h in range(ATT_KV_HEADS):
        ka_ref[:, h * LANES:(h + 1) * LANES] = norm_rope(ak[:, h * LANES:(h + 1) * LANES], kn_ref[...])
    va_ref[...] = seg(1792, 2048).astype(BF16)

    lq_ref[...] = seg(2048, 2560)
    lk_ref[...] = seg(2560, 3072)
    lv_ref[...] = seg(3072, 3584).astype(BF16)
    lo_ref[...] = seg(3584, 4096)
    gt = _dot_nt(wgr_ref[...], xn)
    for kd in range(4):
        for j in range(t // LSTM_CHUNK):
            grow_ref[kd // 2, kd % 2, j, 0] = gt[kd * SUBLANES:(kd + 1) * SUBLANES, j * LANES:(j + 1) * LANES]

    n_chunks = t // SGU_CHUNK
    for grp in range(SGU_GROUPS):
        c0 = grp * LANES
        vcat = jnp.concatenate([vb[j * SGU_CHUNK:(j + 1) * SGU_CHUNK, c0:c0 + LANES] for j in range(n_chunks)], axis=1)
        s = _dot(sguw_ref[grp], vcat)
        for j in range(n_chunks):
            r0 = j * SGU_CHUNK
            sj = s[:, j * LANES:(j + 1) * LANES] + sgub_ref[:, c0:c0 + LANES]
            ya_ref[r0:r0 + SGU_CHUNK, c0:c0 + LANES] = (u[r0:r0 + SGU_CHUNK, c0:c0 + LANES] * sj).astype(BF16)


def _inproj(x2, g, w_in_b, layer, w_gr, lng, lnb, sguw, sgub, qn, kn, cos, sin, seq):
    n = x2.shape[0]
    t = TOK_TILE
    tiles_per_seq = seq // t
    const = lambda *shape: pl.BlockSpec(shape, lambda i: (0,) * len(shape))
    tok = lambda w: pl.BlockSpec((t, w), lambda i: (i, 0))
    rope = pl.BlockSpec((t, LANES), lambda i: (i % tiles_per_seq, 0))
    out_shapes = (
        jax.ShapeDtypeStruct((n, BRANCH_W), BF16),
        jax.ShapeDtypeStruct((n, 512), BF16),
        jax.ShapeDtypeStruct((n, 256), BF16),
        jax.ShapeDtypeStruct((n, 256), BF16),
        jax.ShapeDtypeStruct((n, BRANCH_W), F32),
        jax.ShapeDtypeStruct((n, BRANCH_W), F32),
        jax.ShapeDtypeStruct((n, BRANCH_W), BF16),
        jax.ShapeDtypeStruct((n, BRANCH_W), F32),
        jax.ShapeDtypeStruct((2, 2, seq // LSTM_CHUNK, n // seq, SUBLANES, LSTM_CHUNK), F32),
    )
    out_specs = (tok(512), tok(512), tok(256), tok(256), tok(512), tok(512), tok(512), tok(512),
                 pl.BlockSpec((2, 2, t // LSTM_CHUNK, 1, SUBLANES, LSTM_CHUNK),
                              lambda i: (0, 0, i % tiles_per_seq, i // tiles_per_seq, 0, 0)))
    return pl.pallas_call(
        _inproj_kernel,
        grid=(n // t,),
        in_specs=[tok(D_MODEL), const(1, D_MODEL),
                  pl.BlockSpec((None, D_MODEL, N_MAIN), lambda i: (layer, 0, 0)),
                  const(4 * SUBLANES, D_MODEL), const(1, BRANCH_W), const(1, BRANCH_W),
                  const(SGU_GROUPS, SGU_CHUNK, SGU_CHUNK), const(SGU_CHUNK, BRANCH_W),
                  const(1, LANES), const(1, LANES), rope, rope],
        out_specs=out_specs,
        out_shape=out_shapes,
        compiler_params=pltpu.CompilerParams(dimension_semantics=("arbitrary",),
                                             vmem_limit_bytes=VMEM_LIMIT),
        name="inproj",
    )(x2, g, w_in_b, w_gr, lng, lnb, sguw, sgub, qn, kn, cos, sin)


def _attn_kernel(q_ref, k_ref, v_ref, o_ref):
    exp2_scale = (HEAD_DIM ** -0.5) * LOG2_E
    rep = ATT_HEADS // ATT_KV_HEADS
    tq = q_ref.shape[0]
    kb = ATT_KEY_BLOCK
    ones = jnp.ones((kb, LANES), BF16)
    qs, ms, accs = [], [], []
    for grp in range(ATT_KV_HEADS):
        c0 = grp * rep * LANES
        qs.append(jnp.concatenate([q_ref[:, c0 + r * LANES:c0 + (r + 1) * LANES] for r in range(rep)], axis=0))
    for j in range(k_ref.shape[0] // kb):
        for grp in range(ATT_KV_HEADS):
            kblk = k_ref[j * kb:(j + 1) * kb, grp * LANES:(grp + 1) * LANES]
            vaug = jnp.concatenate([v_ref[j * kb:(j + 1) * kb, grp * LANES:(grp + 1) * LANES], ones], axis=1)
            s = _dot_nt(qs[grp], kblk)
            bmax = jnp.max(s, axis=-1, keepdims=True)
            if j == 0:
                ms.append(bmax)
                accs.append(_dot(jnp.exp2((s - bmax) * exp2_scale).astype(BF16), vaug))
            else:
                m_new = jnp.maximum(ms[grp], bmax)
                alpha = jnp.exp2((ms[grp] - m_new) * exp2_scale)
                p = jnp.exp2((s - m_new) * exp2_scale).astype(BF16)
                accs[grp] = alpha * accs[grp] + _dot(p, vaug)
                ms[grp] = m_new
    for grp in range(ATT_KV_HEADS):
        c0 = grp * rep * LANES
        o = accs[grp][:, :LANES] / accs[grp][:, LANES:]
        for r in range(rep):
            o_ref[:, c0 + r * LANES:c0 + (r + 1) * LANES] = o[r * tq:(r + 1) * tq].astype(BF16)


def _attention(qa, ka, va, batch, seq):
    n = qa.shape[0]
    tq = ATT_Q_TILE
    qpb = seq // tq
    return pl.pallas_call(
        _attn_kernel,
        grid=(batch, qpb),
        in_specs=[pl.BlockSpec((tq, 512), lambda b, i: (b * qpb + i, 0)),
                  pl.BlockSpec((seq, 256), lambda b, i: (b, 0)),
                  pl.BlockSpec((seq, 256), lambda b, i: (b, 0))],
        out_specs=pl.BlockSpec((tq, 512), lambda b, i: (b * qpb + i, 0)),
        out_shape=jax.ShapeDtypeStruct((n, BRANCH_W), BF16),
        compiler_params=pltpu.CompilerParams(dimension_semantics=("arbitrary", "arbitrary"),
                                             vmem_limit_bytes=VMEM_LIMIT),
        name="gqa",
    )(qa, ka, va)


def _log_sigmoid(x):
    return jnp.minimum(x, 0.0) - jnp.log1p(jnp.exp(-jnp.abs(x)))


def _lane_scan(x, op, identity, reverse):
    lane = lax.broadcasted_iota(jnp.int32, x.shape, 1)
    k = 1
    while k < LANES:
        if reverse:
            x = op(x, jnp.where(lane < LANES - k, pltpu.roll(x, LANES - k, 1), identity))
        else:
            x = op(x, jnp.where(lane >= k, pltpu.roll(x, k, 1), identity))
        k *= 2
    return x


N_SPLIT = 3
N_COLQ = 2
N_PIECES = N_COLQ * N_SPLIT
N_TAB = N_PIECES + 3
P_ROWS = 16


def _split3(x):
    hi = x.astype(BF16)
    r = x - hi.astype(F32)
    mid = r.astype(BF16)
    lo = (r - mid.astype(F32)).astype(BF16)
    return [hi, mid, lo]


def _gate_kernel(raw_ref, bias_ref, tab_ref, rm_s, bt_s, gm_s, m0_s, m1_s):
    nr = bias_ref.shape[2]
    nc = raw_ref.shape[2] // nr
    L = LSTM_CHUNK
    for d in range(2):
        rev = d == 1
        last = 0 if rev else L - 1
        for k in range(2):
            for r in range(nr):
                rm_s[k, r * nc:(r + 1) * nc, :] = raw_ref[k, d, pl.ds(r, nc, stride=nr), :] + bias_ref[k, d, r:r + 1, :]
        ig = rm_s[0]
        lf = _log_sigmoid(rm_s[1])
        b = _lane_scan(lf, jnp.add, 0.0, rev)
        a = ig - b
        g = _lane_scan(a, jnp.maximum, -jnp.inf, rev)
        bt_s[...] = jnp.broadcast_to(b[:, last:last + 1], b.shape)
        gm_s[...] = jnp.broadcast_to(g[:, last:last + 1], g.shape)
        m = jnp.zeros((nr, L), F32)
        for c in (range(nc - 1, -1, -1) if rev else range(nc)):
            rows = pl.ds(c, nr, stride=nc)
            m0_s[rows, :] = m
            m = bt_s[rows, :] + jnp.maximum(m, gm_s[rows, :])
            m1_s[rows, :] = m
        m0 = m0_s[...]
        m1 = m1_s[...]
        btot = bt_s[...]
        mx = jnp.maximum(m0, g)
        pieces = _split3(m0 - mx) + _split3(-(b + mx))
        for k, part in enumerate(pieces):
            tab_ref[d, k] = part.astype(F32)
        tab_ref[d, N_PIECES] = a - m0
        tab_ref[d, N_PIECES + 1] = jnp.exp(btot - b + ig - m1)
        tab_ref[d, N_PIECES + 2] = jnp.exp(btot + m0 - m1)


def _gate_prep(raw, bias):
    rows, L = raw.shape[2], raw.shape[3]
    buf = lambda *lead: pltpu.VMEM((*lead, rows, L), F32)
    return pl.pallas_call(
        _gate_kernel,
        out_shape=jax.ShapeDtypeStruct((2, N_TAB, rows, L), F32),
        scratch_shapes=[buf(2), buf(), buf(), buf(), buf()],
        compiler_params=pltpu.CompilerParams(vmem_limit_bytes=VMEM_LIMIT),
        name="mlstm_gates",
    )(raw, bias)


def _mlstm_kernel(lq_ref, lk_ref, lv_ref, tab_ref, sel_ref,
                  cw_q_ref, cw_k_ref, cb_q_ref, cb_k_ref, o_ref, xq_s, xk_s, q_s, kt_s, ktb_s, c_s):
    seq = lq_ref.shape[0]
    L = LSTM_CHUNK
    nc = seq // L

    zpad = jnp.zeros((CONV_PAD, LANES), F32)
    for x_s, src_ref in ((xq_s, lq_ref), (xk_s, lk_ref)):
        x_s[0:CONV_PAD, :] = zpad
        x_s[CONV_PAD + seq:CONV_PAD + seq + CONV_PAD, :] = zpad
        x_s[CONV_PAD:CONV_PAD + seq, :] = src_ref[...]

    def conv_silu(x_s, w_ref, b_ref, r0):
        acc = jnp.zeros((CONV_ROWS, LANES), F32) + b_ref[0]
        for j in range(CONV_K):
            start = CONV_PAD + r0 + j - CONV_K // 2
            acc = acc + x_s[start:start + CONV_ROWS, :] * w_ref[0, j:j + 1, :]
        return jax.nn.silu(acc)

    n_blocks = seq // CONV_ROWS
    for i in range(n_blocks // 2):
        for blk in (i, n_blocks - 1 - i):
            r0 = blk * CONV_ROWS
            q_s[r0:r0 + CONV_ROWS, :] = conv_silu(xq_s, cw_q_ref, cb_q_ref, r0)
            k = conv_silu(xk_s, cw_k_ref, cb_k_ref, r0) * (HEAD_DIM ** -0.5)
            for j in range(CONV_ROWS // L):
                kt = k[j * L:(j + 1) * L, :].T
                kt_s[r0 // L + j] = kt
                ktb_s[r0 // L + j] = kt.astype(BF16)

    row_i = lax.broadcasted_iota(jnp.int32, (L, L), 0)
    col_i = lax.broadcasted_iota(jnp.int32, (L, L), 1)
    masks = (col_i <= row_i, col_i >= row_i)
    ones = jnp.ones((L, LANES), BF16)
    zeros = jnp.zeros((HEAD_DIM, HEAD_DIM), BF16)
    pad_rows = jnp.zeros((P_ROWS - N_PIECES, L), F32)
    c_s[...] = jnp.zeros(c_s.shape, F32)

    def chunk_step(d, c, first):
        rows = pl.ds(c * L, L)
        row = pl.ds(c, 1)
        vaug = jnp.concatenate([lv_ref[rows, :], ones], axis=1)
        caug = c_s[d]
        qc = q_s[rows, :]
        qk = _dot(qc.astype(BF16), ktb_s[c])
        tab = [tab_ref[d, k, row, :] for k in range(N_TAB)]
        pieces = jnp.concatenate(tab[:N_PIECES] + [pad_rows], axis=0).astype(BF16)
        e = _dot_tn(pieces, sel_ref[...])
        dmx = e[:, :LANES]
        w = jnp.where(masks[d], jnp.exp(tab[N_PIECES] + dmx), 0.0)
        s = (qk * w).astype(BF16)
        qw = (qc * jnp.exp(dmx)).astype(BF16)
        kw = (kt_s[c] * tab[N_PIECES + 1]).astype(BF16)
        lhs = jnp.concatenate([jnp.concatenate([s, qw], axis=1), jnp.concatenate([kw, zeros], axis=1)], axis=0)
        rhs = jnp.concatenate([vaug, caug.astype(BF16)], axis=0)
        r = _dot(lhs, rhs)
        h = r[:L, :LANES] / jnp.maximum(jnp.abs(r[:L, LANES:]), jnp.exp(e[:, LANES:]))
        o_ref[rows, :] = h if first else o_ref[rows, :] + h
        dec = tab[N_PIECES + 2]
        c_s[d] = jnp.concatenate([dec, dec], axis=1) * caug + r[L:, :]

    for i in range(nc):
        chunk_step(0, i, first=i < nc - 1 - i)
        chunk_step(1, nc - 1 - i, first=nc - 1 - i > i)


def _mlstm(lq, lk, lv, tab, sel, cw, cb, batch, seq):
    n = lq.shape[0]
    nh = LSTM_HEADS
    nc = seq // LSTM_CHUNK
    head_blk = lambda: pl.BlockSpec((seq, LANES), lambda b, h: (b, h))
    return pl.pallas_call(
        _mlstm_kernel,
        grid=(batch, nh),
        in_specs=[head_blk(), head_blk(), head_blk(),
                  pl.BlockSpec((2, N_TAB, nc, LSTM_CHUNK), lambda b, h: (0, 0, b * SUBLANES + h, 0)),
                  pl.BlockSpec((P_ROWS, N_COLQ * LANES), lambda b, h: (0, 0)),
                  pl.BlockSpec((1, CONV_K, LANES), lambda b, h: (h, 0, 0)),
                  pl.BlockSpec((1, CONV_K, LANES), lambda b, h: (nh + h, 0, 0)),
                  pl.BlockSpec((1, 1, LANES), lambda b, h: (h, 0, 0)),
                  pl.BlockSpec((1, 1, LANES), lambda b, h: (nh + h, 0, 0))],
        out_specs=head_blk(),
        out_shape=jax.ShapeDtypeStruct((n, BRANCH_W), F32),
        scratch_shapes=[pltpu.VMEM((seq + 2 * CONV_PAD, LANES), F32),
                        pltpu.VMEM((seq + 2 * CONV_PAD, LANES), F32),
                        pltpu.VMEM((seq, LANES), F32),
                        pltpu.VMEM((nc, HEAD_DIM, LSTM_CHUNK), F32),
                        pltpu.VMEM((nc, HEAD_DIM, LSTM_CHUNK), BF16),
                        pltpu.VMEM((2, HEAD_DIM, 2 * LANES), F32)],
        compiler_params=pltpu.CompilerParams(dimension_semantics=("arbitrary", "arbitrary"),
                                             vmem_limit_bytes=VMEM_LIMIT),
        name="mlstm",
    )(lq, lk, lv, tab, sel, cw, cw, cb, cb)


def _combine_kernel(x_ref, ya_ref, yb_ref, hc_ref, lo_ref, g_ref, hn_ref, wg_hbm, bg_ref, wb_hbm, wo_hbm, o_ref,
                    wg_ref, wb_ref, wo_ref, wg_stage, wb_stage, wo_stage, w_sem, *, layer):
    @pl.when(pl.program_id(0) == 0)
    def _():
        _cast_weight(wg_hbm.at[layer], wg_ref, wg_stage, w_sem.at[0])
        for i in range(N_BRANCH):
            _cast_weight(wb_hbm.at[layer, i], wb_ref.at[i], wb_stage, w_sem.at[1])
        _cast_weight(wo_hbm.at[layer], wo_ref, wo_stage, w_sem.at[2])

    x = x_ref[...]
    xn = _rms(x, g_ref[...]).astype(BF16)
    yc = jnp.concatenate(
        [_rms(hc_ref[:, h * LANES:(h + 1) * LANES], hn_ref[:, h * LANES:(h + 1) * LANES]) for h in range(LSTM_HEADS)],
        axis=1)
    yc = (yc * jax.nn.sigmoid(lo_ref[...])).astype(BF16)
    mix = None
    for i, y in enumerate((ya_ref[...], yb_ref[...], yc)):
        c0 = i * D_MODEL
        gate = jax.nn.sigmoid(_dot(xn, wg_ref[:, c0:c0 + D_MODEL]) + bg_ref[:, c0:c0 + D_MODEL])
        term = gate * _dot(y, wb_ref[i])
        mix = term if mix is None else mix + term
    o_ref[...] = x + _dot(mix.astype(BF16), wo_ref[...])


def _combine(x2, ya, yb, hc, lo, g, hn, wg, bg, wb, wo, layer):
    n = x2.shape[0]
    t = TOK_TILE
    const = lambda *shape: pl.BlockSpec(shape, lambda i: (0,) * len(shape))
    tok = lambda w: pl.BlockSpec((t, w), lambda i: (i, 0))
    hbm = lambda: pl.BlockSpec(memory_space=pl.ANY)
    stage = lambda k, n_: pltpu.VMEM((2, _stage_rows(k, n_), n_), F32)
    return pl.pallas_call(
        functools.partial(_combine_kernel, layer=layer),
        grid=(n // t,),
        in_specs=[tok(D_MODEL), tok(BRANCH_W), tok(BRANCH_W), tok(BRANCH_W), tok(BRANCH_W), const(1, D_MODEL),
                  const(1, BRANCH_W), hbm(), const(1, N_BRANCH * D_MODEL), hbm(), hbm()],
        out_specs=tok(D_MODEL),
        out_shape=jax.ShapeDtypeStruct((n, D_MODEL), F32),
        scratch_shapes=[pltpu.VMEM((D_MODEL, N_BRANCH * D_MODEL), BF16),
                        pltpu.VMEM((N_BRANCH, BRANCH_W, D_MODEL), BF16),
                        pltpu.VMEM((D_MODEL, D_MODEL), BF16),
                        stage(D_MODEL, N_BRANCH * D_MODEL), stage(BRANCH_W, D_MODEL), stage(D_MODEL, D_MODEL),
                        pltpu.SemaphoreType.DMA((3, 2))],
        compiler_params=pltpu.CompilerParams(dimension_semantics=("arbitrary",),
                                             vmem_limit_bytes=VMEM_LIMIT),
        name="combine",
    )(x2, ya, yb, hc, lo, g, hn, wg, bg, wb, wo)


FFN_COL_CHUNK = 256


def _ffn_kernel(x_ref, g_ref, wi_hbm, wo_hbm, o_ref, wi_ref, wo_ref, wi_stage, wo_stage, w_sem, *, layer):
    @pl.when(pl.program_id(0) == 0)
    def _():
        _cast_weight(wi_hbm.at[layer], wi_ref, wi_stage, w_sem.at[0])
        _cast_weight(wo_hbm.at[layer], wo_ref, wo_stage, w_sem.at[1])

    x = x_ref[...]
    xn = _rms(x, g_ref[...]).astype(BF16)
    acc = x
    for c0 in range(0, FFN_HIDDEN, FFN_COL_CHUNK):
        c1 = min(c0 + FFN_COL_CHUNK, FFN_HIDDEN)
        a = jax.nn.silu(_dot(xn, wi_ref[:, c0:c1])) * _dot(xn, wi_ref[:, FFN_HIDDEN + c0:FFN_HIDDEN + c1])
        acc = acc + _dot(a.astype(BF16), wo_ref[c0:c1, :])
    o_ref[...] = acc


def _ffn(x2, g, wi, wo, layer):
    n = x2.shape[0]
    t = TOK_TILE
    const = lambda *shape: pl.BlockSpec(shape, lambda i: (0,) * len(shape))
    tok = lambda w: pl.BlockSpec((t, w), lambda i: (i, 0))
    hbm = lambda: pl.BlockSpec(memory_space=pl.ANY)
    stage = lambda k, n_: pltpu.VMEM((2, _stage_rows(k, n_), n_), F32)
    return pl.pallas_call(
        functools.partial(_ffn_kernel, layer=layer),
        grid=(n // t,),
        in_specs=[tok(D_MODEL), const(1, D_MODEL), hbm(), hbm()],
        out_specs=tok(D_MODEL),
        out_shape=jax.ShapeDtypeStruct((n, D_MODEL), F32),
        scratch_shapes=[pltpu.VMEM((D_MODEL, 2 * FFN_HIDDEN), BF16), pltpu.VMEM((FFN_HIDDEN, D_MODEL), BF16),
                        stage(D_MODEL, 2 * FFN_HIDDEN), stage(FFN_HIDDEN, D_MODEL),
                        pltpu.SemaphoreType.DMA((2, 2))],
        compiler_params=pltpu.CompilerParams(dimension_semantics=("arbitrary",),
                                             vmem_limit_bytes=VMEM_LIMIT),
        name="ffn",
    )(x2, g, wi, wo)


def _rope_tables(seq):
    rows = seq // GRID_W
    row = jnp.repeat(jnp.arange(rows, dtype=F32), GRID_W)
    col = jnp.tile(jnp.arange(GRID_W, dtype=F32), rows)
    axis_dim = HEAD_DIM // 2
    freqs = ROPE_THETA ** (-jnp.arange(axis_dim // 2, dtype=F32) * 2.0 / axis_dim)
    ang = jnp.concatenate([row[:, None] * freqs[None], col[:, None] * freqs[None]], axis=-1)
    cos, sin = jnp.cos(ang), jnp.sin(ang)
    cos_full = jnp.repeat(cos, 2, axis=-1)
    sin_signed = jnp.stack([-sin, sin], axis=-1).reshape(seq, HEAD_DIM)
    return cos_full, sin_signed


def _sel_matrix():
    sel = np.zeros((P_ROWS, N_COLQ * LANES), np.float32)
    for q in range(N_COLQ):
        sel[q * N_SPLIT:(q + 1) * N_SPLIT, q * LANES:(q + 1) * LANES] = 1.0
    return jnp.asarray(sel, BF16)


def kernel(x, norm_mix, w_in, sgu_ln_g, sgu_ln_b, sgu_w, sgu_b, q_norm, k_norm, conv_w, conv_b, igate_b, fgate_b,
           lstm_norm, w_gate, b_gate, w_branch, w_out, norm_ffn, w_ffn_in, w_ffn_out):
    batch, seq, d = x.shape
    depth = norm_mix.shape[0]
    nh = LSTM_HEADS
    L = LSTM_CHUNK
    nc = seq // L
    cos, sin = _rope_tables(seq)
    sel = _sel_matrix()
    x2 = x.reshape(batch * seq, d)
    w_in_b = w_in.astype(BF16)
    for l in range(depth):
        w_g = w_in_b[l, :, N_MAIN:].T.reshape(4, nh, d)
        w_g = jnp.pad(w_g, ((0, 0), (0, SUBLANES - nh), (0, 0))).reshape(4 * SUBLANES, d)
        sgub = jnp.repeat(sgu_b[l].T, LANES, axis=1)
        ya, qa, ka, va, lq, lk, lv, lo, graw = _inproj(
            x2, norm_mix[l][None], w_in_b, l, w_g, sgu_ln_g[l][None], sgu_ln_b[l][None],
            sgu_w[l].astype(BF16), sgub, q_norm[l][None], k_norm[l][None], cos, sin, seq)
        yb = _attention(qa, ka, va, batch, seq)

        gbias = jnp.pad(jnp.stack([igate_b[l], fgate_b[l]], axis=0), ((0, 0), (0, 0), (0, SUBLANES - nh)))
        gbias = jnp.broadcast_to(gbias[:, :, None, :, None], (2, 2, batch, SUBLANES, L))
        tab = _gate_prep(graw.reshape(2, 2, nc * batch * SUBLANES, L), gbias.reshape(2, 2, batch * SUBLANES, L))
        cw = conv_w[l].reshape(CONV_K, 2 * nh, LANES).transpose(1, 0, 2)
        cb = conv_b[l].reshape(2 * nh, 1, LANES)
        hc = _mlstm(lq, lk, lv, tab, sel, cw, cb, batch, seq)

        x2 = _combine(x2, ya, yb, hc, lo, norm_mix[l][None], lstm_norm[l][None], w_gate,
                      b_gate[l][None], w_branch, w_out, l)
        x2 = _ffn(x2, norm_ffn[l][None], w_ffn_in, w_ffn_out, l)
    return x2.reshape(batch, seq, d)
```

```python
import functools

import jax
import jax.numpy as jnp
import numpy as np
from jax import lax
from jax.experimental import pallas as pl
from jax.experimental.pallas import tpu as pltpu

D_MODEL = 1024
GRID_W = 64
BRANCH_W = 512
N_BRANCH = 3
EPS = 1e-6
SGU_CHUNK = 128
SGU_GROUPS = 4
ATT_HEADS = 4
ATT_KV_HEADS = 2
HEAD_DIM = 128
ROPE_THETA = 10000.0
LSTM_HEADS = 4
LSTM_CHUNK = 128
CONV_K = 5
FFN_HIDDEN = 2816
N_MAIN = 4096

LANES = 128
SUBLANES = 8
VMEM_LIMIT = 56 * 1024 * 1024

TOK_TILE = 512
ATT_Q_TILE = 512
ATT_KEY_BLOCK = 512
CONV_ROWS = 256
CONV_PAD = SUBLANES

LOG2_E = 1.4426950408889634

BF16 = jnp.bfloat16
F32 = jnp.float32


def _rms(x, g):
    return x * lax.rsqrt(jnp.mean(x * x, axis=-1, keepdims=True) + EPS) * g


def _dot(a, b):
    return jnp.dot(a, b, preferred_element_type=F32)


def _dot_nt(a, b):
    return lax.dot_general(a, b, (((1,), (1,)), ((), ())), preferred_element_type=F32)


def _dot_tn(a, b):
    return lax.dot_general(a, b, (((0,), (0,)), ((), ())), preferred_element_type=F32)


W_STAGE_BYTES = 2 * 1024 * 1024


def _stage_rows(k, n):
    rows = max(16, min(k, W_STAGE_BYTES // (4 * n)) // 16 * 16)
    while k % rows:
        rows -= 16
    return rows


def _cast_weight(w_hbm, w_vmem, stage, sem):
    k = w_vmem.shape[0]
    rows = stage.shape[1]

    def copy(c, slot):
        return pltpu.make_async_copy(w_hbm.at[pl.ds(c * rows, rows)], stage.at[slot], sem.at[slot])

    copy(0, 0).start()
    for c in range(k // rows):
        slot = c % 2
        if c + 1 < k // rows:
            copy(c + 1, 1 - slot).start()
        copy(c, slot).wait()
        w_vmem[c * rows:(c + 1) * rows, :] = stage[slot].astype(BF16)


def _inproj_kernel(x_ref, g_ref, w_ref, wgr_ref, lng_ref, lnb_ref, sguw_ref, sgub_ref,
                   qn_ref, kn_ref, cos_ref, sin_ref,
                   ya_ref, qa_ref, ka_ref, va_ref, lq_ref, lk_ref, lv_ref, lo_ref, grow_ref):
    t = x_ref.shape[0]
    xn = _rms(x_ref[...], g_ref[...]).astype(BF16)

    def seg(lo, hi):
        return _dot(xn, w_ref[:, lo:hi])

    gv = jax.nn.gelu(seg(512, 1024))
    mu = jnp.mean(gv, axis=-1, keepdims=True)
    vc = gv - mu
    v = vc * lax.rsqrt(jnp.mean(vc * vc, axis=-1, keepdims=True) + EPS) * lng_ref[...] + lnb_ref[...]
    vb = v.astype(BF16)
    u = jax.nn.gelu(seg(0, 512))

    cos = cos_ref[...]
    sin = sin_ref[...]
    even = (lax.broadcasted_iota(jnp.int32, (t, LANES), 1) % 2) == 0

    def norm_rope(xh, gain):
        xh = _rms(xh, gain)
        partner = jnp.where(even, pltpu.roll(xh, LANES - 1, 1), pltpu.roll(xh, 1, 1))
        return (xh * cos + partner * sin).astype(BF16)

    aq = seg(1024, 1536)
    for h in range(ATT_HEADS):
        qa_ref[:, h * LANES:(h + 1) * LANES] = norm_rope(aq[:, h * LANES:(h + 1) * LANES], qn_ref[...])
    ak = seg(1536, 1792)
    for h in range(ATT_KV_HEADS):
        ka_ref[:, h * LANES:(h + 1) * LANES] = norm_rope(ak[:, h * LANES:(h + 1) * LANES], kn_ref[...])
    va_ref[...] = seg(1792, 2048).astype(BF16)

    lq_ref[...] = seg(2048, 2560)
    lk_ref[...] = seg(2560, 3072)
    lv_ref[...] = seg(3072, 3584).astype(BF16)
    lo_ref[...] = seg(3584, 4096)
    gt = _dot_nt(wgr_ref[...], xn)
    for kd in range(4):
        for j in range(t // LSTM_CHUNK):
            grow_ref[kd // 2, kd % 2, j, 0] = gt[kd * SUBLANES:(kd + 1) * SUBLANES, j * LANES:(j + 1) * LANES]

    n_chunks = t // SGU_CHUNK
    for grp in range(SGU_GROUPS):
        c0 = grp * LANES
        vcat = jnp.concatenate([vb[j * SGU_CHUNK:(j + 1) * SGU_CHUNK, c0:c0 + LANES] for j in range(n_chunks)], axis=1)
        s = _dot(sguw_ref[grp], vcat)
        for j in range(n_chunks):
            r0 = j * SGU_CHUNK
            sj = s[:, j * LANES:(j + 1) * LANES] + sgub_ref[:, c0:c0 + LANES]
            ya_ref[r0:r0 + SGU_CHUNK, c0:c0 + LANES] = (u[r0:r0 + SGU_CHUNK, c0:c0 + LANES] * sj).astype(BF16)


def _inproj(x2, g, w_in_b, layer, w_gr, lng, lnb, sguw, sgub, qn, kn, cos, sin, seq):
    n = x2.shape[0]
    t = TOK_TILE
    tiles_per_seq = seq // t
    const = lambda *shape: pl.BlockSpec(shape, lambda i: (0,) * len(shape))
    tok = lambda w: pl.BlockSpec((t, w), lambda i: (i, 0))
    rope = pl.BlockSpec((t, LANES), lambda i: (i % tiles_per_seq, 0))
    out_shapes = (
        jax.ShapeDtypeStruct((n, BRANCH_W), BF16),
        jax.ShapeDtypeStruct((n, 512), BF16),
        jax.ShapeDtypeStruct((n, 256), BF16),
        jax.ShapeDtypeStruct((n, 256), BF16),
        jax.ShapeDtypeStruct((n, BRANCH_W), F32),
        jax.ShapeDtypeStruct((n, BRANCH_W), F32),
        jax.ShapeDtypeStruct((n, BRANCH_W), BF16),
        jax.ShapeDtypeStruct((n, BRANCH_W), F32),
        jax.ShapeDtypeStruct((2, 2, seq // LSTM_CHUNK, n // seq, SUBLANES, LSTM_CHUNK), F32),
    )
    out_specs = (tok(512), tok(512), tok(256), tok(256), tok(512), tok(512), tok(512), tok(512),
                 pl.BlockSpec((2, 2, t // LSTM_CHUNK, 1, SUBLANES, LSTM_CHUNK),
                              lambda i: (0, 0, i % tiles_per_seq, i // tiles_per_seq, 0, 0)))
    return pl.pallas_call(
        _inproj_kernel,
        grid=(n // t,),
        in_specs=[tok(D_MODEL), const(1, D_MODEL),
                  pl.BlockSpec((None, D_MODEL, N_MAIN), lambda i: (layer, 0, 0)),
                  const(4 * SUBLANES, D_MODEL), const(1, BRANCH_W), const(1, BRANCH_W),
                  const(SGU_GROUPS, SGU_CHUNK, SGU_CHUNK), const(SGU_CHUNK, BRANCH_W),
                  const(1, LANES), const(1, LANES), rope, rope],
        out_specs=out_specs,
        out_shape=out_shapes,
        compiler_params=pltpu.CompilerParams(dimension_semantics=("arbitrary",),
                                             vmem_limit_bytes=VMEM_LIMIT),
        name="inproj",
    )(x2, g, w_in_b, w_gr, lng, lnb, sguw, sgub, qn, kn, cos, sin)


def _attn_kernel(q_ref, k_ref, v_ref, o_ref):
    exp2_scale = (HEAD_DIM ** -0.5) * LOG2_E
    rep = ATT_HEADS // ATT_KV_HEADS
    tq = q_ref.shape[0]
    kb = ATT_KEY_BLOCK
    ones = jnp.ones((kb, LANES), BF16)
    qs, ms, accs = [], [], []
    for grp in range(ATT_KV_HEADS):
        c0 = grp * rep * LANES
        qs.append(jnp.concatenate([q_ref[:, c0 + r * LANES:c0 + (r + 1) * LANES] for r in range(rep)], axis=0))
    for j in range(k_ref.shape[0] // kb):
        for grp in range(ATT_KV_HEADS):
            kblk = k_ref[j * kb:(j + 1) * kb, grp * LANES:(grp + 1) * LANES]
            vaug = jnp.concatenate([v_ref[j * kb:(j + 1) * kb, grp * LANES:(grp + 1) * LANES], ones], axis=1)
            s = _dot_nt(qs[grp], kblk)
            bmax = jnp.max(s, axis=-1, keepdims=True)
            if j == 0:
                ms.append(bmax)
                accs.append(_dot(jnp.exp2((s - bmax) * exp2_scale).astype(BF16), vaug))
            else:
                m_new = jnp.maximum(ms[grp], bmax)
                alpha = jnp.exp2((ms[grp] - m_new) * exp2_scale)
                p = jnp.exp2((s - m_new) * exp2_scale).astype(BF16)
                accs[grp] = alpha * accs[grp] + _dot(p, vaug)
                ms[grp] = m_new
    for grp in range(ATT_KV_HEADS):
        c0 = grp * rep * LANES
        o = accs[grp][:, :LANES] / accs[grp][:, LANES:]
        for r in range(rep):
            o_ref[:, c0 + r * LANES:c0 + (r + 1) * LANES] = o[r * tq:(r + 1) * tq].astype(BF16)


def _attention(qa, ka, va, batch, seq):
    n = qa.shape[0]
    tq = ATT_Q_TILE
    qpb = seq // tq
    return pl.pallas_call(
        _attn_kernel,
        grid=(batch, qpb),
        in_specs=[pl.BlockSpec((tq, 512), lambda b, i: (b * qpb + i, 0)),
                  pl.BlockSpec((seq, 256), lambda b, i: (b, 0)),
                  pl.BlockSpec((seq, 256), lambda b, i: (b, 0))],
        out_specs=pl.BlockSpec((tq, 512), lambda b, i: (b * qpb + i, 0)),
        out_shape=jax.ShapeDtypeStruct((n, BRANCH_W), BF16),
        compiler_params=pltpu.CompilerParams(dimension_semantics=("arbitrary", "arbitrary"),
                                             vmem_limit_bytes=VMEM_LIMIT),
        name="gqa",
    )(qa, ka, va)


def _log_sigmoid(x):
    return jnp.minimum(x, 0.0) - jnp.log1p(jnp.exp(-jnp.abs(x)))


def _lane_scan(x, op, identity, reverse):
    lane = lax.broadcasted_iota(jnp.int32, x.shape, 1)
    k = 1
    while k < LANES:
        if reverse:
            x = op(x, jnp.where(lane < LANES - k, pltpu.roll(x, LANES - k, 1), identity))
        else:
            x = op(x, jnp.where(lane >= k, pltpu.roll(x, k, 1), identity))
        k *= 2
    return x


N_SPLIT = 3
N_COLQ = 2
N_PIECES = N_COLQ * N_SPLIT
N_TAB = N_PIECES + 3
P_ROWS = 16


def _split3(x):
    hi = x.astype(BF16)
    r = x - hi.astype(F32)
    mid = r.astype(BF16)
    lo = (r - mid.astype(F32)).astype(BF16)
    return [hi, mid, lo]


def _gate_kernel(raw_ref, bias_ref, tab_ref, rm_s, bt_s, gm_s, m0_s, m1_s):
    nr = bias_ref.shape[2]
    nc = raw_ref.shape[2] // nr
    L = LSTM_CHUNK
    for d in range(2):
        rev = d == 1
        last = 0 if rev else L - 1
        for k in range(2):
            for r in range(nr):
                rm_s[k, r * nc:(r + 1) * nc, :] = raw_ref[k, d, pl.ds(r, nc, stride=nr), :] + bias_ref[k, d, r:r + 1, :]
        ig = rm_s[0]
        lf = _log_sigmoid(rm_s[1])
        b = _lane_scan(lf, jnp.add, 0.0, rev)
        a = ig - b
        g = _lane_scan(a, jnp.maximum, -jnp.inf, rev)
        bt_s[...] = jnp.broadcast_to(b[:, last:last + 1], b.shape)
        gm_s[...] = jnp.broadcast_to(g[:, last:last + 1], g.shape)
        m = jnp.zeros((nr, L), F32)
        for c in (range(nc - 1, -1, -1) if rev else range(nc)):
            rows = pl.ds(c, nr, stride=nc)
            m0_s[rows, :] = m
            m = bt_s[rows, :] + jnp.maximum(m, gm_s[rows, :])
            m1_s[rows, :] = m
        m0 = m0_s[...]
        m1 = m1_s[...]
        btot = bt_s[...]
        mx = jnp.maximum(m0, g)
        pieces = _split3(m0 - mx) + _split3(-(b + mx))
        for k, part in enumerate(pieces):
            tab_ref[d, k] = part.astype(F32)
        tab_ref[d, N_PIECES] = a - m0
        tab_ref[d, N_PIECES + 1] = jnp.exp(btot - b + ig - m1)
        tab_ref[d, N_PIECES + 2] = jnp.exp(btot + m0 - m1)


def _gate_prep(raw, bias):
    rows, L = raw.shape[2], raw.shape[3]
    buf = lambda *lead: pltpu.VMEM((*lead, rows, L), F32)
    return pl.pallas_call(
        _gate_kernel,
        out_shape=jax.ShapeDtypeStruct((2, N_TAB, rows, L), F32),
        scratch_shapes=[buf(2), buf(), buf(), buf(), buf()],
        compiler_params=pltpu.CompilerParams(vmem_limit_bytes=VMEM_LIMIT),
        name="mlstm_gates",
    )(raw, bias)


def _mlstm_kernel(lq_ref, lk_ref, lv_ref, tab_ref, sel_ref,
                  cw_q_ref, cw_k_ref, cb_q_ref, cb_k_ref, o_ref, xq_s, xk_s, q_s, kt_s, ktb_s, c_s):
    seq = lq_ref.shape[0]
    L = LSTM_CHUNK
    nc = seq // L

    zpad = jnp.zeros((CONV_PAD, LANES), F32)
    for x_s, src_ref in ((xq_s, lq_ref), (xk_s, lk_ref)):
        x_s[0:CONV_PAD, :] = zpad
        x_s[CONV_PAD + seq:CONV_PAD + seq + CONV_PAD, :] = zpad
        x_s[CONV_PAD:CONV_PAD + seq, :] = src_ref[...]

    def conv_silu(x_s, w_ref, b_ref, r0):
        acc = jnp.zeros((CONV_ROWS, LANES), F32) + b_ref[0]
        for j in range(CONV_K):
            start = CONV_PAD + r0 + j - CONV_K // 2
            acc = acc + x_s[start:start + CONV_ROWS, :] * w_ref[0, j:j + 1, :]
        return jax.nn.silu(acc)

    n_blocks = seq // CONV_ROWS
    for i in range(n_blocks // 2):
        for blk in (i, n_blocks - 1 - i):
            r0 = blk * CONV_ROWS
            q_s[r0:r0 + CONV_ROWS, :] = conv_silu(xq_s, cw_q_ref, cb_q_ref, r0)
            k = conv_silu(xk_s, cw_k_ref, cb_k_ref, r0) * (HEAD_DIM ** -0.5)
            for j in range(CONV_ROWS // L):
                kt = k[j * L:(j + 1) * L, :].T
                kt_s[r0 // L + j] = kt
                ktb_s[r0 // L + j] = kt.astype(BF16)

    row_i = lax.broadcasted_iota(jnp.int32, (L, L), 0)
    col_i = lax.broadcasted_iota(jnp.int32, (L, L), 1)
    masks = (col_i <= row_i, col_i >= row_i)
    ones = jnp.ones((L, LANES), BF16)
    pad_rows = jnp.zeros((P_ROWS - N_PIECES, L), F32)
    c_s[...] = jnp.zeros(c_s.shape, F32)

    def chunk_step(d, c, first):
        rows = pl.ds(c * L, L)
        row = pl.ds(c, 1)
        vaug = jnp.concatenate([lv_ref[rows, :], ones], axis=1)
        caug = c_s[d]
        qc = q_s[rows, :]
        qk = _dot(qc.astype(BF16), ktb_s[c])
        tab = [tab_ref[d, k, row, :] for k in range(N_TAB)]
        pieces = jnp.concatenate(tab[:N_PIECES] + [pad_rows], axis=0).astype(BF16)
        e = _dot_tn(pieces, sel_ref[...])
        dmx = e[:, :LANES]
        w = jnp.where(masks[d], jnp.exp(tab[N_PIECES] + dmx), 0.0)
        s = (qk * w).astype(BF16)
        qw = (qc * jnp.exp(dmx)).astype(BF16)
        kw = (kt_s[c] * tab[N_PIECES + 1]).astype(BF16)
        upd = _dot(kw, vaug)
        r = _dot(jnp.concatenate([s, qw], axis=1), jnp.concatenate([vaug, caug.astype(BF16)], axis=0))
        h = r[:, :LANES] / jnp.maximum(jnp.abs(r[:, LANES:]), jnp.exp(e[:, LANES:]))
        o_ref[rows, :] = h if first else o_ref[rows, :] + h
        dec = tab[N_PIECES + 2]
        c_s[d] = jnp.concatenate([dec, dec], axis=1) * caug + upd

    for i in range(nc):
        chunk_step(0, i, first=i < nc - 1 - i)
        chunk_step(1, nc - 1 - i, first=nc - 1 - i > i)


def _mlstm(lq, lk, lv, tab, sel, cw, cb, batch, seq):
    n = lq.shape[0]
    nh = LSTM_HEADS
    nc = seq // LSTM_CHUNK
    head_blk = lambda: pl.BlockSpec((seq, LANES), lambda b, h: (b, h))
    return pl.pallas_call(
        _mlstm_kernel,
        grid=(batch, nh),
        in_specs=[head_blk(), head_blk(), head_blk(),
                  pl.BlockSpec((2, N_TAB, nc, LSTM_CHUNK), lambda b, h: (0, 0, b * SUBLANES + h, 0)),
                  pl.BlockSpec((P_ROWS, N_COLQ * LANES), lambda b, h: (0, 0)),
                  pl.BlockSpec((1, CONV_K, LANES), lambda b, h: (h, 0, 0)),
                  pl.BlockSpec((1, CONV_K, LANES), lambda b, h: (nh + h, 0, 0)),
                  pl.BlockSpec((1, 1, LANES), lambda b, h: (h, 0, 0)),
                  pl.BlockSpec((1, 1, LANES), lambda b, h: (nh + h, 0, 0))],
        out_specs=head_blk(),
        out_shape=jax.ShapeDtypeStruct((n, BRANCH_W), F32),
        scratch_shapes=[pltpu.VMEM((seq + 2 * CONV_PAD, LANES), F32),
                        pltpu.VMEM((seq + 2 * CONV_PAD, LANES), F32),
                        pltpu.VMEM((seq, LANES), F32),
                        pltpu.VMEM((nc, HEAD_DIM, LSTM_CHUNK), F32),
                        pltpu.VMEM((nc, HEAD_DIM, LSTM_CHUNK), BF16),
                        pltpu.VMEM((2, HEAD_DIM, 2 * LANES), F32)],
        compiler_params=pltpu.CompilerParams(dimension_semantics=("arbitrary", "arbitrary"),
                                             vmem_limit_bytes=VMEM_LIMIT),
        name="mlstm",
    )(lq, lk, lv, tab, sel, cw, cw, cb, cb)


def _combine_kernel(x_ref, ya_ref, yb_ref, hc_ref, lo_ref, g_ref, hn_ref, wg_hbm, bg_ref, wb_hbm, wo_hbm, o_ref,
                    wg_ref, wb_ref, wo_ref, wg_stage, wb_stage, wo_stage, w_sem, *, layer):
    @pl.when(pl.program_id(0) == 0)
    def _():
        _cast_weight(wg_hbm.at[layer], wg_ref, wg_stage, w_sem.at[0])
        for i in range(N_BRANCH):
            _cast_weight(wb_hbm.at[layer, i], wb_ref.at[i], wb_stage, w_sem.at[1])
        _cast_weight(wo_hbm.at[layer], wo_ref, wo_stage, w_sem.at[2])

    x = x_ref[...]
    xn = _rms(x, g_ref[...]).astype(BF16)
    yc = jnp.concatenate(
        [_rms(hc_ref[:, h * LANES:(h + 1) * LANES], hn_ref[:, h * LANES:(h + 1) * LANES]) for h in range(LSTM_HEADS)],
        axis=1)
    yc = (yc * jax.nn.sigmoid(lo_ref[...])).astype(BF16)
    mix = None
    for i, y in enumerate((ya_ref[...], yb_ref[...], yc)):
        c0 = i * D_MODEL
        gate = jax.nn.sigmoid(_dot(xn, wg_ref[:, c0:c0 + D_MODEL]) + bg_ref[:, c0:c0 + D_MODEL])
        term = gate * _dot(y, wb_ref[i])
        mix = term if mix is None else mix + term
    o_ref[...] = x + _dot(mix.astype(BF16), wo_ref[...])


def _combine(x2, ya, yb, hc, lo, g, hn, wg, bg, wb, wo, layer):
    n = x2.shape[0]
    t = TOK_TILE
    const = lambda *shape: pl.BlockSpec(shape, lambda i: (0,) * len(shape))
    tok = lambda w: pl.BlockSpec((t, w), lambda i: (i, 0))
    hbm = lambda: pl.BlockSpec(memory_space=pl.ANY)
    stage = lambda k, n_: pltpu.VMEM((2, _stage_rows(k, n_), n_), F32)
    return pl.pallas_call(
        functools.partial(_combine_kernel, layer=layer),
        grid=(n // t,),
        in_specs=[tok(D_MODEL), tok(BRANCH_W), tok(BRANCH_W), tok(BRANCH_W), tok(BRANCH_W), const(1, D_MODEL),
                  const(1, BRANCH_W), hbm(), const(1, N_BRANCH * D_MODEL), hbm(), hbm()],
        out_specs=tok(D_MODEL),
        out_shape=jax.ShapeDtypeStruct((n, D_MODEL), F32),
        scratch_shapes=[pltpu.VMEM((D_MODEL, N_BRANCH * D_MODEL), BF16),
                        pltpu.VMEM((N_BRANCH, BRANCH_W, D_MODEL), BF16),
                        pltpu.VMEM((D_MODEL, D_MODEL), BF16),
                        stage(D_MODEL, N_BRANCH * D_MODEL), stage(BRANCH_W, D_MODEL), stage(D_MODEL, D_MODEL),
                        pltpu.SemaphoreType.DMA((3, 2))],
        compiler_params=pltpu.CompilerParams(dimension_semantics=("arbitrary",),
                                             vmem_limit_bytes=VMEM_LIMIT),
        name="combine",
    )(x2, ya, yb, hc, lo, g, hn, wg, bg, wb, wo)


FFN_COL_CHUNK = 256


def _ffn_kernel(x_ref, g_ref, wi_hbm, wo_hbm, o_ref, wi_ref, wo_ref, wi_stage, wo_stage, w_sem, *, layer):
    @pl.when(pl.program_id(0) == 0)
    def _():
        _cast_weight(wi_hbm.at[layer], wi_ref, wi_stage, w_sem.at[0])
        _cast_weight(wo_hbm.at[layer], wo_ref, wo_stage, w_sem.at[1])

    x = x_ref[...]
    xn = _rms(x, g_ref[...]).astype(BF16)
    acc = x
    for c0 in range(0, FFN_HIDDEN, FFN_COL_CHUNK):
        c1 = min(c0 + FFN_COL_CHUNK, FFN_HIDDEN)
        a = jax.nn.silu(_dot(xn, wi_ref[:, c0:c1])) * _dot(xn, wi_ref[:, FFN_HIDDEN + c0:FFN_HIDDEN + c1])
        acc = acc + _dot(a.astype(BF16), wo_ref[c0:c1, :])
    o_ref[...] = acc


def _ffn(x2, g, wi, wo, layer):
    n = x2.shape[0]
    t = TOK_TILE
    const = lambda *shape: pl.BlockSpec(shape, lambda i: (0,) * len(shape))
    tok = lambda w: pl.BlockSpec((t, w), lambda i: (i, 0))
    hbm = lambda: pl.BlockSpec(memory_space=pl.ANY)
    stage = lambda k, n_: pltpu.VMEM((2, _stage_rows(k, n_), n_), F32)
    return pl.pallas_call(
        functools.partial(_ffn_kernel, layer=layer),
        grid=(n // t,),
        in_specs=[tok(D_MODEL), const(1, D_MODEL), hbm(), hbm()],
        out_specs=tok(D_MODEL),
        out_shape=jax.ShapeDtypeStruct((n, D_MODEL), F32),
        scratch_shapes=[pltpu.VMEM((D_MODEL, 2 * FFN_HIDDEN), BF16), pltpu.VMEM((FFN_HIDDEN, D_MODEL), BF16),
                        stage(D_MODEL, 2 * FFN_HIDDEN), stage(FFN_HIDDEN, D_MODEL),
                        pltpu.SemaphoreType.DMA((2, 2))],
        compiler_params=pltpu.CompilerParams(dimension_semantics=("arbitrary",),
                                             vmem_limit_bytes=VMEM_LIMIT),
        name="ffn",
    )(x2, g, wi, wo)


def _rope_tables(seq):
    rows = seq // GRID_W
    row = jnp.repeat(jnp.arange(rows, dtype=F32), GRID_W)
    col = jnp.tile(jnp.arange(GRID_W, dtype=F32), rows)
    axis_dim = HEAD_DIM // 2
    freqs = ROPE_THETA ** (-jnp.arange(axis_dim // 2, dtype=F32) * 2.0 / axis_dim)
    ang = jnp.concatenate([row[:, None] * freqs[None], col[:, None] * freqs[None]], axis=-1)
    cos, sin = jnp.cos(ang), jnp.sin(ang)
    cos_full = jnp.repeat(cos, 2, axis=-1)
    sin_signed = jnp.stack([-sin, sin], axis=-1).reshape(seq, HEAD_DIM)
    return cos_full, sin_signed


def _sel_matrix():
    sel = np.zeros((P_ROWS, N_COLQ * LANES), np.float32)
    for q in range(N_COLQ):
        sel[q * N_SPLIT:(q + 1) * N_SPLIT, q * LANES:(q + 1) * LANES] = 1.0
    return jnp.asarray(sel, BF16)


def kernel(x, norm_mix, w_in, sgu_ln_g, sgu_ln_b, sgu_w, sgu_b, q_norm, k_norm, conv_w, conv_b, igate_b, fgate_b,
           lstm_norm, w_gate, b_gate, w_branch, w_out, norm_ffn, w_ffn_in, w_ffn_out):
    batch, seq, d = x.shape
    depth = norm_mix.shape[0]
    nh = LSTM_HEADS
    L = LSTM_CHUNK
    nc = seq // L
    cos, sin = _rope_tables(seq)
    sel = _sel_matrix()
    x2 = x.reshape(batch * seq, d)
    w_in_b = w_in.astype(BF16)
    for l in range(depth):
        w_g = w_in_b[l, :, N_MAIN:].T.reshape(4, nh, d)
        w_g = jnp.pad(w_g, ((0, 0), (0, SUBLANES - nh), (0, 0))).reshape(4 * SUBLANES, d)
        sgub = jnp.repeat(sgu_b[l].T, LANES, axis=1)
        ya, qa, ka, va, lq, lk, lv, lo, graw = _inproj(
            x2, norm_mix[l][None], w_in_b, l, w_g, sgu_ln_g[l][None], sgu_ln_b[l][None],
            sgu_w[l].astype(BF16), sgub, q_norm[l][None], k_norm[l][None], cos, sin, seq)
        yb = _attention(qa, ka, va, batch, seq)

        gbias = jnp.pad(jnp.stack([igate_b[l], fgate_b[l]], axis=0), ((0, 0), (0, 0), (0, SUBLANES - nh)))
        gbias = jnp.broadcast_to(gbias[:, :, None, :, None], (2, 2, batch, SUBLANES, L))
        tab = _gate_prep(graw.reshape(2, 2, nc * batch * SUBLANES, L), gbias.reshape(2, 2, batch * SUBLANES, L))
        cw = conv_w[l].reshape(CONV_K, 2 * nh, LANES).transpose(1, 0, 2)
        cb = conv_b[l].reshape(2 * nh, 1, LANES)
        hc = _mlstm(lq, lk, lv, tab, sel, cw, cb, batch, seq)

        x2 = _combine(x2, ya, yb, hc, lo, norm_mix[l][None], lstm_norm[l][None], w_gate,
                      b_gate[l][None], w_branch, w_out, l)
        x2 = _ffn(x2, norm_ffn[l][None], w_ffn_in, w_ffn_out, l)
    return x2.reshape(batch, seq, d)
```

```python
import functools

import jax
import jax.numpy as jnp
import numpy as np
from jax import lax
from jax.experimental import pallas as pl
from jax.experimental.pallas import tpu as pltpu

D_MODEL = 1024
GRID_W = 64
BRANCH_W = 512
N_BRANCH = 3
EPS = 1e-6
SGU_CHUNK = 128
SGU_GROUPS = 4
ATT_HEADS = 4
ATT_KV_HEADS = 2
HEAD_DIM = 128
ROPE_THETA = 10000.0
LSTM_HEADS = 4
LSTM_CHUNK = 128
CONV_K = 5
FFN_HIDDEN = 2816
N_MAIN = 4096

LANES = 128
SUBLANES = 8
VMEM_LIMIT = 56 * 1024 * 1024

TOK_TILE = 512
ATT_Q_TILE = 512
ATT_KEY_BLOCK = 512
CONV_ROWS = 256
CONV_PAD = SUBLANES

LOG2_E = 1.4426950408889634

BF16 = jnp.bfloat16
F32 = jnp.float32


def _rms(x, g):
    return x * lax.rsqrt(jnp.mean(x * x, axis=-1, keepdims=True) + EPS) * g


def _dot(a, b):
    return jnp.dot(a, b, preferred_element_type=F32)


def _dot_nt(a, b):
    return lax.dot_general(a, b, (((1,), (1,)), ((), ())), preferred_element_type=F32)


def _dot_tn(a, b):
    return lax.dot_general(a, b, (((0,), (0,)), ((), ())), preferred_element_type=F32)


W_STAGE_BYTES = 2 * 1024 * 1024


def _stage_rows(k, n):
    rows = max(16, min(k, W_STAGE_BYTES // (4 * n)) // 16 * 16)
    while k % rows:
        rows -= 16
    return rows


def _cast_weight(w_hbm, w_vmem, stage, sem):
    k = w_vmem.shape[0]
    rows = stage.shape[1]

    def copy(c, slot):
        return pltpu.make_async_copy(w_hbm.at[pl.ds(c * rows, rows)], stage.at[slot], sem.at[slot])

    copy(0, 0).start()
    for c in range(k // rows):
        slot = c % 2
        if c + 1 < k // rows:
            copy(c + 1, 1 - slot).start()
        copy(c, slot).wait()
        w_vmem[c * rows:(c + 1) * rows, :] = stage[slot].astype(BF16)


def _inproj_kernel(x_ref, g_ref, w_ref, wgr_ref, lng_ref, lnb_ref, sguw_ref, sgub_ref,
                   qn_ref, kn_ref, cos_ref, sin_ref,
                   ya_ref, qa_ref, ka_ref, va_ref, lq_ref, lk_ref, lv_ref, lo_ref, grow_ref):
    t = x_ref.shape[0]
    xn = _rms(x_ref[...], g_ref[...]).astype(BF16)

    def seg(lo, hi):
        return _dot(xn, w_ref[:, lo:hi])

    gv = jax.nn.gelu(seg(512, 1024))
    mu = jnp.mean(gv, axis=-1, keepdims=True)
    vc = gv - mu
    v = vc * lax.rsqrt(jnp.mean(vc * vc, axis=-1, keepdims=True) + EPS) * lng_ref[...] + lnb_ref[...]
    vb = v.astype(BF16)
    u = jax.nn.gelu(seg(0, 512))

    cos = cos_ref[...]
    sin = sin_ref[...]
    even = (lax.broadcasted_iota(jnp.int32, (t, LANES), 1) % 2) == 0

    def norm_rope(xh, gain):
        xh = _rms(xh, gain)
        partner = jnp.where(even, pltpu.roll(xh, LANES - 1, 1), pltpu.roll(xh, 1, 1))
        return (xh * cos + partner * sin).astype(BF16)

    aq = seg(1024, 1536)
    for h in range(ATT_HEADS):
        qa_ref[:, h * LANES:(h + 1) * LANES] = norm_rope(aq[:, h * LANES:(h + 1) * LANES], qn_ref[...])
    ak = seg(1536, 1792)
    for h in range(ATT_KV_HEADS):
        ka_ref[:, h * LANES:(h + 1) * LANES] = norm_rope(ak[:, h * LANES:(h + 1) * LANES], kn_ref[...])
    va_ref[...] = seg(1792, 2048).astype(BF16)

    lq_ref[...] = seg(2048, 2560)
    lk_ref[...] = seg(2560, 3072)
    lv_ref[...] = seg(3072, 3584).astype(BF16)
    lo_ref[...] = seg(3584, 4096)
    gt = _dot_nt(wgr_ref[...], xn)
    for kd in range(4):
        for j in range(t // LSTM_CHUNK):
            grow_ref[kd // 2, kd % 2, j, 0] = gt[kd * SUBLANES:(kd + 1) * SUBLANES, j * LANES:(j + 1) * LANES]

    n_chunks = t // SGU_CHUNK
    for grp in range(SGU_GROUPS):
        c0 = grp * LANES
        vcat = jnp.concatenate([vb[j * SGU_CHUNK:(j + 1) * SGU_CHUNK, c0:c0 + LANES] for j in range(n_chunks)], axis=1)
        s = _dot(sguw_ref[grp], vcat)
        for j in range(n_chunks):
            r0 = j * SGU_CHUNK
            sj = s[:, j * LANES:(j + 1) * LANES] + sgub_ref[:, c0:c0 + LANES]
            ya_ref[r0:r0 + SGU_CHUNK, c0:c0 + LANES] = (u[r0:r0 + SGU_CHUNK, c0:c0 + LANES] * sj).astype(BF16)


def _inproj(x2, g, w_in_b, layer, w_gr, lng, lnb, sguw, sgub, qn, kn, cos, sin, seq):
    n = x2.shape[0]
    t = TOK_TILE
    tiles_per_seq = seq // t
    const = lambda *shape: pl.BlockSpec(shape, lambda i: (0,) * len(shape))
    tok = lambda w: pl.BlockSpec((t, w), lambda i: (i, 0))
    rope = pl.BlockSpec((t, LANES), lambda i: (i % tiles_per_seq, 0))
    out_shapes = (
        jax.ShapeDtypeStruct((n, BRANCH_W), BF16),
        jax.ShapeDtypeStruct((n, 512), BF16),
        jax.ShapeDtypeStruct((n, 256), BF16),
        jax.ShapeDtypeStruct((n, 256), BF16),
        jax.ShapeDtypeStruct((n, BRANCH_W), F32),
        jax.ShapeDtypeStruct((n, BRANCH_W), F32),
        jax.ShapeDtypeStruct((n, BRANCH_W), BF16),
        jax.ShapeDtypeStruct((n, BRANCH_W), F32),
        jax.ShapeDtypeStruct((2, 2, seq // LSTM_CHUNK, n // seq, SUBLANES, LSTM_CHUNK), F32),
    )
    out_specs = (tok(512), tok(512), tok(256), tok(256), tok(512), tok(512), tok(512), tok(512),
                 pl.BlockSpec((2, 2, t // LSTM_CHUNK, 1, SUBLANES, LSTM_CHUNK),
                              lambda i: (0, 0, i % tiles_per_seq, i // tiles_per_seq, 0, 0)))
    return pl.pallas_call(
        _inproj_kernel,
        grid=(n // t,),
        in_specs=[tok(D_MODEL), const(1, D_MODEL),
                  pl.BlockSpec((None, D_MODEL, N_MAIN), lambda i: (layer, 0, 0)),
                  const(4 * SUBLANES, D_MODEL), const(1, BRANCH_W), const(1, BRANCH_W),
                  const(SGU_GROUPS, SGU_CHUNK, SGU_CHUNK), const(SGU_CHUNK, BRANCH_W),
                  const(1, LANES), const(1, LANES), rope, rope],
        out_specs=out_specs,
        out_shape=out_shapes,
        compiler_params=pltpu.CompilerParams(dimension_semantics=("arbitrary",),
                                             vmem_limit_bytes=VMEM_LIMIT),
        name="inproj",
    )(x2, g, w_in_b, w_gr, lng, lnb, sguw, sgub, qn, kn, cos, sin)


def _attn_kernel(q_ref, k_ref, v_ref, o_ref):
    exp2_scale = (HEAD_DIM ** -0.5) * LOG2_E
    rep = ATT_HEADS // ATT_KV_HEADS
    tq = q_ref.shape[0]
    kb = ATT_KEY_BLOCK
    ones = jnp.ones((kb, LANES), BF16)
    qs, ms, accs = [], [], []
    for grp in range(ATT_KV_HEADS):
        c0 = grp * rep * LANES
        qs.append(jnp.concatenate([q_ref[:, c0 + r * LANES:c0 + (r + 1) * LANES] for r in range(rep)], axis=0))
    for j in range(k_ref.shape[0] // kb):
        for grp in range(ATT_KV_HEADS):
            kblk = k_ref[j * kb:(j + 1) * kb, grp * LANES:(grp + 1) * LANES]
            vaug = jnp.concatenate([v_ref[j * kb:(j + 1) * kb, grp * LANES:(grp + 1) * LANES], ones], axis=1)
            s = _dot_nt(qs[grp], kblk)
            bmax = jnp.max(s, axis=-1, keepdims=True)
            if j == 0:
                ms.append(bmax)
                accs.append(_dot(jnp.exp2((s - bmax) * exp2_scale).astype(BF16), vaug))
            else:
                m_new = jnp.maximum(ms[grp], bmax)
                alpha = jnp.exp2((ms[grp] - m_new) * exp2_scale)
                p = jnp.exp2((s - m_new) * exp2_scale).astype(BF16)
                accs[grp] = alpha * accs[grp] + _dot(p, vaug)
                ms[grp] = m_new
    for grp in range(ATT_KV_HEADS):
        c0 = grp * rep * LANES
        o = accs[grp][:, :LANES] / accs[grp][:, LANES:]
        for r in range(rep):
            o_ref[:, c0 + r * LANES:c0 + (r + 1) * LANES] = o[r * tq:(r + 1) * tq].astype(BF16)


def _attention(qa, ka, va, batch, seq):
    n = qa.shape[0]
    tq = ATT_Q_TILE
    qpb = seq // tq
    return pl.pallas_call(
        _attn_kernel,
        grid=(batch, qpb),
        in_specs=[pl.BlockSpec((tq, 512), lambda b, i: (b * qpb + i, 0)),
                  pl.BlockSpec((seq, 256), lambda b, i: (b, 0)),
                  pl.BlockSpec((seq, 256), lambda b, i: (b, 0))],
        out_specs=pl.BlockSpec((tq, 512), lambda b, i: (b * qpb + i, 0)),
        out_shape=jax.ShapeDtypeStruct((n, BRANCH_W), BF16),
        compiler_params=pltpu.CompilerParams(dimension_semantics=("arbitrary", "arbitrary"),
                                             vmem_limit_bytes=VMEM_LIMIT),
        name="gqa",
    )(qa, ka, va)


def _log_sigmoid(x):
    return jnp.minimum(x, 0.0) - jnp.log1p(jnp.exp(-jnp.abs(x)))


def _lane_scan(x, op, identity, reverse):
    lane = lax.broadcasted_iota(jnp.int32, x.shape, 1)
    k = 1
    while k < LANES:
        if reverse:
            x = op(x, jnp.where(lane < LANES - k, pltpu.roll(x, LANES - k, 1), identity))
        else:
            x = op(x, jnp.where(lane >= k, pltpu.roll(x, k, 1), identity))
        k *= 2
    return x


N_SPLIT = 3
N_COLQ = 2
N_PIECES = N_COLQ * N_SPLIT
N_TAB = N_PIECES + 3
P_ROWS = 16


def _split3(x):
    hi = x.astype(BF16)
    r = x - hi.astype(F32)
    mid = r.astype(BF16)
    lo = (r - mid.astype(F32)).astype(BF16)
    return [hi, mid, lo]


def _gate_kernel(raw_ref, bias_ref, tab_ref, rm_s, bt_s, gm_s, m0_s, m1_s):
    nr = bias_ref.shape[2]
    slots = nr // LSTM_HEADS * SUBLANES
    nc = raw_ref.shape[2] // slots
    L = LSTM_CHUNK
    for d in range(2):
        rev = d == 1
        last = 0 if rev else L - 1
        for k in range(2):
            for r in range(nr):
                slot = r // LSTM_HEADS * SUBLANES + r % LSTM_HEADS
                rm_s[k, r * nc:(r + 1) * nc, :] = (raw_ref[k, d, pl.ds(slot, nc, stride=slots), :]
                                                   + bias_ref[k, d, r:r + 1, :])
        ig = rm_s[0]
        lf = _log_sigmoid(rm_s[1])
        b = _lane_scan(lf, jnp.add, 0.0, rev)
        a = ig - b
        g = _lane_scan(a, jnp.maximum, -jnp.inf, rev)
        bt_s[...] = jnp.broadcast_to(b[:, last:last + 1], b.shape)
        gm_s[...] = jnp.broadcast_to(g[:, last:last + 1], g.shape)
        m = jnp.zeros((nr, L), F32)
        for c in (range(nc - 1, -1, -1) if rev else range(nc)):
            rows = pl.ds(c, nr, stride=nc)
            m0_s[rows, :] = m
            m = bt_s[rows, :] + jnp.maximum(m, gm_s[rows, :])
            m1_s[rows, :] = m
        m0 = m0_s[...]
        m1 = m1_s[...]
        btot = bt_s[...]
        mx = jnp.maximum(m0, g)
        pieces = _split3(m0 - mx) + _split3(-(b + mx))
        for k, part in enumerate(pieces):
            tab_ref[d, k] = part.astype(F32)
        tab_ref[d, N_PIECES] = a - m0
        tab_ref[d, N_PIECES + 1] = jnp.exp(btot - b + ig - m1)
        tab_ref[d, N_PIECES + 2] = jnp.exp(btot + m0 - m1)


def _gate_prep(raw, bias):
    L = raw.shape[3]
    rows = raw.shape[2] // SUBLANES * LSTM_HEADS
    buf = lambda *lead: pltpu.VMEM((*lead, rows, L), F32)
    return pl.pallas_call(
        _gate_kernel,
        out_shape=jax.ShapeDtypeStruct((2, N_TAB, rows, L), F32),
        scratch_shapes=[buf(2), buf(), buf(), buf(), buf()],
        compiler_params=pltpu.CompilerParams(vmem_limit_bytes=VMEM_LIMIT),
        name="mlstm_gates",
    )(raw, bias)


def _mlstm_kernel(lq_ref, lk_ref, lv_ref, tab_ref, sel_ref,
                  cw_q_ref, cw_k_ref, cb_q_ref, cb_k_ref, o_ref, xq_s, xk_s, q_s, kt_s, ktb_s, c_s):
    seq = lq_ref.shape[0]
    L = LSTM_CHUNK
    nc = seq // L

    zpad = jnp.zeros((CONV_PAD, LANES), F32)
    for x_s, src_ref in ((xq_s, lq_ref), (xk_s, lk_ref)):
        x_s[0:CONV_PAD, :] = zpad
        x_s[CONV_PAD + seq:CONV_PAD + seq + CONV_PAD, :] = zpad
        x_s[CONV_PAD:CONV_PAD + seq, :] = src_ref[...]

    def conv_silu(x_s, w_ref, b_ref, r0):
        acc = jnp.zeros((CONV_ROWS, LANES), F32) + b_ref[0]
        for j in range(CONV_K):
            start = CONV_PAD + r0 + j - CONV_K // 2
            acc = acc + x_s[start:start + CONV_ROWS, :] * w_ref[0, j:j + 1, :]
        return jax.nn.silu(acc)

    n_blocks = seq // CONV_ROWS
    for i in range(n_blocks // 2):
        for blk in (i, n_blocks - 1 - i):
            r0 = blk * CONV_ROWS
            q_s[r0:r0 + CONV_ROWS, :] = conv_silu(xq_s, cw_q_ref, cb_q_ref, r0)
            k = conv_silu(xk_s, cw_k_ref, cb_k_ref, r0) * (HEAD_DIM ** -0.5)
            for j in range(CONV_ROWS // L):
                kt = k[j * L:(j + 1) * L, :].T
                kt_s[r0 // L + j] = kt
                ktb_s[r0 // L + j] = kt.astype(BF16)

    row_i = lax.broadcasted_iota(jnp.int32, (L, L), 0)
    col_i = lax.broadcasted_iota(jnp.int32, (L, L), 1)
    masks = (col_i <= row_i, col_i >= row_i)
    ones = jnp.ones((L, LANES), BF16)
    pad_rows = jnp.zeros((P_ROWS - N_PIECES, L), F32)
    c_s[...] = jnp.zeros(c_s.shape, F32)

    def chunk_step(d, c, first):
        rows = pl.ds(c * L, L)
        row = pl.ds(c, 1)
        vaug = jnp.concatenate([lv_ref[rows, :], ones], axis=1)
        caug = c_s[d]
        qc = q_s[rows, :]
        qk = _dot(qc.astype(BF16), ktb_s[c])
        tab = [tab_ref[d, k, row, :] for k in range(N_TAB)]
        pieces = jnp.concatenate(tab[:N_PIECES] + [pad_rows], axis=0).astype(BF16)
        e = _dot_tn(pieces, sel_ref[...])
        dmx = e[:, :LANES]
        w = jnp.where(masks[d], jnp.exp(tab[N_PIECES] + dmx), 0.0)
        s = (qk * w).astype(BF16)
        qw = (qc * jnp.exp(dmx)).astype(BF16)
        kw = (kt_s[c] * tab[N_PIECES + 1]).astype(BF16)
        upd = _dot(kw, vaug)
        r = _dot(jnp.concatenate([s, qw], axis=1), jnp.concatenate([vaug, caug.astype(BF16)], axis=0))
        h = r[:, :LANES] / jnp.maximum(jnp.abs(r[:, LANES:]), jnp.exp(e[:, LANES:]))
        o_ref[rows, :] = h if first else o_ref[rows, :] + h
        dec = tab[N_PIECES + 2]
        c_s[d] = jnp.concatenate([dec, dec], axis=1) * caug + upd

    for i in range(nc):
        chunk_step(0, i, first=i < nc - 1 - i)
        chunk_step(1, nc - 1 - i, first=nc - 1 - i > i)


def _mlstm(lq, lk, lv, tab, sel, cw, cb, batch, seq):
    n = lq.shape[0]
    nh = LSTM_HEADS
    nc = seq // LSTM_CHUNK
    head_blk = lambda: pl.BlockSpec((seq, LANES), lambda b, h: (b, h))
    return pl.pallas_call(
        _mlstm_kernel,
        grid=(batch, nh),
        in_specs=[head_blk(), head_blk(), head_blk(),
                  pl.BlockSpec((2, N_TAB, nc, LSTM_CHUNK), lambda b, h: (0, 0, b * nh + h, 0)),
                  pl.BlockSpec((P_ROWS, N_COLQ * LANES), lambda b, h: (0, 0)),
                  pl.BlockSpec((1, CONV_K, LANES), lambda b, h: (h, 0, 0)),
                  pl.BlockSpec((1, CONV_K, LANES), lambda b, h: (nh + h, 0, 0)),
                  pl.BlockSpec((1, 1, LANES), lambda b, h: (h, 0, 0)),
                  pl.BlockSpec((1, 1, LANES), lambda b, h: (nh + h, 0, 0))],
        out_specs=head_blk(),
        out_shape=jax.ShapeDtypeStruct((n, BRANCH_W), F32),
        scratch_shapes=[pltpu.VMEM((seq + 2 * CONV_PAD, LANES), F32),
                        pltpu.VMEM((seq + 2 * CONV_PAD, LANES), F32),
                        pltpu.VMEM((seq, LANES), F32),
                        pltpu.VMEM((nc, HEAD_DIM, LSTM_CHUNK), F32),
                        pltpu.VMEM((nc, HEAD_DIM, LSTM_CHUNK), BF16),
                        pltpu.VMEM((2, HEAD_DIM, 2 * LANES), F32)],
        compiler_params=pltpu.CompilerParams(dimension_semantics=("arbitrary", "arbitrary"),
                                             vmem_limit_bytes=VMEM_LIMIT),
        name="mlstm",
    )(lq, lk, lv, tab, sel, cw, cw, cb, cb)


def _combine_kernel(x_ref, ya_ref, yb_ref, hc_ref, lo_ref, g_ref, hn_ref, wg_hbm, bg_ref, wb_hbm, wo_hbm, o_ref,
                    wg_ref, wb_ref, wo_ref, wg_stage, wb_stage, wo_stage, w_sem, *, layer):
    @pl.when(pl.program_id(0) == 0)
    def _():
        _cast_weight(wg_hbm.at[layer], wg_ref, wg_stage, w_sem.at[0])
        for i in range(N_BRANCH):
            _cast_weight(wb_hbm.at[layer, i], wb_ref.at[i], wb_stage, w_sem.at[1])
        _cast_weight(wo_hbm.at[layer], wo_ref, wo_stage, w_sem.at[2])

    x = x_ref[...]
    xn = _rms(x, g_ref[...]).astype(BF16)
    yc = jnp.concatenate(
        [_rms(hc_ref[:, h * LANES:(h + 1) * LANES], hn_ref[:, h * LANES:(h + 1) * LANES]) for h in range(LSTM_HEADS)],
        axis=1)
    yc = (yc * jax.nn.sigmoid(lo_ref[...])).astype(BF16)
    mix = None
    for i, y in enumerate((ya_ref[...], yb_ref[...], yc)):
        c0 = i * D_MODEL
        gate = jax.nn.sigmoid(_dot(xn, wg_ref[:, c0:c0 + D_MODEL]) + bg_ref[:, c0:c0 + D_MODEL])
        term = gate * _dot(y, wb_ref[i])
        mix = term if mix is None else mix + term
    o_ref[...] = x + _dot(mix.astype(BF16), wo_ref[...])


def _combine(x2, ya, yb, hc, lo, g, hn, wg, bg, wb, wo, layer):
    n = x2.shape[0]
    t = TOK_TILE
    const = lambda *shape: pl.BlockSpec(shape, lambda i: (0,) * len(shape))
    tok = lambda w: pl.BlockSpec((t, w), lambda i: (i, 0))
    hbm = lambda: pl.BlockSpec(memory_space=pl.ANY)
    stage = lambda k, n_: pltpu.VMEM((2, _stage_rows(k, n_), n_), F32)
    return pl.pallas_call(
        functools.partial(_combine_kernel, layer=layer),
        grid=(n // t,),
        in_specs=[tok(D_MODEL), tok(BRANCH_W), tok(BRANCH_W), tok(BRANCH_W), tok(BRANCH_W), const(1, D_MODEL),
                  const(1, BRANCH_W), hbm(), const(1, N_BRANCH * D_MODEL), hbm(), hbm()],
        out_specs=tok(D_MODEL),
        out_shape=jax.ShapeDtypeStruct((n, D_MODEL), F32),
        scratch_shapes=[pltpu.VMEM((D_MODEL, N_BRANCH * D_MODEL), BF16),
                        pltpu.VMEM((N_BRANCH, BRANCH_W, D_MODEL), BF16),
                        pltpu.VMEM((D_MODEL, D_MODEL), BF16),
                        stage(D_MODEL, N_BRANCH * D_MODEL), stage(BRANCH_W, D_MODEL), stage(D_MODEL, D_MODEL),
                        pltpu.SemaphoreType.DMA((3, 2))],
        compiler_params=pltpu.CompilerParams(dimension_semantics=("arbitrary",),
                                             vmem_limit_bytes=VMEM_LIMIT),
        name="combine",
    )(x2, ya, yb, hc, lo, g, hn, wg, bg, wb, wo)


FFN_COL_CHUNK = 256


def _ffn_kernel(x_ref, g_ref, wi_hbm, wo_hbm, o_ref, wi_ref, wo_ref, wi_stage, wo_stage, w_sem, *, layer):
    @pl.when(pl.program_id(0) == 0)
    def _():
        _cast_weight(wi_hbm.at[layer], wi_ref, wi_stage, w_sem.at[0])
        _cast_weight(wo_hbm.at[layer], wo_ref, wo_stage, w_sem.at[1])

    x = x_ref[...]
    xn = _rms(x, g_ref[...]).astype(BF16)
    acc = x
    for c0 in range(0, FFN_HIDDEN, FFN_COL_CHUNK):
        c1 = min(c0 + FFN_COL_CHUNK, FFN_HIDDEN)
        a = jax.nn.silu(_dot(xn, wi_ref[:, c0:c1])) * _dot(xn, wi_ref[:, FFN_HIDDEN + c0:FFN_HIDDEN + c1])
        acc = acc + _dot(a.astype(BF16), wo_ref[c0:c1, :])
    o_ref[...] = acc


def _ffn(x2, g, wi, wo, layer):
    n = x2.shape[0]
    t = TOK_TILE
    const = lambda *shape: pl.BlockSpec(shape, lambda i: (0,) * len(shape))
    tok = lambda w: pl.BlockSpec((t, w), lambda i: (i, 0))
    hbm = lambda: pl.BlockSpec(memory_space=pl.ANY)
    stage = lambda k, n_: pltpu.VMEM((2, _stage_rows(k, n_), n_), F32)
    return pl.pallas_call(
        functools.partial(_ffn_kernel, layer=layer),
        grid=(n // t,),
        in_specs=[tok(D_MODEL), const(1, D_MODEL), hbm(), hbm()],
        out_specs=tok(D_MODEL),
        out_shape=jax.ShapeDtypeStruct((n, D_MODEL), F32),
        scratch_shapes=[pltpu.VMEM((D_MODEL, 2 * FFN_HIDDEN), BF16), pltpu.VMEM((FFN_HIDDEN, D_MODEL), BF16),
                        stage(D_MODEL, 2 * FFN_HIDDEN), stage(FFN_HIDDEN, D_MODEL),
                        pltpu.SemaphoreType.DMA((2, 2))],
        compiler_params=pltpu.CompilerParams(dimension_semantics=("arbitrary",),
                                             vmem_limit_bytes=VMEM_LIMIT),
        name="ffn",
    )(x2, g, wi, wo)


def _rope_tables(seq):
    rows = seq // GRID_W
    row = jnp.repeat(jnp.arange(rows, dtype=F32), GRID_W)
    col = jnp.tile(jnp.arange(GRID_W, dtype=F32), rows)
    axis_dim = HEAD_DIM // 2
    freqs = ROPE_THETA ** (-jnp.arange(axis_dim // 2, dtype=F32) * 2.0 / axis_dim)
    ang = jnp.concatenate([row[:, None] * freqs[None], col[:, None] * freqs[None]], axis=-1)
    cos, sin = jnp.cos(ang), jnp.sin(ang)
    cos_full = jnp.repeat(cos, 2, axis=-1)
    sin_signed = jnp.stack([-sin, sin], axis=-1).reshape(seq, HEAD_DIM)
    return cos_full, sin_signed


def _sel_matrix():
    sel = np.zeros((P_ROWS, N_COLQ * LANES), np.float32)
    for q in range(N_COLQ):
        sel[q * N_SPLIT:(q + 1) * N_SPLIT, q * LANES:(q + 1) * LANES] = 1.0
    return jnp.asarray(sel, BF16)


def kernel(x, norm_mix, w_in, sgu_ln_g, sgu_ln_b, sgu_w, sgu_b, q_norm, k_norm, conv_w, conv_b, igate_b, fgate_b,
           lstm_norm, w_gate, b_gate, w_branch, w_out, norm_ffn, w_ffn_in, w_ffn_out):
    batch, seq, d = x.shape
    depth = norm_mix.shape[0]
    nh = LSTM_HEADS
    L = LSTM_CHUNK
    nc = seq // L
    cos, sin = _rope_tables(seq)
    sel = _sel_matrix()
    x2 = x.reshape(batch * seq, d)
    w_in_b = w_in.astype(BF16)
    for l in range(depth):
        w_g = w_in_b[l, :, N_MAIN:].T.reshape(4, nh, d)
        w_g = jnp.pad(w_g, ((0, 0), (0, SUBLANES - nh), (0, 0))).reshape(4 * SUBLANES, d)
        sgub = jnp.repeat(sgu_b[l].T, LANES, axis=1)
        ya, qa, ka, va, lq, lk, lv, lo, graw = _inproj(
            x2, norm_mix[l][None], w_in_b, l, w_g, sgu_ln_g[l][None], sgu_ln_b[l][None],
            sgu_w[l].astype(BF16), sgub, q_norm[l][None], k_norm[l][None], cos, sin, seq)
        yb = _attention(qa, ka, va, batch, seq)

        gbias = jnp.stack([igate_b[l], fgate_b[l]], axis=0)
        gbias = jnp.broadcast_to(gbias[:, :, None, :, None], (2, 2, batch, nh, L)).reshape(2, 2, batch * nh, L)
        tab = _gate_prep(graw.reshape(2, 2, nc * batch * SUBLANES, L), gbias)
        cw = conv_w[l].reshape(CONV_K, 2 * nh, LANES).transpose(1, 0, 2)
        cb = conv_b[l].reshape(2 * nh, 1, LANES)
        hc = _mlstm(lq, lk, lv, tab, sel, cw, cb, batch, seq)

        x2 = _combine(x2, ya, yb, hc, lo, norm_mix[l][None], lstm_norm[l][None], w_gate,
                      b_gate[l][None], w_branch, w_out, l)
        x2 = _ffn(x2, norm_ffn[l][None], w_ffn_in, w_ffn_out, l)
    return x2.reshape(batch, seq, d)
```

```python
import functools

import jax
import jax.numpy as jnp
import numpy as np
from jax import lax
from jax.experimental import pallas as pl
from jax.experimental.pallas import tpu as pltpu

D_MODEL = 1024
GRID_W = 64
BRANCH_W = 512
N_BRANCH = 3
EPS = 1e-6
SGU_CHUNK = 128
SGU_GROUPS = 4
ATT_HEADS = 4
ATT_KV_HEADS = 2
HEAD_DIM = 128
ROPE_THETA = 10000.0
LSTM_HEADS = 4
LSTM_CHUNK = 128
CONV_K = 5
FFN_HIDDEN = 2816
N_MAIN = 4096

LANES = 128
SUBLANES = 8
VMEM_LIMIT = 56 * 1024 * 1024

TOK_TILE = 512
ATT_Q_TILE = 1024
ATT_KEY_BLOCK = 512
CONV_ROWS = 256
CONV_PAD = SUBLANES

LOG2_E = 1.4426950408889634

BF16 = jnp.bfloat16
F32 = jnp.float32


def _rms(x, g):
    return x * lax.rsqrt(jnp.mean(x * x, axis=-1, keepdims=True) + EPS) * g


def _dot(a, b):
    return jnp.dot(a, b, preferred_element_type=F32)


def _dot_nt(a, b):
    return lax.dot_general(a, b, (((1,), (1,)), ((), ())), preferred_element_type=F32)


def _dot_tn(a, b):
    return lax.dot_general(a, b, (((0,), (0,)), ((), ())), preferred_element_type=F32)


W_STAGE_BYTES = 2 * 1024 * 1024


def _stage_rows(k, n):
    rows = max(16, min(k, W_STAGE_BYTES // (4 * n)) // 16 * 16)
    while k % rows:
        rows -= 16
    return rows


def _cast_weight(w_hbm, w_vmem, stage, sem):
    k = w_vmem.shape[0]
    rows = stage.shape[1]

    def copy(c, slot):
        return pltpu.make_async_copy(w_hbm.at[pl.ds(c * rows, rows)], stage.at[slot], sem.at[slot])

    copy(0, 0).start()
    for c in range(k // rows):
        slot = c % 2
        if c + 1 < k // rows:
            copy(c + 1, 1 - slot).start()
        copy(c, slot).wait()
        w_vmem[c * rows:(c + 1) * rows, :] = stage[slot].astype(BF16)


def _inproj_kernel(x_ref, g_ref, w_ref, wgr_ref, lng_ref, lnb_ref, sguw_ref, sgub_ref,
                   qn_ref, kn_ref, cos_ref, sin_ref,
                   ya_ref, qa_ref, ka_ref, va_ref, lq_ref, lk_ref, lv_ref, lo_ref, grow_ref):
    t = x_ref.shape[0]
    xn = _rms(x_ref[...], g_ref[...]).astype(BF16)

    def seg(lo, hi):
        return _dot(xn, w_ref[:, lo:hi])

    gv = jax.nn.gelu(seg(512, 1024))
    mu = jnp.mean(gv, axis=-1, keepdims=True)
    vc = gv - mu
    v = vc * lax.rsqrt(jnp.mean(vc * vc, axis=-1, keepdims=True) + EPS) * lng_ref[...] + lnb_ref[...]
    vb = v.astype(BF16)
    u = jax.nn.gelu(seg(0, 512))

    cos = cos_ref[...]
    sin = sin_ref[...]
    even = (lax.broadcasted_iota(jnp.int32, (t, LANES), 1) % 2) == 0

    def norm_rope(xh, gain):
        xh = _rms(xh, gain)
        partner = jnp.where(even, pltpu.roll(xh, LANES - 1, 1), pltpu.roll(xh, 1, 1))
        return (xh * cos + partner * sin).astype(BF16)

    aq = seg(1024, 1536)
    for h in range(ATT_HEADS):
        qa_ref[:, h * LANES:(h + 1) * LANES] = norm_rope(aq[:, h * LANES:(h + 1) * LANES], qn_ref[...])
    ak = seg(1536, 1792)
    for h in range(ATT_KV_HEADS):
        ka_ref[:, h * LANES:(h + 1) * LANES] = norm_rope(ak[:, h * LANES:(h + 1) * LANES], kn_ref[...])
    va_ref[...] = seg(1792, 2048).astype(BF16)

    lq_ref[...] = seg(2048, 2560)
    lk_ref[...] = seg(2560, 3072)
    lv_ref[...] = seg(3072, 3584).astype(BF16)
    lo_ref[...] = seg(3584, 4096)
    gt = _dot_nt(wgr_ref[...], xn)
    for kd in range(4):
        for j in range(t // LSTM_CHUNK):
            grow_ref[kd // 2, kd % 2, j, 0] = gt[kd * SUBLANES:(kd + 1) * SUBLANES, j * LANES:(j + 1) * LANES]

    n_chunks = t // SGU_CHUNK
    for grp in range(SGU_GROUPS):
        c0 = grp * LANES
        vcat = jnp.concatenate([vb[j * SGU_CHUNK:(j + 1) * SGU_CHUNK, c0:c0 + LANES] for j in range(n_chunks)], axis=1)
        s = _dot(sguw_ref[grp], vcat)
        for j in range(n_chunks):
            r0 = j * SGU_CHUNK
            sj = s[:, j * LANES:(j + 1) * LANES] + sgub_ref[:, c0:c0 + LANES]
            ya_ref[r0:r0 + SGU_CHUNK, c0:c0 + LANES] = (u[r0:r0 + SGU_CHUNK, c0:c0 + LANES] * sj).astype(BF16)


def _inproj(x2, g, w_in_b, layer, w_gr, lng, lnb, sguw, sgub, qn, kn, cos, sin, seq):
    n = x2.shape[0]
    t = TOK_TILE
    tiles_per_seq = seq // t
    tok = lambda w: pl.BlockSpec((t, w), lambda i: (i, 0))
    rope = pl.BlockSpec((t, LANES), lambda i: (i % tiles_per_seq, 0))
    per_layer = lambda *shape: pl.BlockSpec((None, *shape), lambda i: (layer,) + (0,) * len(shape))
    out_shapes = (
        jax.ShapeDtypeStruct((n, BRANCH_W), BF16),
        jax.ShapeDtypeStruct((n, 512), BF16),
        jax.ShapeDtypeStruct((n, 256), BF16),
        jax.ShapeDtypeStruct((n, 256), BF16),
        jax.ShapeDtypeStruct((n, BRANCH_W), F32),
        jax.ShapeDtypeStruct((n, BRANCH_W), F32),
        jax.ShapeDtypeStruct((n, BRANCH_W), BF16),
        jax.ShapeDtypeStruct((n, BRANCH_W), F32),
        jax.ShapeDtypeStruct((2, 2, seq // LSTM_CHUNK, n // seq, SUBLANES, LSTM_CHUNK), F32),
    )
    out_specs = (tok(512), tok(512), tok(256), tok(256), tok(512), tok(512), tok(512), tok(512),
                 pl.BlockSpec((2, 2, t // LSTM_CHUNK, 1, SUBLANES, LSTM_CHUNK),
                              lambda i: (0, 0, i % tiles_per_seq, i // tiles_per_seq, 0, 0)))
    return pl.pallas_call(
        _inproj_kernel,
        grid=(n // t,),
        in_specs=[tok(D_MODEL), per_layer(1, D_MODEL),
                  per_layer(D_MODEL, N_MAIN),
                  per_layer(4 * SUBLANES, D_MODEL), per_layer(1, BRANCH_W), per_layer(1, BRANCH_W),
                  per_layer(SGU_GROUPS, SGU_CHUNK, SGU_CHUNK), per_layer(SGU_CHUNK, BRANCH_W),
                  per_layer(1, LANES), per_layer(1, LANES), rope, rope],
        out_specs=out_specs,
        out_shape=out_shapes,
        compiler_params=pltpu.CompilerParams(dimension_semantics=("arbitrary",),
                                             vmem_limit_bytes=VMEM_LIMIT),
        name="inproj",
    )(x2, g, w_in_b, w_gr, lng, lnb, sguw, sgub, qn, kn, cos, sin)


def _attn_kernel(q_ref, k_ref, v_ref, o_ref):
    exp2_scale = (HEAD_DIM ** -0.5) * LOG2_E
    rep = ATT_HEADS // ATT_KV_HEADS
    tq = q_ref.shape[0]
    kb = ATT_KEY_BLOCK
    ones = jnp.ones((kb, LANES), BF16)
    qs, ms, accs = [], [], []
    for grp in range(ATT_KV_HEADS):
        c0 = grp * rep * LANES
        qs.append(jnp.concatenate([q_ref[:, c0 + r * LANES:c0 + (r + 1) * LANES] for r in range(rep)], axis=0))
    for j in range(k_ref.shape[0] // kb):
        for grp in range(ATT_KV_HEADS):
            kblk = k_ref[j * kb:(j + 1) * kb, grp * LANES:(grp + 1) * LANES]
            vaug = jnp.concatenate([v_ref[j * kb:(j + 1) * kb, grp * LANES:(grp + 1) * LANES], ones], axis=1)
            s = _dot_nt(qs[grp], kblk)
            bmax = jnp.max(s, axis=-1, keepdims=True)
            if j == 0:
                ms.append(bmax)
                accs.append(_dot(jnp.exp2((s - bmax) * exp2_scale).astype(BF16), vaug))
            else:
                m_new = jnp.maximum(ms[grp], bmax)
                alpha = jnp.exp2((ms[grp] - m_new) * exp2_scale)
                p = jnp.exp2((s - m_new) * exp2_scale).astype(BF16)
                accs[grp] = alpha * accs[grp] + _dot(p, vaug)
                ms[grp] = m_new
    for grp in range(ATT_KV_HEADS):
        c0 = grp * rep * LANES
        o = accs[grp][:, :LANES] / accs[grp][:, LANES:]
        for r in range(rep):
            o_ref[:, c0 + r * LANES:c0 + (r + 1) * LANES] = o[r * tq:(r + 1) * tq].astype(BF16)


def _attention(qa, ka, va, batch, seq):
    n = qa.shape[0]
    tq = ATT_Q_TILE
    qpb = seq // tq
    return pl.pallas_call(
        _attn_kernel,
        grid=(batch, qpb),
        in_specs=[pl.BlockSpec((tq, 512), lambda b, i: (b * qpb + i, 0)),
                  pl.BlockSpec((seq, 256), lambda b, i: (b, 0)),
                  pl.BlockSpec((seq, 256), lambda b, i: (b, 0))],
        out_specs=pl.BlockSpec((tq, 512), lambda b, i: (b * qpb + i, 0)),
        out_shape=jax.ShapeDtypeStruct((n, BRANCH_W), BF16),
        compiler_params=pltpu.CompilerParams(dimension_semantics=("arbitrary", "arbitrary"),
                                             vmem_limit_bytes=VMEM_LIMIT),
        name="gqa",
    )(qa, ka, va)


def _log_sigmoid(x):
    return jnp.minimum(x, 0.0) - jnp.log1p(jnp.exp(-jnp.abs(x)))


def _lane_scan(x, op, identity, reverse):
    lane = lax.broadcasted_iota(jnp.int32, x.shape, 1)
    k = 1
    while k < LANES:
        if reverse:
            x = op(x, jnp.where(lane < LANES - k, pltpu.roll(x, LANES - k, 1), identity))
        else:
            x = op(x, jnp.where(lane >= k, pltpu.roll(x, k, 1), identity))
        k *= 2
    return x


N_SPLIT = 3
N_COLQ = 2
N_PIECES = N_COLQ * N_SPLIT
N_TAB = N_PIECES + 3
P_ROWS = 16


def _split3(x):
    hi = x.astype(BF16)
    r = x - hi.astype(F32)
    mid = r.astype(BF16)
    lo = (r - mid.astype(F32)).astype(BF16)
    return [hi, mid, lo]


def _gate_kernel(raw_ref, bias_ref, tab_ref, rm_s, bt_s, gm_s, m0_s, m1_s):
    nr = bias_ref.shape[2]
    slots = nr // LSTM_HEADS * SUBLANES
    nc = raw_ref.shape[2] // slots
    L = LSTM_CHUNK
    for d in range(2):
        rev = d == 1
        last = 0 if rev else L - 1
        for k in range(2):
            for r in range(nr):
                slot = r // LSTM_HEADS * SUBLANES + r % LSTM_HEADS
                rm_s[k, r * nc:(r + 1) * nc, :] = (raw_ref[k, d, pl.ds(slot, nc, stride=slots), :]
                                                   + bias_ref[k, d, r:r + 1, :])
        ig = rm_s[0]
        lf = _log_sigmoid(rm_s[1])
        b = _lane_scan(lf, jnp.add, 0.0, rev)
        a = ig - b
        g = _lane_scan(a, jnp.maximum, -jnp.inf, rev)
        bt_s[...] = jnp.broadcast_to(b[:, last:last + 1], b.shape)
        gm_s[...] = jnp.broadcast_to(g[:, last:last + 1], g.shape)
        m = jnp.zeros((nr, L), F32)
        for c in (range(nc - 1, -1, -1) if rev else range(nc)):
            rows = pl.ds(c, nr, stride=nc)
            m0_s[rows, :] = m
            m = bt_s[rows, :] + jnp.maximum(m, gm_s[rows, :])
            m1_s[rows, :] = m
        m0 = m0_s[...]
        m1 = m1_s[...]
        btot = bt_s[...]
        mx = jnp.maximum(m0, g)
        pieces = _split3(m0 - mx) + _split3(-(b + mx))
        for k, part in enumerate(pieces):
            tab_ref[d, k] = part.astype(F32)
        tab_ref[d, N_PIECES] = a - m0
        tab_ref[d, N_PIECES + 1] = jnp.exp(btot - b + ig - m1)
        tab_ref[d, N_PIECES + 2] = jnp.exp(btot + m0 - m1)


def _gate_prep(raw, bias):
    L = raw.shape[3]
    rows = raw.shape[2] // SUBLANES * LSTM_HEADS
    buf = lambda *lead: pltpu.VMEM((*lead, rows, L), F32)
    return pl.pallas_call(
        _gate_kernel,
        out_shape=jax.ShapeDtypeStruct((2, N_TAB, rows, L), F32),
        scratch_shapes=[buf(2), buf(), buf(), buf(), buf()],
        compiler_params=pltpu.CompilerParams(vmem_limit_bytes=VMEM_LIMIT),
        name="mlstm_gates",
    )(raw, bias)


def _mlstm_kernel(lq_ref, lk_ref, lv_ref, tab_ref, sel_ref,
                  cw_q_ref, cw_k_ref, cb_q_ref, cb_k_ref, o_ref, xq_s, xk_s, q_s, kt_s, ktb_s, c_s):
    seq = lq_ref.shape[0]
    L = LSTM_CHUNK
    nc = seq // L

    zpad = jnp.zeros((CONV_PAD, LANES), F32)
    for x_s, src_ref in ((xq_s, lq_ref), (xk_s, lk_ref)):
        x_s[0:CONV_PAD, :] = zpad
        x_s[CONV_PAD + seq:CONV_PAD + seq + CONV_PAD, :] = zpad
        x_s[CONV_PAD:CONV_PAD + seq, :] = src_ref[...]

    def conv_silu(x_s, w_ref, b_ref, r0):
        acc = jnp.zeros((CONV_ROWS, LANES), F32) + b_ref[0]
        for j in range(CONV_K):
            start = CONV_PAD + r0 + j - CONV_K // 2
            acc = acc + x_s[start:start + CONV_ROWS, :] * w_ref[0, j:j + 1, :]
        return jax.nn.silu(acc)

    n_blocks = seq // CONV_ROWS
    for i in range(n_blocks // 2):
        for blk in (i, n_blocks - 1 - i):
            r0 = blk * CONV_ROWS
            q_s[r0:r0 + CONV_ROWS, :] = conv_silu(xq_s, cw_q_ref, cb_q_ref, r0)
            k = conv_silu(xk_s, cw_k_ref, cb_k_ref, r0) * (HEAD_DIM ** -0.5)
            for j in range(CONV_ROWS // L):
                kt = k[j * L:(j + 1) * L, :].T
                kt_s[r0 // L + j] = kt
                ktb_s[r0 // L + j] = kt.astype(BF16)

    row_i = lax.broadcasted_iota(jnp.int32, (L, L), 0)
    col_i = lax.broadcasted_iota(jnp.int32, (L, L), 1)
    masks = (col_i <= row_i, col_i >= row_i)
    ones = jnp.ones((L, LANES), BF16)
    pad_rows = jnp.zeros((P_ROWS - N_PIECES, L), F32)
    c_s[...] = jnp.zeros(c_s.shape, F32)

    def chunk_step(d, c, first):
        rows = pl.ds(c * L, L)
        row = pl.ds(c, 1)
        vaug = jnp.concatenate([lv_ref[rows, :], ones], axis=1)
        caug = c_s[d]
        qc = q_s[rows, :]
        qk = _dot(qc.astype(BF16), ktb_s[c])
        tab = [tab_ref[d, k, row, :] for k in range(N_TAB)]
        pieces = jnp.concatenate(tab[:N_PIECES] + [pad_rows], axis=0).astype(BF16)
        e = _dot_tn(pieces, sel_ref[...])
        dmx = e[:, :LANES]
        w = jnp.where(masks[d], jnp.exp(tab[N_PIECES] + dmx), 0.0)
        s = (qk * w).astype(BF16)
        qw = (qc * jnp.exp(dmx)).astype(BF16)
        kw = (kt_s[c] * tab[N_PIECES + 1]).astype(BF16)
        upd = _dot(kw, vaug)
        r = _dot(jnp.concatenate([s, qw], axis=1), jnp.concatenate([vaug, caug.astype(BF16)], axis=0))
        h = r[:, :LANES] / jnp.maximum(jnp.abs(r[:, LANES:]), jnp.exp(e[:, LANES:]))
        o_ref[rows, :] = h if first else o_ref[rows, :] + h
        dec = tab[N_PIECES + 2]
        c_s[d] = jnp.concatenate([dec, dec], axis=1) * caug + upd

    for i in range(nc):
        chunk_step(0, i, first=i < nc - 1 - i)
        chunk_step(1, nc - 1 - i, first=nc - 1 - i > i)


def _mlstm(lq, lk, lv, tab, sel, cw, cb, layer, batch, seq):
    n = lq.shape[0]
    nh = LSTM_HEADS
    nc = seq // LSTM_CHUNK
    head_blk = lambda: pl.BlockSpec((seq, LANES), lambda b, h: (b, h))
    return pl.pallas_call(
        _mlstm_kernel,
        grid=(batch, nh),
        in_specs=[head_blk(), head_blk(), head_blk(),
                  pl.BlockSpec((2, N_TAB, nc, LSTM_CHUNK), lambda b, h: (0, 0, b * nh + h, 0)),
                  pl.BlockSpec((P_ROWS, N_COLQ * LANES), lambda b, h: (0, 0)),
                  pl.BlockSpec((None, 1, CONV_K, LANES), lambda b, h: (layer, h, 0, 0)),
                  pl.BlockSpec((None, 1, CONV_K, LANES), lambda b, h: (layer, nh + h, 0, 0)),
                  pl.BlockSpec((None, 1, 1, LANES), lambda b, h: (layer, h, 0, 0)),
                  pl.BlockSpec((None, 1, 1, LANES), lambda b, h: (layer, nh + h, 0, 0))],
        out_specs=head_blk(),
        out_shape=jax.ShapeDtypeStruct((n, BRANCH_W), F32),
        scratch_shapes=[pltpu.VMEM((seq + 2 * CONV_PAD, LANES), F32),
                        pltpu.VMEM((seq + 2 * CONV_PAD, LANES), F32),
                        pltpu.VMEM((seq, LANES), F32),
                        pltpu.VMEM((nc, HEAD_DIM, LSTM_CHUNK), F32),
                        pltpu.VMEM((nc, HEAD_DIM, LSTM_CHUNK), BF16),
                        pltpu.VMEM((2, HEAD_DIM, 2 * LANES), F32)],
        compiler_params=pltpu.CompilerParams(dimension_semantics=("arbitrary", "arbitrary"),
                                             vmem_limit_bytes=VMEM_LIMIT),
        name="mlstm",
    )(lq, lk, lv, tab, sel, cw, cw, cb, cb)


COMBINE_COL_CHUNK = 256


def _combine_kernel(x_ref, ya_ref, yb_ref, hc_ref, lo_ref, g_ref, hn_ref, wg_hbm, bg_ref, wb_hbm, wo_hbm, o_ref,
                    wg_ref, wb_ref, wo_ref, wg_stage, wb_stage, wo_stage, w_sem, *, layer):
    @pl.when(pl.program_id(0) == 0)
    def _():
        _cast_weight(wg_hbm.at[layer], wg_ref, wg_stage, w_sem.at[0])
        for i in range(N_BRANCH):
            _cast_weight(wb_hbm.at[layer, i], wb_ref.at[i], wb_stage, w_sem.at[1])
        _cast_weight(wo_hbm.at[layer], wo_ref, wo_stage, w_sem.at[2])

    x = x_ref[...]
    xn = _rms(x, g_ref[...]).astype(BF16)
    yc = jnp.concatenate(
        [_rms(hc_ref[:, h * LANES:(h + 1) * LANES], hn_ref[:, h * LANES:(h + 1) * LANES]) for h in range(LSTM_HEADS)],
        axis=1)
    yc = (yc * jax.nn.sigmoid(lo_ref[...])).astype(BF16)
    ys = (ya_ref[...], yb_ref[...], yc)
    acc = x
    for k0 in range(0, D_MODEL, COMBINE_COL_CHUNK):
        k1 = k0 + COMBINE_COL_CHUNK
        mix = None
        for i, y in enumerate(ys):
            c0 = i * D_MODEL + k0
            gate = jax.nn.sigmoid(_dot(xn, wg_ref[:, c0:c0 + COMBINE_COL_CHUNK]) + bg_ref[:, c0:c0 + COMBINE_COL_CHUNK])
            term = gate * _dot(y, wb_ref[i, :, k0:k1])
            mix = term if mix is None else mix + term
        acc = acc + _dot(mix.astype(BF16), wo_ref[k0:k1, :])
    o_ref[...] = acc


def _combine(x2, ya, yb, hc, lo, g, hn, wg, bg, wb, wo, layer):
    n = x2.shape[0]
    t = TOK_TILE
    tok = lambda w: pl.BlockSpec((t, w), lambda i: (i, 0))
    hbm = lambda: pl.BlockSpec(memory_space=pl.ANY)
    row = lambda w: pl.BlockSpec((None, 1, w), lambda i: (layer, 0, 0))
    stage = lambda k, n_: pltpu.VMEM((2, _stage_rows(k, n_), n_), F32)
    return pl.pallas_call(
        functools.partial(_combine_kernel, layer=layer),
        grid=(n // t,),
        in_specs=[tok(D_MODEL), tok(BRANCH_W), tok(BRANCH_W), tok(BRANCH_W), tok(BRANCH_W), row(D_MODEL),
                  row(BRANCH_W), hbm(), row(N_BRANCH * D_MODEL), hbm(), hbm()],
        out_specs=tok(D_MODEL),
        out_shape=jax.ShapeDtypeStruct((n, D_MODEL), F32),
        scratch_shapes=[pltpu.VMEM((D_MODEL, N_BRANCH * D_MODEL), BF16),
                        pltpu.VMEM((N_BRANCH, BRANCH_W, D_MODEL), BF16),
                        pltpu.VMEM((D_MODEL, D_MODEL), BF16),
                        stage(D_MODEL, N_BRANCH * D_MODEL), stage(BRANCH_W, D_MODEL), stage(D_MODEL, D_MODEL),
                        pltpu.SemaphoreType.DMA((3, 2))],
        compiler_params=pltpu.CompilerParams(dimension_semantics=("arbitrary",),
                                             vmem_limit_bytes=VMEM_LIMIT),
        name="combine",
    )(x2, ya, yb, hc, lo, g, hn, wg, bg, wb, wo)


FFN_COL_CHUNK = 256


def _ffn_kernel(x_ref, g_ref, wi_hbm, wo_hbm, o_ref, wi_ref, wo_ref, wi_stage, wo_stage, w_sem, *, layer):
    @pl.when(pl.program_id(0) == 0)
    def _():
        _cast_weight(wi_hbm.at[layer], wi_ref, wi_stage, w_sem.at[0])
        _cast_weight(wo_hbm.at[layer], wo_ref, wo_stage, w_sem.at[1])

    x = x_ref[...]
    xn = _rms(x, g_ref[...]).astype(BF16)
    acc = x
    for c0 in range(0, FFN_HIDDEN, FFN_COL_CHUNK):
        c1 = min(c0 + FFN_COL_CHUNK, FFN_HIDDEN)
        a = jax.nn.silu(_dot(xn, wi_ref[:, c0:c1])) * _dot(xn, wi_ref[:, FFN_HIDDEN + c0:FFN_HIDDEN + c1])
        acc = acc + _dot(a.astype(BF16), wo_ref[c0:c1, :])
    o_ref[...] = acc


def _ffn(x2, g, wi, wo, layer):
    n = x2.shape[0]
    t = TOK_TILE
    tok = lambda w: pl.BlockSpec((t, w), lambda i: (i, 0))
    hbm = lambda: pl.BlockSpec(memory_space=pl.ANY)
    stage = lambda k, n_: pltpu.VMEM((2, _stage_rows(k, n_), n_), F32)
    return pl.pallas_call(
        functools.partial(_ffn_kernel, layer=layer),
        grid=(n // t,),
        in_specs=[tok(D_MODEL), pl.BlockSpec((None, 1, D_MODEL), lambda i: (layer, 0, 0)), hbm(), hbm()],
        out_specs=tok(D_MODEL),
        out_shape=jax.ShapeDtypeStruct((n, D_MODEL), F32),
        scratch_shapes=[pltpu.VMEM((D_MODEL, 2 * FFN_HIDDEN), BF16), pltpu.VMEM((FFN_HIDDEN, D_MODEL), BF16),
                        stage(D_MODEL, 2 * FFN_HIDDEN), stage(FFN_HIDDEN, D_MODEL),
                        pltpu.SemaphoreType.DMA((2, 2))],
        compiler_params=pltpu.CompilerParams(dimension_semantics=("arbitrary",),
                                             vmem_limit_bytes=VMEM_LIMIT),
        name="ffn",
    )(x2, g, wi, wo)


def _rope_tables(seq):
    rows = seq // GRID_W
    row = jnp.repeat(jnp.arange(rows, dtype=F32), GRID_W)
    col = jnp.tile(jnp.arange(GRID_W, dtype=F32), rows)
    axis_dim = HEAD_DIM // 2
    freqs = ROPE_THETA ** (-jnp.arange(axis_dim // 2, dtype=F32) * 2.0 / axis_dim)
    ang = jnp.concatenate([row[:, None] * freqs[None], col[:, None] * freqs[None]], axis=-1)
    cos, sin = jnp.cos(ang), jnp.sin(ang)
    cos_full = jnp.repeat(cos, 2, axis=-1)
    sin_signed = jnp.stack([-sin, sin], axis=-1).reshape(seq, HEAD_DIM)
    return cos_full, sin_signed


def _sel_matrix():
    sel = np.zeros((P_ROWS, N_COLQ * LANES), np.float32)
    for q in range(N_COLQ):
        sel[q * N_SPLIT:(q + 1) * N_SPLIT, q * LANES:(q + 1) * LANES] = 1.0
    return jnp.asarray(sel, BF16)


def kernel(x, norm_mix, w_in, sgu_ln_g, sgu_ln_b, sgu_w, sgu_b, q_norm, k_norm, conv_w, conv_b, igate_b, fgate_b,
           lstm_norm, w_gate, b_gate, w_branch, w_out, norm_ffn, w_ffn_in, w_ffn_out):
    batch, seq, d = x.shape
    depth = norm_mix.shape[0]
    nh = LSTM_HEADS
    L = LSTM_CHUNK
    nc = seq // L
    cos, sin = _rope_tables(seq)
    sel = _sel_matrix()
    x2 = x.reshape(batch * seq, d)
    rows = lambda p: p[:, None, :]
    w_in_b = w_in.astype(BF16)
    w_g = jnp.swapaxes(w_in_b[:, :, N_MAIN:], 1, 2).reshape(depth, 4, nh, d)
    w_g = jnp.pad(w_g, ((0, 0), (0, 0), (0, SUBLANES - nh), (0, 0))).reshape(depth, 4 * SUBLANES, d)
    sgub = jnp.repeat(jnp.swapaxes(sgu_b, 1, 2), LANES, axis=2)
    sguw = sgu_w.astype(BF16)
    gbias = jnp.stack([igate_b, fgate_b], axis=1)
    gbias = jnp.broadcast_to(gbias[:, :, :, None, :, None], (depth, 2, 2, batch, nh, L))
    gbias = gbias.reshape(depth, 2, 2, batch * nh, L)
    cw = conv_w.reshape(depth, CONV_K, 2 * nh, LANES).transpose(0, 2, 1, 3)
    cb = conv_b.reshape(depth, 2 * nh, 1, LANES)
    for l in range(depth):
        ya, qa, ka, va, lq, lk, lv, lo, graw = _inproj(
            x2, rows(norm_mix), w_in_b, l, w_g, rows(sgu_ln_g), rows(sgu_ln_b),
            sguw, sgub, rows(q_norm), rows(k_norm), cos, sin, seq)
        yb = _attention(qa, ka, va, batch, seq)

        tab = _gate_prep(graw.reshape(2, 2, nc * batch * SUBLANES, L), gbias[l])
        hc = _mlstm(lq, lk, lv, tab, sel, cw, cb, l, batch, seq)

        x2 = _combine(x2, ya, yb, hc, lo, rows(norm_mix), rows(lstm_norm), w_gate,
                      rows(b_gate), w_branch, w_out, l)
        x2 = _ffn(x2, rows(norm_ffn), w_ffn_in, w_ffn_out, l)
    return x2.reshape(batch, seq, d)
```

```python
import functools

import jax
import jax.numpy as jnp
import numpy as np
from jax import lax
from jax.experimental import pallas as pl
from jax.experimental.pallas import tpu as pltpu

D_MODEL = 1024
GRID_W = 64
BRANCH_W = 512
N_BRANCH = 3
EPS = 1e-6
SGU_CHUNK = 128
SGU_GROUPS = 4
ATT_HEADS = 4
ATT_KV_HEADS = 2
HEAD_DIM = 128
ROPE_THETA = 10000.0
LSTM_HEADS = 4
LSTM_CHUNK = 128
CONV_K = 5
FFN_HIDDEN = 2816
N_MAIN = 4096

LANES = 128
SUBLANES = 8
VMEM_LIMIT = 56 * 1024 * 1024

TOK_TILE = 512
ATT_Q_TILE = 1024
ATT_KEY_BLOCK = 256
CONV_ROWS = 256
CONV_PAD = SUBLANES

LOG2_E = 1.4426950408889634

BF16 = jnp.bfloat16
F32 = jnp.float32


def _rms(x, g):
    return x * lax.rsqrt(jnp.mean(x * x, axis=-1, keepdims=True) + EPS) * g


def _dot(a, b):
    return jnp.dot(a, b, preferred_element_type=F32)


def _dot_nt(a, b):
    return lax.dot_general(a, b, (((1,), (1,)), ((), ())), preferred_element_type=F32)


def _dot_tn(a, b):
    return lax.dot_general(a, b, (((0,), (0,)), ((), ())), preferred_element_type=F32)


W_STAGE_BYTES = 4 * 1024 * 1024


def _stage_rows(k, n):
    rows = max(16, min(k, W_STAGE_BYTES // (4 * n)) // 16 * 16)
    while k % rows:
        rows -= 16
    return rows


def _cast_weight(w_hbm, w_vmem, stage, sem):
    k = w_vmem.shape[0]
    rows = stage.shape[1]

    def copy(c, slot):
        return pltpu.make_async_copy(w_hbm.at[pl.ds(c * rows, rows)], stage.at[slot], sem.at[slot])

    copy(0, 0).start()
    for c in range(k // rows):
        slot = c % 2
        if c + 1 < k // rows:
            copy(c + 1, 1 - slot).start()
        copy(c, slot).wait()
        w_vmem[c * rows:(c + 1) * rows, :] = stage[slot].astype(BF16)


def _inproj_kernel(x_ref, g_ref, w_ref, wgr_ref, lng_ref, lnb_ref, sguw_ref, sgub_ref,
                   qn_ref, kn_ref, cos_ref, sin_ref,
                   ya_ref, qa_ref, ka_ref, va_ref, lq_ref, lk_ref, lv_ref, lo_ref, grow_ref):
    t = x_ref.shape[0]
    xn = _rms(x_ref[...], g_ref[...]).astype(BF16)

    def seg(lo, hi):
        return _dot(xn, w_ref[:, lo:hi])

    gv = jax.nn.gelu(seg(512, 1024))
    mu = jnp.mean(gv, axis=-1, keepdims=True)
    vc = gv - mu
    v = vc * lax.rsqrt(jnp.mean(vc * vc, axis=-1, keepdims=True) + EPS) * lng_ref[...] + lnb_ref[...]
    vb = v.astype(BF16)
    u = jax.nn.gelu(seg(0, 512))

    cos = cos_ref[...]
    sin = sin_ref[...]
    even = (lax.broadcasted_iota(jnp.int32, (t, LANES), 1) % 2) == 0

    def norm_rope(xh, gain):
        xh = _rms(xh, gain)
        partner = jnp.where(even, pltpu.roll(xh, LANES - 1, 1), pltpu.roll(xh, 1, 1))
        return (xh * cos + partner * sin).astype(BF16)

    aq = seg(1024, 1536)
    for h in range(ATT_HEADS):
        qa_ref[:, h * LANES:(h + 1) * LANES] = norm_rope(aq[:, h * LANES:(h + 1) * LANES], qn_ref[...])
    ak = seg(1536, 1792)
    for h in range(ATT_KV_HEADS):
        ka_ref[:, h * LANES:(h + 1) * LANES] = norm_rope(ak[:, h * LANES:(h + 1) * LANES], kn_ref[...])
    va_ref[...] = seg(1792, 2048).astype(BF16)

    lq_ref[...] = seg(2048, 2560)
    lk_ref[...] = seg(2560, 3072)
    lv_ref[...] = seg(3072, 3584).astype(BF16)
    lo_ref[...] = seg(3584, 4096)
    gt = _dot_nt(wgr_ref[...], xn)
    for kd in range(4):
        for j in range(t // LSTM_CHUNK):
            grow_ref[kd // 2, kd % 2, j, 0] = gt[kd * SUBLANES:(kd + 1) * SUBLANES, j * LANES:(j + 1) * LANES]

    n_chunks = t // SGU_CHUNK
    for grp in range(SGU_GROUPS):
        c0 = grp * LANES
        vcat = jnp.concatenate([vb[j * SGU_CHUNK:(j + 1) * SGU_CHUNK, c0:c0 + LANES] for j in range(n_chunks)], axis=1)
        s = _dot(sguw_ref[grp], vcat)
        for j in range(n_chunks):
            r0 = j * SGU_CHUNK
            sj = s[:, j * LANES:(j + 1) * LANES] + sgub_ref[:, c0:c0 + LANES]
            ya_ref[r0:r0 + SGU_CHUNK, c0:c0 + LANES] = (u[r0:r0 + SGU_CHUNK, c0:c0 + LANES] * sj).astype(BF16)


def _inproj(x2, g, w_in_b, layer, w_gr, lng, lnb, sguw, sgub, qn, kn, cos, sin, seq):
    n = x2.shape[0]
    t = TOK_TILE
    tiles_per_seq = seq // t
    tok = lambda w: pl.BlockSpec((t, w), lambda i: (i, 0))
    rope = pl.BlockSpec((t, LANES), lambda i: (i % tiles_per_seq, 0))
    per_layer = lambda *shape: pl.BlockSpec((None, *shape), lambda i: (layer,) + (0,) * len(shape))
    out_shapes = (
        jax.ShapeDtypeStruct((n, BRANCH_W), BF16),
        jax.ShapeDtypeStruct((n, 512), BF16),
        jax.ShapeDtypeStruct((n, 256), BF16),
        jax.ShapeDtypeStruct((n, 256), BF16),
        jax.ShapeDtypeStruct((n, BRANCH_W), F32),
        jax.ShapeDtypeStruct((n, BRANCH_W), F32),
        jax.ShapeDtypeStruct((n, BRANCH_W), BF16),
        jax.ShapeDtypeStruct((n, BRANCH_W), F32),
        jax.ShapeDtypeStruct((2, 2, seq // LSTM_CHUNK, n // seq, SUBLANES, LSTM_CHUNK), F32),
    )
    out_specs = (tok(512), tok(512), tok(256), tok(256), tok(512), tok(512), tok(512), tok(512),
                 pl.BlockSpec((2, 2, t // LSTM_CHUNK, 1, SUBLANES, LSTM_CHUNK),
                              lambda i: (0, 0, i % tiles_per_seq, i // tiles_per_seq, 0, 0)))
    return pl.pallas_call(
        _inproj_kernel,
        grid=(n // t,),
        in_specs=[tok(D_MODEL), per_layer(1, D_MODEL),
                  per_layer(D_MODEL, N_MAIN),
                  per_layer(4 * SUBLANES, D_MODEL), per_layer(1, BRANCH_W), per_layer(1, BRANCH_W),
                  per_layer(SGU_GROUPS, SGU_CHUNK, SGU_CHUNK), per_layer(SGU_CHUNK, BRANCH_W),
                  per_layer(1, LANES), per_layer(1, LANES), rope, rope],
        out_specs=out_specs,
        out_shape=out_shapes,
        compiler_params=pltpu.CompilerParams(dimension_semantics=("arbitrary",),
                                             vmem_limit_bytes=VMEM_LIMIT),
        name="inproj",
    )(x2, g, w_in_b, w_gr, lng, lnb, sguw, sgub, qn, kn, cos, sin)


def _attn_kernel(q_ref, k_ref, v_ref, o_ref):
    exp2_scale = (HEAD_DIM ** -0.5) * LOG2_E
    rep = ATT_HEADS // ATT_KV_HEADS
    tq = q_ref.shape[0]
    kb = ATT_KEY_BLOCK
    ones = jnp.ones((kb, LANES), BF16)
    qs, ms, accs = [], [], []
    for grp in range(ATT_KV_HEADS):
        c0 = grp * rep * LANES
        qs.append(jnp.concatenate([q_ref[:, c0 + r * LANES:c0 + (r + 1) * LANES] for r in range(rep)], axis=0))
    for j in range(k_ref.shape[0] // kb):
        for grp in range(ATT_KV_HEADS):
            kblk = k_ref[j * kb:(j + 1) * kb, grp * LANES:(grp + 1) * LANES]
            vaug = jnp.concatenate([v_ref[j * kb:(j + 1) * kb, grp * LANES:(grp + 1) * LANES], ones], axis=1)
            s = _dot_nt(qs[grp], kblk)
            bmax = jnp.max(s, axis=-1, keepdims=True)
            if j == 0:
                ms.append(bmax)
                accs.append(_dot(jnp.exp2((s - bmax) * exp2_scale).astype(BF16), vaug))
            else:
                m_new = jnp.maximum(ms[grp], bmax)
                alpha = jnp.exp2((ms[grp] - m_new) * exp2_scale)
                p = jnp.exp2((s - m_new) * exp2_scale).astype(BF16)
                accs[grp] = alpha * accs[grp] + _dot(p, vaug)
                ms[grp] = m_new
    for grp in range(ATT_KV_HEADS):
        c0 = grp * rep * LANES
        o = accs[grp][:, :LANES] / accs[grp][:, LANES:]
        for r in range(rep):
            o_ref[:, c0 + r * LANES:c0 + (r + 1) * LANES] = o[r * tq:(r + 1) * tq].astype(BF16)


def _attention(qa, ka, va, batch, seq):
    n = qa.shape[0]
    tq = ATT_Q_TILE
    qpb = seq // tq
    return pl.pallas_call(
        _attn_kernel,
        grid=(batch, qpb),
        in_specs=[pl.BlockSpec((tq, 512), lambda b, i: (b * qpb + i, 0)),
                  pl.BlockSpec((seq, 256), lambda b, i: (b, 0)),
                  pl.BlockSpec((seq, 256), lambda b, i: (b, 0))],
        out_specs=pl.BlockSpec((tq, 512), lambda b, i: (b * qpb + i, 0)),
        out_shape=jax.ShapeDtypeStruct((n, BRANCH_W), BF16),
        compiler_params=pltpu.CompilerParams(dimension_semantics=("arbitrary", "arbitrary"),
                                             vmem_limit_bytes=VMEM_LIMIT),
        name="gqa",
    )(qa, ka, va)


def _log_sigmoid(x):
    return jnp.minimum(x, 0.0) - jnp.log1p(jnp.exp(-jnp.abs(x)))


def _lane_scan(x, op, identity, reverse):
    lane = lax.broadcasted_iota(jnp.int32, x.shape, 1)
    k = 1
    while k < LANES:
        if reverse:
            x = op(x, jnp.where(lane < LANES - k, pltpu.roll(x, LANES - k, 1), identity))
        else:
            x = op(x, jnp.where(lane >= k, pltpu.roll(x, k, 1), identity))
        k *= 2
    return x


N_SPLIT = 3
N_COLQ = 2
N_PIECES = N_COLQ * N_SPLIT
N_TAB = N_PIECES + 3
P_ROWS = 16


def _split3(x):
    hi = x.astype(BF16)
    r = x - hi.astype(F32)
    mid = r.astype(BF16)
    lo = (r - mid.astype(F32)).astype(BF16)
    return [hi, mid, lo]


def _gate_kernel(raw_ref, bias_ref, tab_ref, rm_s, bt_s, gm_s, m0_s, m1_s):
    nr = bias_ref.shape[2]
    slots = nr // LSTM_HEADS * SUBLANES
    nc = raw_ref.shape[2] // slots
    L = LSTM_CHUNK
    for d in range(2):
        rev = d == 1
        last = 0 if rev else L - 1
        for k in range(2):
            for r in range(nr):
                slot = r // LSTM_HEADS * SUBLANES + r % LSTM_HEADS
                rm_s[k, r * nc:(r + 1) * nc, :] = (raw_ref[k, d, pl.ds(slot, nc, stride=slots), :]
                                                   + bias_ref[k, d, r:r + 1, :])
        ig = rm_s[0]
        lf = _log_sigmoid(rm_s[1])
        b = _lane_scan(lf, jnp.add, 0.0, rev)
        a = ig - b
        g = _lane_scan(a, jnp.maximum, -jnp.inf, rev)
        bt_s[...] = jnp.broadcast_to(b[:, last:last + 1], b.shape)
        gm_s[...] = jnp.broadcast_to(g[:, last:last + 1], g.shape)
        m = jnp.zeros((nr, L), F32)
        for c in (range(nc - 1, -1, -1) if rev else range(nc)):
            rows = pl.ds(c, nr, stride=nc)
            m0_s[rows, :] = m
            m = bt_s[rows, :] + jnp.maximum(m, gm_s[rows, :])
            m1_s[rows, :] = m
        m0 = m0_s[...]
        m1 = m1_s[...]
        btot = bt_s[...]
        mx = jnp.maximum(m0, g)
        pieces = _split3(m0 - mx) + _split3(-(b + mx))
        for k, part in enumerate(pieces):
            tab_ref[d, k] = part.astype(F32)
        tab_ref[d, N_PIECES] = a - m0
        tab_ref[d, N_PIECES + 1] = jnp.exp(btot - b + ig - m1)
        tab_ref[d, N_PIECES + 2] = jnp.exp(btot + m0 - m1)


def _gate_prep(raw, bias):
    L = raw.shape[3]
    rows = raw.shape[2] // SUBLANES * LSTM_HEADS
    buf = lambda *lead: pltpu.VMEM((*lead, rows, L), F32)
    return pl.pallas_call(
        _gate_kernel,
        out_shape=jax.ShapeDtypeStruct((2, N_TAB, rows, L), F32),
        scratch_shapes=[buf(2), buf(), buf(), buf(), buf()],
        compiler_params=pltpu.CompilerParams(vmem_limit_bytes=VMEM_LIMIT),
        name="mlstm_gates",
    )(raw, bias)


def _mlstm_kernel(lq_ref, lk_ref, lv_ref, tab_ref, sel_ref,
                  cw_q_ref, cw_k_ref, cb_q_ref, cb_k_ref, o_ref, xq_s, xk_s, q_s, kt_s, ktb_s, c_s):
    seq = lq_ref.shape[0]
    L = LSTM_CHUNK
    nc = seq // L

    zpad = jnp.zeros((CONV_PAD, LANES), F32)
    for x_s, src_ref in ((xq_s, lq_ref), (xk_s, lk_ref)):
        x_s[0:CONV_PAD, :] = zpad
        x_s[CONV_PAD + seq:CONV_PAD + seq + CONV_PAD, :] = zpad
        x_s[CONV_PAD:CONV_PAD + seq, :] = src_ref[...]

    def conv_silu(x_s, w_ref, b_ref, r0):
        acc = jnp.zeros((CONV_ROWS, LANES), F32) + b_ref[0]
        for j in range(CONV_K):
            start = CONV_PAD + r0 + j - CONV_K // 2
            acc = acc + x_s[start:start + CONV_ROWS, :] * w_ref[0, j:j + 1, :]
        return jax.nn.silu(acc)

    n_blocks = seq // CONV_ROWS
    for i in range(n_blocks // 2):
        for blk in (i, n_blocks - 1 - i):
            r0 = blk * CONV_ROWS
            q_s[r0:r0 + CONV_ROWS, :] = conv_silu(xq_s, cw_q_ref, cb_q_ref, r0)
            k = conv_silu(xk_s, cw_k_ref, cb_k_ref, r0) * (HEAD_DIM ** -0.5)
            for j in range(CONV_ROWS // L):
                kt = k[j * L:(j + 1) * L, :].T
                kt_s[r0 // L + j] = kt
                ktb_s[r0 // L + j] = kt.astype(BF16)

    row_i = lax.broadcasted_iota(jnp.int32, (L, L), 0)
    col_i = lax.broadcasted_iota(jnp.int32, (L, L), 1)
    masks = (col_i <= row_i, col_i >= row_i)
    ones = jnp.ones((L, LANES), BF16)
    pad_rows = jnp.zeros((P_ROWS - N_PIECES, L), F32)
    c_s[...] = jnp.zeros(c_s.shape, F32)

    def chunk_step(d, c, first):
        rows = pl.ds(c * L, L)
        row = pl.ds(c, 1)
        vaug = jnp.concatenate([lv_ref[rows, :], ones], axis=1)
        caug = c_s[d]
        qc = q_s[rows, :]
        qk = _dot(qc.astype(BF16), ktb_s[c])
        tab = [tab_ref[d, k, row, :] for k in range(N_TAB)]
        pieces = jnp.concatenate(tab[:N_PIECES] + [pad_rows], axis=0).astype(BF16)
        e = _dot_tn(pieces, sel_ref[...])
        dmx = e[:, :LANES]
        w = jnp.where(masks[d], jnp.exp(tab[N_PIECES] + dmx), 0.0)
        s = (qk * w).astype(BF16)
        qw = (qc * jnp.exp(dmx)).astype(BF16)
        kw = (kt_s[c] * tab[N_PIECES + 1]).astype(BF16)
        upd = _dot(kw, vaug)
        r = _dot(jnp.concatenate([s, qw], axis=1), jnp.concatenate([vaug, caug.astype(BF16)], axis=0))
        h = r[:, :LANES] / jnp.maximum(jnp.abs(r[:, LANES:]), jnp.exp(e[:, LANES:]))
        o_ref[rows, :] = h if first else o_ref[rows, :] + h
        dec = tab[N_PIECES + 2]
        c_s[d] = jnp.concatenate([dec, dec], axis=1) * caug + upd

    for i in range(nc):
        chunk_step(0, i, first=i < nc - 1 - i)
        chunk_step(1, nc - 1 - i, first=nc - 1 - i > i)


def _mlstm(lq, lk, lv, tab, sel, cw, cb, layer, batch, seq):
    n = lq.shape[0]
    nh = LSTM_HEADS
    nc = seq // LSTM_CHUNK
    head_blk = lambda: pl.BlockSpec((seq, LANES), lambda b, h: (b, h))
    return pl.pallas_call(
        _mlstm_kernel,
        grid=(batch, nh),
        in_specs=[head_blk(), head_blk(), head_blk(),
                  pl.BlockSpec((2, N_TAB, nc, LSTM_CHUNK), lambda b, h: (0, 0, b * nh + h, 0)),
                  pl.BlockSpec((P_ROWS, N_COLQ * LANES), lambda b, h: (0, 0)),
                  pl.BlockSpec((None, 1, CONV_K, LANES), lambda b, h: (layer, h, 0, 0)),
                  pl.BlockSpec((None, 1, CONV_K, LANES), lambda b, h: (layer, nh + h, 0, 0)),
                  pl.BlockSpec((None, 1, 1, LANES), lambda b, h: (layer, h, 0, 0)),
                  pl.BlockSpec((None, 1, 1, LANES), lambda b, h: (layer, nh + h, 0, 0))],
        out_specs=head_blk(),
        out_shape=jax.ShapeDtypeStruct((n, BRANCH_W), F32),
        scratch_shapes=[pltpu.VMEM((seq + 2 * CONV_PAD, LANES), F32),
                        pltpu.VMEM((seq + 2 * CONV_PAD, LANES), F32),
                        pltpu.VMEM((seq, LANES), F32),
                        pltpu.VMEM((nc, HEAD_DIM, LSTM_CHUNK), F32),
                        pltpu.VMEM((nc, HEAD_DIM, LSTM_CHUNK), BF16),
                        pltpu.VMEM((2, HEAD_DIM, 2 * LANES), F32)],
        compiler_params=pltpu.CompilerParams(dimension_semantics=("arbitrary", "arbitrary"),
                                             vmem_limit_bytes=VMEM_LIMIT),
        name="mlstm",
    )(lq, lk, lv, tab, sel, cw, cw, cb, cb)


COMBINE_COL_CHUNK = 256


def _combine_kernel(x_ref, ya_ref, yb_ref, hc_ref, lo_ref, g_ref, hn_ref, wg_hbm, bg_ref, wb_hbm, wo_hbm, o_ref,
                    wg_ref, wb_ref, wo_ref, wg_stage, wb_stage, wo_stage, w_sem, *, layer):
    @pl.when(pl.program_id(0) == 0)
    def _():
        _cast_weight(wg_hbm.at[layer], wg_ref, wg_stage, w_sem.at[0])
        for i in range(N_BRANCH):
            _cast_weight(wb_hbm.at[layer, i], wb_ref.at[i], wb_stage, w_sem.at[1])
        _cast_weight(wo_hbm.at[layer], wo_ref, wo_stage, w_sem.at[2])

    x = x_ref[...]
    xn = _rms(x, g_ref[...]).astype(BF16)
    yc = jnp.concatenate(
        [_rms(hc_ref[:, h * LANES:(h + 1) * LANES], hn_ref[:, h * LANES:(h + 1) * LANES]) for h in range(LSTM_HEADS)],
        axis=1)
    yc = (yc * jax.nn.sigmoid(lo_ref[...])).astype(BF16)
    ys = (ya_ref[...], yb_ref[...], yc)
    acc = x
    for k0 in range(0, D_MODEL, COMBINE_COL_CHUNK):
        k1 = k0 + COMBINE_COL_CHUNK
        mix = None
        for i, y in enumerate(ys):
            c0 = i * D_MODEL + k0
            gate = jax.nn.sigmoid(_dot(xn, wg_ref[:, c0:c0 + COMBINE_COL_CHUNK]) + bg_ref[:, c0:c0 + COMBINE_COL_CHUNK])
            term = gate * _dot(y, wb_ref[i, :, k0:k1])
            mix = term if mix is None else mix + term
        acc = acc + _dot(mix.astype(BF16), wo_ref[k0:k1, :])
    o_ref[...] = acc


def _combine(x2, ya, yb, hc, lo, g, hn, wg, bg, wb, wo, layer):
    n = x2.shape[0]
    t = TOK_TILE
    tok = lambda w: pl.BlockSpec((t, w), lambda i: (i, 0))
    hbm = lambda: pl.BlockSpec(memory_space=pl.ANY)
    row = lambda w: pl.BlockSpec((None, 1, w), lambda i: (layer, 0, 0))
    stage = lambda k, n_: pltpu.VMEM((2, _stage_rows(k, n_), n_), F32)
    return pl.pallas_call(
        functools.partial(_combine_kernel, layer=layer),
        grid=(n // t,),
        in_specs=[tok(D_MODEL), tok(BRANCH_W), tok(BRANCH_W), tok(BRANCH_W), tok(BRANCH_W), row(D_MODEL),
                  row(BRANCH_W), hbm(), row(N_BRANCH * D_MODEL), hbm(), hbm()],
        out_specs=tok(D_MODEL),
        out_shape=jax.ShapeDtypeStruct((n, D_MODEL), F32),
        scratch_shapes=[pltpu.VMEM((D_MODEL, N_BRANCH * D_MODEL), BF16),
                        pltpu.VMEM((N_BRANCH, BRANCH_W, D_MODEL), BF16),
                        pltpu.VMEM((D_MODEL, D_MODEL), BF16),
                        stage(D_MODEL, N_BRANCH * D_MODEL), stage(BRANCH_W, D_MODEL), stage(D_MODEL, D_MODEL),
                        pltpu.SemaphoreType.DMA((3, 2))],
        compiler_params=pltpu.CompilerParams(dimension_semantics=("arbitrary",),
                                             vmem_limit_bytes=VMEM_LIMIT),
        name="combine",
    )(x2, ya, yb, hc, lo, g, hn, wg, bg, wb, wo)


FFN_COL_CHUNK = 256


def _ffn_kernel(x_ref, g_ref, wi_hbm, wo_hbm, o_ref, wi_ref, wo_ref, wi_stage, wo_stage, w_sem, *, layer):
    @pl.when(pl.program_id(0) == 0)
    def _():
        _cast_weight(wi_hbm.at[layer], wi_ref, wi_stage, w_sem.at[0])
        _cast_weight(wo_hbm.at[layer], wo_ref, wo_stage, w_sem.at[1])

    x = x_ref[...]
    xn = _rms(x, g_ref[...]).astype(BF16)
    acc = x
    for c0 in range(0, FFN_HIDDEN, FFN_COL_CHUNK):
        c1 = min(c0 + FFN_COL_CHUNK, FFN_HIDDEN)
        a = jax.nn.silu(_dot(xn, wi_ref[:, c0:c1])) * _dot(xn, wi_ref[:, FFN_HIDDEN + c0:FFN_HIDDEN + c1])
        acc = acc + _dot(a.astype(BF16), wo_ref[c0:c1, :])
    o_ref[...] = acc


def _ffn(x2, g, wi, wo, layer):
    n = x2.shape[0]
    t = TOK_TILE
    tok = lambda w: pl.BlockSpec((t, w), lambda i: (i, 0))
    hbm = lambda: pl.BlockSpec(memory_space=pl.ANY)
    stage = lambda k, n_: pltpu.VMEM((2, _stage_rows(k, n_), n_), F32)
    return pl.pallas_call(
        functools.partial(_ffn_kernel, layer=layer),
        grid=(n // t,),
        in_specs=[tok(D_MODEL), pl.BlockSpec((None, 1, D_MODEL), lambda i: (layer, 0, 0)), hbm(), hbm()],
        out_specs=tok(D_MODEL),
        out_shape=jax.ShapeDtypeStruct((n, D_MODEL), F32),
        scratch_shapes=[pltpu.VMEM((D_MODEL, 2 * FFN_HIDDEN), BF16), pltpu.VMEM((FFN_HIDDEN, D_MODEL), BF16),
                        stage(D_MODEL, 2 * FFN_HIDDEN), stage(FFN_HIDDEN, D_MODEL),
                        pltpu.SemaphoreType.DMA((2, 2))],
        compiler_params=pltpu.CompilerParams(dimension_semantics=("arbitrary",),
                                             vmem_limit_bytes=VMEM_LIMIT),
        name="ffn",
    )(x2, g, wi, wo)


def _rope_tables(seq):
    rows = seq // GRID_W
    row = jnp.repeat(jnp.arange(rows, dtype=F32), GRID_W)
    col = jnp.tile(jnp.arange(GRID_W, dtype=F32), rows)
    axis_dim = HEAD_DIM // 2
    freqs = ROPE_THETA ** (-jnp.arange(axis_dim // 2, dtype=F32) * 2.0 / axis_dim)
    ang = jnp.concatenate([row[:, None] * freqs[None], col[:, None] * freqs[None]], axis=-1)
    cos, sin = jnp.cos(ang), jnp.sin(ang)
    cos_full = jnp.repeat(cos, 2, axis=-1)
    sin_signed = jnp.stack([-sin, sin], axis=-1).reshape(seq, HEAD_DIM)
    return cos_full, sin_signed


def _sel_matrix():
    sel = np.zeros((P_ROWS, N_COLQ * LANES), np.float32)
    for q in range(N_COLQ):
        sel[q * N_SPLIT:(q + 1) * N_SPLIT, q * LANES:(q + 1) * LANES] = 1.0
    return jnp.asarray(sel, BF16)


def kernel(x, norm_mix, w_in, sgu_ln_g, sgu_ln_b, sgu_w, sgu_b, q_norm, k_norm, conv_w, conv_b, igate_b, fgate_b,
           lstm_norm, w_gate, b_gate, w_branch, w_out, norm_ffn, w_ffn_in, w_ffn_out):
    batch, seq, d = x.shape
    depth = norm_mix.shape[0]
    nh = LSTM_HEADS
    L = LSTM_CHUNK
    nc = seq // L
    cos, sin = _rope_tables(seq)
    sel = _sel_matrix()
    x2 = x.reshape(batch * seq, d)
    rows = lambda p: p[:, None, :]
    w_in_b = w_in.astype(BF16)
    w_g = jnp.swapaxes(w_in_b[:, :, N_MAIN:], 1, 2).reshape(depth, 4, nh, d)
    w_g = jnp.pad(w_g, ((0, 0), (0, 0), (0, SUBLANES - nh), (0, 0))).reshape(depth, 4 * SUBLANES, d)
    sgub = jnp.repeat(jnp.swapaxes(sgu_b, 1, 2), LANES, axis=2)
    sguw = sgu_w.astype(BF16)
    gbias = jnp.stack([igate_b, fgate_b], axis=1)
    gbias = jnp.broadcast_to(gbias[:, :, :, None, :, None], (depth, 2, 2, batch, nh, L))
    gbias = gbias.reshape(depth, 2, 2, batch * nh, L)
    cw = conv_w.reshape(depth, CONV_K, 2 * nh, LANES).transpose(0, 2, 1, 3)
    cb = conv_b.reshape(depth, 2 * nh, 1, LANES)
    for l in range(depth):
        ya, qa, ka, va, lq, lk, lv, lo, graw = _inproj(
            x2, rows(norm_mix), w_in_b, l, w_g, rows(sgu_ln_g), rows(sgu_ln_b),
            sguw, sgub, rows(q_norm), rows(k_norm), cos, sin, seq)
        yb = _attention(qa, ka, va, batch, seq)

        tab = _gate_prep(graw.reshape(2, 2, nc * batch * SUBLANES, L), gbias[l])
        hc = _mlstm(lq, lk, lv, tab, sel, cw, cb, l, batch, seq)

        x2 = _combine(x2, ya, yb, hc, lo, rows(norm_mix), rows(lstm_norm), w_gate,
                      rows(b_gate), w_branch, w_out, l)
        x2 = _ffn(x2, rows(norm_ffn), w_ffn_in, w_ffn_out, l)
    return x2.reshape(batch, seq, d)
```

```python
import functools

import jax
import jax.numpy as jnp
import numpy as np
from jax import lax
from jax.experimental import pallas as pl
from jax.experimental.pallas import tpu as pltpu

D_MODEL = 1024
GRID_W = 64
BRANCH_W = 512
N_BRANCH = 3
EPS = 1e-6
SGU_CHUNK = 128
SGU_GROUPS = 4
ATT_HEADS = 4
ATT_KV_HEADS = 2
HEAD_DIM = 128
ROPE_THETA = 10000.0
LSTM_HEADS = 4
LSTM_CHUNK = 128
CONV_K = 5
FFN_HIDDEN = 2816
N_MAIN = 4096

LANES = 128
SUBLANES = 8
VMEM_LIMIT = 56 * 1024 * 1024

TOK_TILE = 512
ATT_Q_TILE = 2048
ATT_KEY_BLOCK = 256
CONV_ROWS = 256
CONV_PAD = SUBLANES

LOG2_E = 1.4426950408889634

BF16 = jnp.bfloat16
F32 = jnp.float32


def _rms(x, g):
    return x * lax.rsqrt(jnp.mean(x * x, axis=-1, keepdims=True) + EPS) * g


def _dot(a, b):
    return jnp.dot(a, b, preferred_element_type=F32)


def _dot_nt(a, b):
    return lax.dot_general(a, b, (((1,), (1,)), ((), ())), preferred_element_type=F32)


def _dot_tn(a, b):
    return lax.dot_general(a, b, (((0,), (0,)), ((), ())), preferred_element_type=F32)


W_STAGE_BYTES = 4 * 1024 * 1024


def _stage_rows(k, n):
    rows = max(16, min(k, W_STAGE_BYTES // (4 * n)) // 16 * 16)
    while k % rows:
        rows -= 16
    return rows


def _cast_weight(w_hbm, w_vmem, stage, sem):
    slots, rows = stage.shape[0], stage.shape[1]
    chunks = w_vmem.shape[0] // rows

    def copy(c):
        return pltpu.make_async_copy(w_hbm.at[pl.ds(c * rows, rows)], stage.at[c % slots], sem.at[c % slots])

    for c in range(min(slots - 1, chunks)):
        copy(c).start()
    for c in range(chunks):
        if c + slots - 1 < chunks:
            copy(c + slots - 1).start()
        copy(c).wait()
        w_vmem[c * rows:(c + 1) * rows, :] = stage[c % slots].astype(BF16)


def _inproj_kernel(x_ref, g_ref, w_ref, wgr_ref, lng_ref, lnb_ref, sguw_ref, sgub_ref,
                   qn_ref, kn_ref, cos_ref, sin_ref,
                   ya_ref, qa_ref, ka_ref, va_ref, lq_ref, lk_ref, lv_ref, lo_ref, grow_ref):
    t = x_ref.shape[0]
    xn = _rms(x_ref[...], g_ref[...]).astype(BF16)

    def seg(lo, hi):
        return _dot(xn, w_ref[:, lo:hi])

    gv = jax.nn.gelu(seg(512, 1024))
    mu = jnp.mean(gv, axis=-1, keepdims=True)
    vc = gv - mu
    v = vc * lax.rsqrt(jnp.mean(vc * vc, axis=-1, keepdims=True) + EPS) * lng_ref[...] + lnb_ref[...]
    vb = v.astype(BF16)
    u = jax.nn.gelu(seg(0, 512))

    cos = cos_ref[...]
    sin = sin_ref[...]
    even = (lax.broadcasted_iota(jnp.int32, (t, LANES), 1) % 2) == 0

    def norm_rope(xh, gain):
        xh = _rms(xh, gain)
        partner = jnp.where(even, pltpu.roll(xh, LANES - 1, 1), pltpu.roll(xh, 1, 1))
        return (xh * cos + partner * sin).astype(BF16)

    aq = seg(1024, 1536)
    for h in range(ATT_HEADS):
        qa_ref[:, h * LANES:(h + 1) * LANES] = norm_rope(aq[:, h * LANES:(h + 1) * LANES], qn_ref[...])
    ak = seg(1536, 1792)
    for h in range(ATT_KV_HEADS):
        ka_ref[:, h * LANES:(h + 1) * LANES] = norm_rope(ak[:, h * LANES:(h + 1) * LANES], kn_ref[...])
    va_ref[...] = seg(1792, 2048).astype(BF16)

    lq_ref[...] = seg(2048, 2560)
    lk_ref[...] = seg(2560, 3072)
    lv_ref[...] = seg(3072, 3584).astype(BF16)
    lo_ref[...] = seg(3584, 4096)
    gt = _dot_nt(wgr_ref[...], xn)
    for kd in range(4):
        for j in range(t // LSTM_CHUNK):
            grow_ref[kd // 2, kd % 2, j, 0] = gt[kd * SUBLANES:(kd + 1) * SUBLANES, j * LANES:(j + 1) * LANES]

    n_chunks = t // SGU_CHUNK
    for grp in range(SGU_GROUPS):
        c0 = grp * LANES
        vcat = jnp.concatenate([vb[j * SGU_CHUNK:(j + 1) * SGU_CHUNK, c0:c0 + LANES] for j in range(n_chunks)], axis=1)
        s = _dot(sguw_ref[grp], vcat)
        for j in range(n_chunks):
            r0 = j * SGU_CHUNK
            sj = s[:, j * LANES:(j + 1) * LANES] + sgub_ref[:, c0:c0 + LANES]
            ya_ref[r0:r0 + SGU_CHUNK, c0:c0 + LANES] = (u[r0:r0 + SGU_CHUNK, c0:c0 + LANES] * sj).astype(BF16)


def _inproj(x2, g, w_in_b, layer, w_gr, lng, lnb, sguw, sgub, qn, kn, cos, sin, seq):
    n = x2.shape[0]
    t = TOK_TILE
    tiles_per_seq = seq // t
    tok = lambda w: pl.BlockSpec((t, w), lambda i: (i, 0))
    rope = pl.BlockSpec((t, LANES), lambda i: (i % tiles_per_seq, 0))
    per_layer = lambda *shape: pl.BlockSpec((None, *shape), lambda i: (layer,) + (0,) * len(shape))
    out_shapes = (
        jax.ShapeDtypeStruct((n, BRANCH_W), BF16),
        jax.ShapeDtypeStruct((n, 512), BF16),
        jax.ShapeDtypeStruct((n, 256), BF16),
        jax.ShapeDtypeStruct((n, 256), BF16),
        jax.ShapeDtypeStruct((n, BRANCH_W), F32),
        jax.ShapeDtypeStruct((n, BRANCH_W), F32),
        jax.ShapeDtypeStruct((n, BRANCH_W), BF16),
        jax.ShapeDtypeStruct((n, BRANCH_W), F32),
        jax.ShapeDtypeStruct((2, 2, seq // LSTM_CHUNK, n // seq, SUBLANES, LSTM_CHUNK), F32),
    )
    out_specs = (tok(512), tok(512), tok(256), tok(256), tok(512), tok(512), tok(512), tok(512),
                 pl.BlockSpec((2, 2, t // LSTM_CHUNK, 1, SUBLANES, LSTM_CHUNK),
                              lambda i: (0, 0, i % tiles_per_seq, i // tiles_per_seq, 0, 0)))
    return pl.pallas_call(
        _inproj_kernel,
        grid=(n // t,),
        in_specs=[tok(D_MODEL), per_layer(1, D_MODEL),
                  per_layer(D_MODEL, N_MAIN),
                  per_layer(4 * SUBLANES, D_MODEL), per_layer(1, BRANCH_W), per_layer(1, BRANCH_W),
                  per_layer(SGU_GROUPS, SGU_CHUNK, SGU_CHUNK), per_layer(SGU_CHUNK, BRANCH_W),
                  per_layer(1, LANES), per_layer(1, LANES), rope, rope],
        out_specs=out_specs,
        out_shape=out_shapes,
        compiler_params=pltpu.CompilerParams(dimension_semantics=("arbitrary",),
                                             vmem_limit_bytes=VMEM_LIMIT),
        name="inproj",
    )(x2, g, w_in_b, w_gr, lng, lnb, sguw, sgub, qn, kn, cos, sin)


def _attn_kernel(q_ref, k_ref, v_ref, o_ref):
    exp2_scale = (HEAD_DIM ** -0.5) * LOG2_E
    rep = ATT_HEADS // ATT_KV_HEADS
    tq = q_ref.shape[0]
    kb = ATT_KEY_BLOCK
    ones = jnp.ones((kb, LANES), BF16)
    qs, ms, accs = [], [], []
    for grp in range(ATT_KV_HEADS):
        c0 = grp * rep * LANES
        qs.append(jnp.concatenate([q_ref[:, c0 + r * LANES:c0 + (r + 1) * LANES] for r in range(rep)], axis=0))
    for j in range(k_ref.shape[0] // kb):
        for grp in range(ATT_KV_HEADS):
            kblk = k_ref[j * kb:(j + 1) * kb, grp * LANES:(grp + 1) * LANES]
            vaug = jnp.concatenate([v_ref[j * kb:(j + 1) * kb, grp * LANES:(grp + 1) * LANES], ones], axis=1)
            s = _dot_nt(qs[grp], kblk)
            bmax = jnp.max(s, axis=-1, keepdims=True)
            if j == 0:
                ms.append(bmax)
                accs.append(_dot(jnp.exp2((s - bmax) * exp2_scale).astype(BF16), vaug))
            else:
                m_new = jnp.maximum(ms[grp], bmax)
                alpha = jnp.exp2((ms[grp] - m_new) * exp2_scale)
                p = jnp.exp2((s - m_new) * exp2_scale).astype(BF16)
                accs[grp] = alpha * accs[grp] + _dot(p, vaug)
                ms[grp] = m_new
    for grp in range(ATT_KV_HEADS):
        c0 = grp * rep * LANES
        o = accs[grp][:, :LANES] / accs[grp][:, LANES:]
        for r in range(rep):
            o_ref[:, c0 + r * LANES:c0 + (r + 1) * LANES] = o[r * tq:(r + 1) * tq].astype(BF16)


def _attention(qa, ka, va, batch, seq):
    n = qa.shape[0]
    tq = ATT_Q_TILE
    qpb = seq // tq
    return pl.pallas_call(
        _attn_kernel,
        grid=(batch, qpb),
        in_specs=[pl.BlockSpec((tq, 512), lambda b, i: (b * qpb + i, 0)),
                  pl.BlockSpec((seq, 256), lambda b, i: (b, 0)),
                  pl.BlockSpec((seq, 256), lambda b, i: (b, 0))],
        out_specs=pl.BlockSpec((tq, 512), lambda b, i: (b * qpb + i, 0)),
        out_shape=jax.ShapeDtypeStruct((n, BRANCH_W), BF16),
        compiler_params=pltpu.CompilerParams(dimension_semantics=("arbitrary", "arbitrary"),
                                             vmem_limit_bytes=VMEM_LIMIT),
        name="gqa",
    )(qa, ka, va)


def _log_sigmoid(x):
    return jnp.minimum(x, 0.0) - jnp.log1p(jnp.exp(-jnp.abs(x)))


def _lane_scan(x, op, identity, reverse):
    lane = lax.broadcasted_iota(jnp.int32, x.shape, 1)
    k = 1
    while k < LANES:
        if reverse:
            x = op(x, jnp.where(lane < LANES - k, pltpu.roll(x, LANES - k, 1), identity))
        else:
            x = op(x, jnp.where(lane >= k, pltpu.roll(x, k, 1), identity))
        k *= 2
    return x


N_SPLIT = 3
N_COLQ = 2
N_PIECES = N_COLQ * N_SPLIT
N_TAB = N_PIECES + 3
P_ROWS = 16


def _split3(x):
    hi = x.astype(BF16)
    r = x - hi.astype(F32)
    mid = r.astype(BF16)
    lo = (r - mid.astype(F32)).astype(BF16)
    return [hi, mid, lo]


def _gate_kernel(raw_ref, bias_ref, tab_ref, rm_s, bt_s, gm_s, m0_s, m1_s):
    nr = bias_ref.shape[2]
    slots = nr // LSTM_HEADS * SUBLANES
    nc = raw_ref.shape[2] // slots
    L = LSTM_CHUNK
    for d in range(2):
        rev = d == 1
        last = 0 if rev else L - 1
        for k in range(2):
            for r in range(nr):
                slot = r // LSTM_HEADS * SUBLANES + r % LSTM_HEADS
                rm_s[k, r * nc:(r + 1) * nc, :] = (raw_ref[k, d, pl.ds(slot, nc, stride=slots), :]
                                                   + bias_ref[k, d, r:r + 1, :])
        ig = rm_s[0]
        lf = _log_sigmoid(rm_s[1])
        b = _lane_scan(lf, jnp.add, 0.0, rev)
        a = ig - b
        g = _lane_scan(a, jnp.maximum, -jnp.inf, rev)
        bt_s[...] = jnp.broadcast_to(b[:, last:last + 1], b.shape)
        gm_s[...] = jnp.broadcast_to(g[:, last:last + 1], g.shape)
        m = jnp.zeros((nr, L), F32)
        for c in (range(nc - 1, -1, -1) if rev else range(nc)):
            rows = pl.ds(c, nr, stride=nc)
            m0_s[rows, :] = m
            m = bt_s[rows, :] + jnp.maximum(m, gm_s[rows, :])
            m1_s[rows, :] = m
        m0 = m0_s[...]
        m1 = m1_s[...]
        btot = bt_s[...]
        mx = jnp.maximum(m0, g)
        pieces = _split3(m0 - mx) + _split3(-(b + mx))
        for k, part in enumerate(pieces):
            tab_ref[d, k] = part.astype(F32)
        tab_ref[d, N_PIECES] = a - m0
        tab_ref[d, N_PIECES + 1] = jnp.exp(btot - b + ig - m1)
        tab_ref[d, N_PIECES + 2] = jnp.exp(btot + m0 - m1)


def _gate_prep(raw, bias):
    L = raw.shape[3]
    rows = raw.shape[2] // SUBLANES * LSTM_HEADS
    buf = lambda *lead: pltpu.VMEM((*lead, rows, L), F32)
    return pl.pallas_call(
        _gate_kernel,
        out_shape=jax.ShapeDtypeStruct((2, N_TAB, rows, L), F32),
        scratch_shapes=[buf(2), buf(), buf(), buf(), buf()],
        compiler_params=pltpu.CompilerParams(vmem_limit_bytes=VMEM_LIMIT),
        name="mlstm_gates",
    )(raw, bias)


def _mlstm_kernel(lq_ref, lk_ref, lv_ref, tab_ref, sel_ref,
                  cw_q_ref, cw_k_ref, cb_q_ref, cb_k_ref, o_ref, xq_s, xk_s, q_s, kt_s, ktb_s, c_s):
    seq = lq_ref.shape[0]
    L = LSTM_CHUNK
    nc = seq // L

    zpad = jnp.zeros((CONV_PAD, LANES), F32)
    for x_s, src_ref in ((xq_s, lq_ref), (xk_s, lk_ref)):
        x_s[0:CONV_PAD, :] = zpad
        x_s[CONV_PAD + seq:CONV_PAD + seq + CONV_PAD, :] = zpad
        x_s[CONV_PAD:CONV_PAD + seq, :] = src_ref[...]

    def conv_silu(x_s, w_ref, b_ref, r0):
        acc = jnp.zeros((CONV_ROWS, LANES), F32) + b_ref[0]
        for j in range(CONV_K):
            start = CONV_PAD + r0 + j - CONV_K // 2
            acc = acc + x_s[start:start + CONV_ROWS, :] * w_ref[0, j:j + 1, :]
        return jax.nn.silu(acc)

    n_blocks = seq // CONV_ROWS
    for i in range(n_blocks // 2):
        for blk in (i, n_blocks - 1 - i):
            r0 = blk * CONV_ROWS
            q_s[r0:r0 + CONV_ROWS, :] = conv_silu(xq_s, cw_q_ref, cb_q_ref, r0)
            k = conv_silu(xk_s, cw_k_ref, cb_k_ref, r0) * (HEAD_DIM ** -0.5)
            for j in range(CONV_ROWS // L):
                kt = k[j * L:(j + 1) * L, :].T
                kt_s[r0 // L + j] = kt
                ktb_s[r0 // L + j] = kt.astype(BF16)

    row_i = lax.broadcasted_iota(jnp.int32, (L, L), 0)
    col_i = lax.broadcasted_iota(jnp.int32, (L, L), 1)
    masks = (col_i <= row_i, col_i >= row_i)
    ones = jnp.ones((L, LANES), BF16)
    pad_rows = jnp.zeros((P_ROWS - N_PIECES, L), F32)
    c_s[...] = jnp.zeros(c_s.shape, F32)

    def chunk_step(d, c, first):
        rows = pl.ds(c * L, L)
        row = pl.ds(c, 1)
        vaug = jnp.concatenate([lv_ref[rows, :], ones], axis=1)
        caug = c_s[d]
        qc = q_s[rows, :]
        qk = _dot(qc.astype(BF16), ktb_s[c])
        tab = [tab_ref[d, k, row, :] for k in range(N_TAB)]
        pieces = jnp.concatenate(tab[:N_PIECES] + [pad_rows], axis=0).astype(BF16)
        e = _dot_tn(pieces, sel_ref[...])
        dmx = e[:, :LANES]
        w = jnp.where(masks[d], jnp.exp(tab[N_PIECES] + dmx), 0.0)
        s = (qk * w).astype(BF16)
        qw = (qc * jnp.exp(dmx)).astype(BF16)
        kw = (kt_s[c] * tab[N_PIECES + 1]).astype(BF16)
        upd = _dot(kw, vaug)
        r = _dot(jnp.concatenate([s, qw], axis=1), jnp.concatenate([vaug, caug.astype(BF16)], axis=0))
        h = r[:, :LANES] / jnp.maximum(jnp.abs(r[:, LANES:]), jnp.exp(e[:, LANES:]))
        o_ref[rows, :] = h if first else o_ref[rows, :] + h
        dec = tab[N_PIECES + 2]
        c_s[d] = jnp.concatenate([dec, dec], axis=1) * caug + upd

    for i in range(nc):
        chunk_step(0, i, first=i < nc - 1 - i)
        chunk_step(1, nc - 1 - i, first=nc - 1 - i > i)


def _mlstm(lq, lk, lv, tab, sel, cw, cb, layer, batch, seq):
    n = lq.shape[0]
    nh = LSTM_HEADS
    nc = seq // LSTM_CHUNK
    head_blk = lambda: pl.BlockSpec((seq, LANES), lambda b, h: (b, h))
    return pl.pallas_call(
        _mlstm_kernel,
        grid=(batch, nh),
        in_specs=[head_blk(), head_blk(), head_blk(),
                  pl.BlockSpec((2, N_TAB, nc, LSTM_CHUNK), lambda b, h: (0, 0, b * nh + h, 0)),
                  pl.BlockSpec((P_ROWS, N_COLQ * LANES), lambda b, h: (0, 0)),
                  pl.BlockSpec((None, 1, CONV_K, LANES), lambda b, h: (layer, h, 0, 0)),
                  pl.BlockSpec((None, 1, CONV_K, LANES), lambda b, h: (layer, nh + h, 0, 0)),
                  pl.BlockSpec((None, 1, 1, LANES), lambda b, h: (layer, h, 0, 0)),
                  pl.BlockSpec((None, 1, 1, LANES), lambda b, h: (layer, nh + h, 0, 0))],
        out_specs=head_blk(),
        out_shape=jax.ShapeDtypeStruct((n, BRANCH_W), F32),
        scratch_shapes=[pltpu.VMEM((seq + 2 * CONV_PAD, LANES), F32),
                        pltpu.VMEM((seq + 2 * CONV_PAD, LANES), F32),
                        pltpu.VMEM((seq, LANES), F32),
                        pltpu.VMEM((nc, HEAD_DIM, LSTM_CHUNK), F32),
                        pltpu.VMEM((nc, HEAD_DIM, LSTM_CHUNK), BF16),
                        pltpu.VMEM((2, HEAD_DIM, 2 * LANES), F32)],
        compiler_params=pltpu.CompilerParams(dimension_semantics=("arbitrary", "arbitrary"),
                                             vmem_limit_bytes=VMEM_LIMIT),
        name="mlstm",
    )(lq, lk, lv, tab, sel, cw, cw, cb, cb)


COMBINE_COL_CHUNK = 256


def _combine_kernel(x_ref, ya_ref, yb_ref, hc_ref, lo_ref, g_ref, hn_ref, wg_hbm, bg_ref, wb_hbm, wo_hbm, o_ref,
                    wg_ref, wb_ref, wo_ref, wg_stage, wb_stage, wo_stage, w_sem, *, layer):
    @pl.when(pl.program_id(0) == 0)
    def _():
        _cast_weight(wg_hbm.at[layer], wg_ref, wg_stage, w_sem.at[0])
        for i in range(N_BRANCH):
            _cast_weight(wb_hbm.at[layer, i], wb_ref.at[i], wb_stage, w_sem.at[1])
        _cast_weight(wo_hbm.at[layer], wo_ref, wo_stage, w_sem.at[2])

    x = x_ref[...]
    xn = _rms(x, g_ref[...]).astype(BF16)
    yc = jnp.concatenate(
        [_rms(hc_ref[:, h * LANES:(h + 1) * LANES], hn_ref[:, h * LANES:(h + 1) * LANES]) for h in range(LSTM_HEADS)],
        axis=1)
    yc = (yc * jax.nn.sigmoid(lo_ref[...])).astype(BF16)
    ys = (ya_ref[...], yb_ref[...], yc)
    acc = x
    for k0 in range(0, D_MODEL, COMBINE_COL_CHUNK):
        k1 = k0 + COMBINE_COL_CHUNK
        mix = None
        for i, y in enumerate(ys):
            c0 = i * D_MODEL + k0
            gate = jax.nn.sigmoid(_dot(xn, wg_ref[:, c0:c0 + COMBINE_COL_CHUNK]) + bg_ref[:, c0:c0 + COMBINE_COL_CHUNK])
            term = gate * _dot(y, wb_ref[i, :, k0:k1])
            mix = term if mix is None else mix + term
        acc = acc + _dot(mix.astype(BF16), wo_ref[k0:k1, :])
    o_ref[...] = acc


def _combine(x2, ya, yb, hc, lo, g, hn, wg, bg, wb, wo, layer):
    n = x2.shape[0]
    t = TOK_TILE
    tok = lambda w: pl.BlockSpec((t, w), lambda i: (i, 0))
    hbm = lambda: pl.BlockSpec(memory_space=pl.ANY)
    row = lambda w: pl.BlockSpec((None, 1, w), lambda i: (layer, 0, 0))
    stage = lambda k, n_: pltpu.VMEM((2, _stage_rows(k, n_), n_), F32)
    return pl.pallas_call(
        functools.partial(_combine_kernel, layer=layer),
        grid=(n // t,),
        in_specs=[tok(D_MODEL), tok(BRANCH_W), tok(BRANCH_W), tok(BRANCH_W), tok(BRANCH_W), row(D_MODEL),
                  row(BRANCH_W), hbm(), row(N_BRANCH * D_MODEL), hbm(), hbm()],
        out_specs=tok(D_MODEL),
        out_shape=jax.ShapeDtypeStruct((n, D_MODEL), F32),
        scratch_shapes=[pltpu.VMEM((D_MODEL, N_BRANCH * D_MODEL), BF16),
                        pltpu.VMEM((N_BRANCH, BRANCH_W, D_MODEL), BF16),
                        pltpu.VMEM((D_MODEL, D_MODEL), BF16),
                        stage(D_MODEL, N_BRANCH * D_MODEL), stage(BRANCH_W, D_MODEL), stage(D_MODEL, D_MODEL),
                        pltpu.SemaphoreType.DMA((3, 2))],
        compiler_params=pltpu.CompilerParams(dimension_semantics=("arbitrary",),
                                             vmem_limit_bytes=VMEM_LIMIT),
        name="combine",
    )(x2, ya, yb, hc, lo, g, hn, wg, bg, wb, wo)


FFN_COL_CHUNK = 256
FFN_STAGE_SLOTS = 3


def _ffn_kernel(x_ref, g_ref, wi_hbm, wo_hbm, o_ref, wi_ref, wo_ref, wi_stage, wo_stage, w_sem, *, layer):
    @pl.when(pl.program_id(0) == 0)
    def _():
        _cast_weight(wi_hbm.at[layer], wi_ref, wi_stage, w_sem.at[0])
        _cast_weight(wo_hbm.at[layer], wo_ref, wo_stage, w_sem.at[1])

    x = x_ref[...]
    xn = _rms(x, g_ref[...]).astype(BF16)
    acc = x
    for c0 in range(0, FFN_HIDDEN, FFN_COL_CHUNK):
        c1 = min(c0 + FFN_COL_CHUNK, FFN_HIDDEN)
        a = jax.nn.silu(_dot(xn, wi_ref[:, c0:c1])) * _dot(xn, wi_ref[:, FFN_HIDDEN + c0:FFN_HIDDEN + c1])
        acc = acc + _dot(a.astype(BF16), wo_ref[c0:c1, :])
    o_ref[...] = acc


def _ffn(x2, g, wi, wo, layer):
    n = x2.shape[0]
    t = TOK_TILE
    tok = lambda w: pl.BlockSpec((t, w), lambda i: (i, 0))
    hbm = lambda: pl.BlockSpec(memory_space=pl.ANY)
    stage = lambda k, n_: pltpu.VMEM((FFN_STAGE_SLOTS, _stage_rows(k, n_), n_), F32)
    return pl.pallas_call(
        functools.partial(_ffn_kernel, layer=layer),
        grid=(n // t,),
        in_specs=[tok(D_MODEL), pl.BlockSpec((None, 1, D_MODEL), lambda i: (layer, 0, 0)), hbm(), hbm()],
        out_specs=tok(D_MODEL),
        out_shape=jax.ShapeDtypeStruct((n, D_MODEL), F32),
        scratch_shapes=[pltpu.VMEM((D_MODEL, 2 * FFN_HIDDEN), BF16), pltpu.VMEM((FFN_HIDDEN, D_MODEL), BF16),
                        stage(D_MODEL, 2 * FFN_HIDDEN), stage(FFN_HIDDEN, D_MODEL),
                        pltpu.SemaphoreType.DMA((2, FFN_STAGE_SLOTS))],
        compiler_params=pltpu.CompilerParams(dimension_semantics=("arbitrary",),
                                             vmem_limit_bytes=VMEM_LIMIT),
        name="ffn",
    )(x2, g, wi, wo)


def _rope_tables(seq):
    rows = seq // GRID_W
    row = jnp.repeat(jnp.arange(rows, dtype=F32), GRID_W)
    col = jnp.tile(jnp.arange(GRID_W, dtype=F32), rows)
    axis_dim = HEAD_DIM // 2
    freqs = ROPE_THETA ** (-jnp.arange(axis_dim // 2, dtype=F32) * 2.0 / axis_dim)
    ang = jnp.concatenate([row[:, None] * freqs[None], col[:, None] * freqs[None]], axis=-1)
    cos, sin = jnp.cos(ang), jnp.sin(ang)
    cos_full = jnp.repeat(cos, 2, axis=-1)
    sin_signed = jnp.stack([-sin, sin], axis=-1).reshape(seq, HEAD_DIM)
    return cos_full, sin_signed


def _sel_matrix():
    sel = np.zeros((P_ROWS, N_COLQ * LANES), np.float32)
    for q in range(N_COLQ):
        sel[q * N_SPLIT:(q + 1) * N_SPLIT, q * LANES:(q + 1) * LANES] = 1.0
    return jnp.asarray(sel, BF16)


def kernel(x, norm_mix, w_in, sgu_ln_g, sgu_ln_b, sgu_w, sgu_b, q_norm, k_norm, conv_w, conv_b, igate_b, fgate_b,
           lstm_norm, w_gate, b_gate, w_branch, w_out, norm_ffn, w_ffn_in, w_ffn_out):
    batch, seq, d = x.shape
    depth = norm_mix.shape[0]
    nh = LSTM_HEADS
    L = LSTM_CHUNK
    nc = seq // L
    cos, sin = _rope_tables(seq)
    sel = _sel_matrix()
    x2 = x.reshape(batch * seq, d)
    rows = lambda p: p[:, None, :]
    w_in_b = w_in.astype(BF16)
    w_g = jnp.swapaxes(w_in_b[:, :, N_MAIN:], 1, 2).reshape(depth, 4, nh, d)
    w_g = jnp.pad(w_g, ((0, 0), (0, 0), (0, SUBLANES - nh), (0, 0))).reshape(depth, 4 * SUBLANES, d)
    sgub = jnp.repeat(jnp.swapaxes(sgu_b, 1, 2), LANES, axis=2)
    sguw = sgu_w.astype(BF16)
    gbias = jnp.stack([igate_b, fgate_b], axis=1)
    gbias = jnp.broadcast_to(gbias[:, :, :, None, :, None], (depth, 2, 2, batch, nh, L))
    gbias = gbias.reshape(depth, 2, 2, batch * nh, L)
    cw = conv_w.reshape(depth, CONV_K, 2 * nh, LANES).transpose(0, 2, 1, 3)
    cb = conv_b.reshape(depth, 2 * nh, 1, LANES)
    for l in range(depth):
        ya, qa, ka, va, lq, lk, lv, lo, graw = _inproj(
            x2, rows(norm_mix), w_in_b, l, w_g, rows(sgu_ln_g), rows(sgu_ln_b),
            sguw, sgub, rows(q_norm), rows(k_norm), cos, sin, seq)
        yb = _attention(qa, ka, va, batch, seq)

        tab = _gate_prep(graw.reshape(2, 2, nc * batch * SUBLANES, L), gbias[l])
        hc = _mlstm(lq, lk, lv, tab, sel, cw, cb, l, batch, seq)

        x2 = _combine(x2, ya, yb, hc, lo, rows(norm_mix), rows(lstm_norm), w_gate,
                      rows(b_gate), w_branch, w_out, l)
        x2 = _ffn(x2, rows(norm_ffn), w_ffn_in, w_ffn_out, l)
    return x2.reshape(batch, seq, d)
```

```python
import functools

import jax
import jax.numpy as jnp
import numpy as np
from jax import lax
from jax.experimental import pallas as pl
from jax.experimental.pallas import tpu as pltpu

D_MODEL = 1024
GRID_W = 64
BRANCH_W = 512
N_BRANCH = 3
EPS = 1e-6
SGU_CHUNK = 128
SGU_GROUPS = 4
ATT_HEADS = 4
ATT_KV_HEADS = 2
HEAD_DIM = 128
ROPE_THETA = 10000.0
LSTM_HEADS = 4
LSTM_CHUNK = 128
CONV_K = 5
FFN_HIDDEN = 2816
N_MAIN = 4096

LANES = 128
SUBLANES = 8
VMEM_LIMIT = 56 * 1024 * 1024

TOK_TILE = 512
ATT_Q_TILE = 2048
ATT_KEY_BLOCK = 256
CONV_ROWS = 256
CONV_PAD = SUBLANES

LOG2_E = 1.4426950408889634

BF16 = jnp.bfloat16
F32 = jnp.float32


def _rms(x, g):
    return x * lax.rsqrt(jnp.mean(x * x, axis=-1, keepdims=True) + EPS) * g


def _dot(a, b):
    return jnp.dot(a, b, preferred_element_type=F32)


def _dot_nt(a, b):
    return lax.dot_general(a, b, (((1,), (1,)), ((), ())), preferred_element_type=F32)


def _dot_tn(a, b):
    return lax.dot_general(a, b, (((0,), (0,)), ((), ())), preferred_element_type=F32)


W_STAGE_BYTES = 4 * 1024 * 1024


def _stage_rows(k, n):
    rows = max(16, min(k, W_STAGE_BYTES // (4 * n)) // 16 * 16)
    while k % rows:
        rows -= 16
    return rows


def _cast_weight(w_hbm, w_vmem, stage, sem):
    slots, rows = stage.shape[0], stage.shape[1]
    chunks = w_vmem.shape[0] // rows

    def copy(c):
        return pltpu.make_async_copy(w_hbm.at[pl.ds(c * rows, rows)], stage.at[c % slots], sem.at[c % slots])

    for c in range(min(slots - 1, chunks)):
        copy(c).start()
    for c in range(chunks):
        if c + slots - 1 < chunks:
            copy(c + slots - 1).start()
        copy(c).wait()
        w_vmem[c * rows:(c + 1) * rows, :] = stage[c % slots].astype(BF16)


def _inproj_kernel(x_ref, g_ref, w_ref, wgr_ref, lng_ref, lnb_ref, sguw_ref, sgub_ref,
                   qn_ref, kn_ref, cos_ref, sin_ref,
                   ya_ref, qa_ref, ka_ref, va_ref, lq_ref, lk_ref, lv_ref, lo_ref, grow_ref):
    t = x_ref.shape[0]
    xn = _rms(x_ref[...], g_ref[...]).astype(BF16)

    def seg(lo, hi):
        return _dot(xn, w_ref[:, lo:hi])

    gv = jax.nn.gelu(seg(512, 1024))
    mu = jnp.mean(gv, axis=-1, keepdims=True)
    vc = gv - mu
    v = vc * lax.rsqrt(jnp.mean(vc * vc, axis=-1, keepdims=True) + EPS) * lng_ref[...] + lnb_ref[...]
    vb = v.astype(BF16)
    u = jax.nn.gelu(seg(0, 512))

    cos = cos_ref[...]
    sin = sin_ref[...]
    even = (lax.broadcasted_iota(jnp.int32, (t, LANES), 1) % 2) == 0

    def norm_rope(xh, gain):
        xh = _rms(xh, gain)
        partner = jnp.where(even, pltpu.roll(xh, LANES - 1, 1), pltpu.roll(xh, 1, 1))
        return (xh * cos + partner * sin).astype(BF16)

    aq = seg(1024, 1536)
    for h in range(ATT_HEADS):
        qa_ref[:, h * LANES:(h + 1) * LANES] = norm_rope(aq[:, h * LANES:(h + 1) * LANES], qn_ref[...])
    ak = seg(1536, 1792)
    for h in range(ATT_KV_HEADS):
        ka_ref[:, h * LANES:(h + 1) * LANES] = norm_rope(ak[:, h * LANES:(h + 1) * LANES], kn_ref[...])
    va_ref[...] = seg(1792, 2048).astype(BF16)

    lq_ref[...] = seg(2048, 2560)
    lk_ref[...] = seg(2560, 3072)
    lv_ref[...] = seg(3072, 3584).astype(BF16)
    lo_ref[...] = seg(3584, 4096)
    gt = _dot_nt(wgr_ref[...], xn)
    for kd in range(4):
        for j in range(t // LSTM_CHUNK):
            grow_ref[kd // 2, kd % 2, j, 0] = gt[kd * SUBLANES:(kd + 1) * SUBLANES, j * LANES:(j + 1) * LANES]

    n_chunks = t // SGU_CHUNK
    for grp in range(SGU_GROUPS):
        c0 = grp * LANES
        vcat = jnp.concatenate([vb[j * SGU_CHUNK:(j + 1) * SGU_CHUNK, c0:c0 + LANES] for j in range(n_chunks)], axis=1)
        s = _dot(sguw_ref[grp], vcat)
        for j in range(n_chunks):
            r0 = j * SGU_CHUNK
            sj = s[:, j * LANES:(j + 1) * LANES] + sgub_ref[:, c0:c0 + LANES]
            ya_ref[r0:r0 + SGU_CHUNK, c0:c0 + LANES] = (u[r0:r0 + SGU_CHUNK, c0:c0 + LANES] * sj).astype(BF16)


def _inproj(x2, g, w_in_b, layer, w_gr, lng, lnb, sguw, sgub, qn, kn, cos, sin, seq):
    n = x2.shape[0]
    t = TOK_TILE
    tiles_per_seq = seq // t
    tok = lambda w: pl.BlockSpec((t, w), lambda i: (i, 0))
    rope = pl.BlockSpec((t, LANES), lambda i: (i % tiles_per_seq, 0))
    per_layer = lambda *shape: pl.BlockSpec((None, *shape), lambda i: (layer,) + (0,) * len(shape))
    out_shapes = (
        jax.ShapeDtypeStruct((n, BRANCH_W), BF16),
        jax.ShapeDtypeStruct((n, 512), BF16),
        jax.ShapeDtypeStruct((n, 256), BF16),
        jax.ShapeDtypeStruct((n, 256), BF16),
        jax.ShapeDtypeStruct((n, BRANCH_W), F32),
        jax.ShapeDtypeStruct((n, BRANCH_W), F32),
        jax.ShapeDtypeStruct((n, BRANCH_W), BF16),
        jax.ShapeDtypeStruct((n, BRANCH_W), F32),
        jax.ShapeDtypeStruct((2, 2, seq // LSTM_CHUNK, n // seq, SUBLANES, LSTM_CHUNK), F32),
    )
    out_specs = (tok(512), tok(512), tok(256), tok(256), tok(512), tok(512), tok(512), tok(512),
                 pl.BlockSpec((2, 2, t // LSTM_CHUNK, 1, SUBLANES, LSTM_CHUNK),
                              lambda i: (0, 0, i % tiles_per_seq, i // tiles_per_seq, 0, 0)))
    return pl.pallas_call(
        _inproj_kernel,
        grid=(n // t,),
        in_specs=[tok(D_MODEL), per_layer(1, D_MODEL),
                  per_layer(D_MODEL, N_MAIN),
                  per_layer(4 * SUBLANES, D_MODEL), per_layer(1, BRANCH_W), per_layer(1, BRANCH_W),
                  per_layer(SGU_GROUPS, SGU_CHUNK, SGU_CHUNK), per_layer(SGU_CHUNK, BRANCH_W),
                  per_layer(1, LANES), per_layer(1, LANES), rope, rope],
        out_specs=out_specs,
        out_shape=out_shapes,
        compiler_params=pltpu.CompilerParams(dimension_semantics=("arbitrary",),
                                             vmem_limit_bytes=VMEM_LIMIT),
        name="inproj",
    )(x2, g, w_in_b, w_gr, lng, lnb, sguw, sgub, qn, kn, cos, sin)


def _attn_kernel(q_ref, k_ref, v_ref, o_ref):
    exp2_scale = (HEAD_DIM ** -0.5) * LOG2_E
    rep = ATT_HEADS // ATT_KV_HEADS
    tq = q_ref.shape[0]
    kb = ATT_KEY_BLOCK
    ones = jnp.ones((kb, LANES), BF16)
    qs, ms, accs = [], [], []
    for grp in range(ATT_KV_HEADS):
        c0 = grp * rep * LANES
        qs.append(jnp.concatenate([q_ref[:, c0 + r * LANES:c0 + (r + 1) * LANES] for r in range(rep)], axis=0))
    for j in range(k_ref.shape[0] // kb):
        for grp in range(ATT_KV_HEADS):
            kblk = k_ref[j * kb:(j + 1) * kb, grp * LANES:(grp + 1) * LANES]
            vaug = jnp.concatenate([v_ref[j * kb:(j + 1) * kb, grp * LANES:(grp + 1) * LANES], ones], axis=1)
            s = _dot_nt(qs[grp], kblk)
            bmax = jnp.max(s, axis=-1, keepdims=True)
            if j == 0:
                ms.append(bmax)
                accs.append(_dot(jnp.exp2((s - bmax) * exp2_scale).astype(BF16), vaug))
            else:
                m_new = jnp.maximum(ms[grp], bmax)
                alpha = jnp.exp2((ms[grp] - m_new) * exp2_scale)
                p = jnp.exp2((s - m_new) * exp2_scale).astype(BF16)
                accs[grp] = alpha * accs[grp] + _dot(p, vaug)
                ms[grp] = m_new
    for grp in range(ATT_KV_HEADS):
        c0 = grp * rep * LANES
        o = accs[grp][:, :LANES] / accs[grp][:, LANES:]
        for r in range(rep):
            o_ref[:, c0 + r * LANES:c0 + (r + 1) * LANES] = o[r * tq:(r + 1) * tq].astype(BF16)


def _attention(qa, ka, va, batch, seq):
    n = qa.shape[0]
    tq = ATT_Q_TILE
    qpb = seq // tq
    return pl.pallas_call(
        _attn_kernel,
        grid=(batch, qpb),
        in_specs=[pl.BlockSpec((tq, 512), lambda b, i: (b * qpb + i, 0)),
                  pl.BlockSpec((seq, 256), lambda b, i: (b, 0)),
                  pl.BlockSpec((seq, 256), lambda b, i: (b, 0))],
        out_specs=pl.BlockSpec((tq, 512), lambda b, i: (b * qpb + i, 0)),
        out_shape=jax.ShapeDtypeStruct((n, BRANCH_W), BF16),
        compiler_params=pltpu.CompilerParams(dimension_semantics=("arbitrary", "arbitrary"),
                                             vmem_limit_bytes=VMEM_LIMIT),
        name="gqa",
    )(qa, ka, va)


def _log_sigmoid(x):
    return jnp.minimum(x, 0.0) - jnp.log1p(jnp.exp(-jnp.abs(x)))


def _lane_scan(x, op, identity, reverse):
    lane = lax.broadcasted_iota(jnp.int32, x.shape, 1)
    k = 1
    while k < LANES:
        if reverse:
            x = op(x, jnp.where(lane < LANES - k, pltpu.roll(x, LANES - k, 1), identity))
        else:
            x = op(x, jnp.where(lane >= k, pltpu.roll(x, k, 1), identity))
        k *= 2
    return x


N_SPLIT = 3
N_COLQ = 2
N_PIECES = N_COLQ * N_SPLIT
N_TAB = N_PIECES + 3
P_ROWS = 16


def _split3(x):
    hi = x.astype(BF16)
    r = x - hi.astype(F32)
    mid = r.astype(BF16)
    lo = (r - mid.astype(F32)).astype(BF16)
    return [hi, mid, lo]


def _gate_kernel(raw_ref, bias_ref, tab_ref, rm_s, bt_s, gm_s, m0_s, m1_s):
    nr = bias_ref.shape[2]
    slots = nr // LSTM_HEADS * SUBLANES
    nc = raw_ref.shape[2] // slots
    L = LSTM_CHUNK
    for d in range(2):
        rev = d == 1
        last = 0 if rev else L - 1
        for k in range(2):
            for r in range(nr):
                slot = r // LSTM_HEADS * SUBLANES + r % LSTM_HEADS
                rm_s[k, r * nc:(r + 1) * nc, :] = (raw_ref[k, d, pl.ds(slot, nc, stride=slots), :]
                                                   + bias_ref[k, d, r:r + 1, :])
        ig = rm_s[0]
        lf = _log_sigmoid(rm_s[1])
        b = _lane_scan(lf, jnp.add, 0.0, rev)
        a = ig - b
        g = _lane_scan(a, jnp.maximum, -jnp.inf, rev)
        bt_s[...] = jnp.broadcast_to(b[:, last:last + 1], b.shape)
        gm_s[...] = jnp.broadcast_to(g[:, last:last + 1], g.shape)
        m = jnp.zeros((nr, L), F32)
        for c in (range(nc - 1, -1, -1) if rev else range(nc)):
            rows = pl.ds(c, nr, stride=nc)
            m0_s[rows, :] = m
            m = bt_s[rows, :] + jnp.maximum(m, gm_s[rows, :])
            m1_s[rows, :] = m
        m0 = m0_s[...]
        m1 = m1_s[...]
        btot = bt_s[...]
        mx = jnp.maximum(m0, g)
        pieces = _split3(m0 - mx) + _split3(-(b + mx))
        for k, part in enumerate(pieces):
            tab_ref[d, k] = part.astype(F32)
        tab_ref[d, N_PIECES] = a - m0
        tab_ref[d, N_PIECES + 1] = jnp.exp(btot - b + ig - m1)
        tab_ref[d, N_PIECES + 2] = jnp.exp(btot + m0 - m1)


def _gate_prep(raw, bias):
    L = raw.shape[3]
    rows = raw.shape[2] // SUBLANES * LSTM_HEADS
    buf = lambda *lead: pltpu.VMEM((*lead, rows, L), F32)
    return pl.pallas_call(
        _gate_kernel,
        out_shape=jax.ShapeDtypeStruct((2, N_TAB, rows, L), F32),
        scratch_shapes=[buf(2), buf(), buf(), buf(), buf()],
        compiler_params=pltpu.CompilerParams(vmem_limit_bytes=VMEM_LIMIT),
        name="mlstm_gates",
    )(raw, bias)


MLSTM_LOOKAHEAD = 1


def _mlstm_kernel(lq_ref, lk_ref, lv_ref, tab_ref, sel_ref,
                  cw_q_ref, cw_k_ref, cb_q_ref, cb_k_ref, o_ref, xq_s, xk_s, q_s, kt_s, ktb_s, c_s):
    seq = lq_ref.shape[0]
    L = LSTM_CHUNK
    nc = seq // L

    zpad = jnp.zeros((CONV_PAD, LANES), F32)
    for x_s, src_ref in ((xq_s, lq_ref), (xk_s, lk_ref)):
        x_s[0:CONV_PAD, :] = zpad
        x_s[CONV_PAD + seq:CONV_PAD + seq + CONV_PAD, :] = zpad
        x_s[CONV_PAD:CONV_PAD + seq, :] = src_ref[...]

    def conv_silu(x_s, w_ref, b_ref, r0):
        acc = jnp.zeros((CONV_ROWS, LANES), F32) + b_ref[0]
        for j in range(CONV_K):
            start = CONV_PAD + r0 + j - CONV_K // 2
            acc = acc + x_s[start:start + CONV_ROWS, :] * w_ref[0, j:j + 1, :]
        return jax.nn.silu(acc)

    n_blocks = seq // CONV_ROWS
    for i in range(n_blocks // 2):
        for blk in (i, n_blocks - 1 - i):
            r0 = blk * CONV_ROWS
            q_s[r0:r0 + CONV_ROWS, :] = conv_silu(xq_s, cw_q_ref, cb_q_ref, r0)
            k = conv_silu(xk_s, cw_k_ref, cb_k_ref, r0) * (HEAD_DIM ** -0.5)
            for j in range(CONV_ROWS // L):
                kt = k[j * L:(j + 1) * L, :].T
                kt_s[r0 // L + j] = kt
                ktb_s[r0 // L + j] = kt.astype(BF16)

    row_i = lax.broadcasted_iota(jnp.int32, (L, L), 0)
    col_i = lax.broadcasted_iota(jnp.int32, (L, L), 1)
    masks = (col_i <= row_i, col_i >= row_i)
    ones = jnp.ones((L, LANES), BF16)
    pad_rows = jnp.zeros((P_ROWS - N_PIECES, L), F32)
    c_s[...] = jnp.zeros(c_s.shape, F32)

    def chunk_weights(d, c):
        rows = pl.ds(c * L, L)
        row = pl.ds(c, 1)
        vaug = jnp.concatenate([lv_ref[rows, :], ones], axis=1)
        qc = q_s[rows, :]
        qk = _dot(qc.astype(BF16), ktb_s[c])
        tab = [tab_ref[d, k, row, :] for k in range(N_TAB)]
        pieces = jnp.concatenate(tab[:N_PIECES] + [pad_rows], axis=0).astype(BF16)
        e = _dot_tn(pieces, sel_ref[...])
        dmx = e[:, :LANES]
        w = jnp.where(masks[d], jnp.exp(tab[N_PIECES] + dmx), 0.0)
        s = (qk * w).astype(BF16)
        qw = (qc * jnp.exp(dmx)).astype(BF16)
        kw = (kt_s[c] * tab[N_PIECES + 1]).astype(BF16)
        upd = _dot(kw, vaug)
        return jnp.concatenate([s, qw], axis=1), vaug, upd, jnp.exp(e[:, LANES:]), tab[N_PIECES + 2]

    def chunk_output(d, c, first, lhs, vaug, upd, floor, dec):
        rows = pl.ds(c * L, L)
        caug = c_s[d]
        r = _dot(lhs, jnp.concatenate([vaug, caug.astype(BF16)], axis=0))
        h = r[:, :LANES] / jnp.maximum(jnp.abs(r[:, LANES:]), floor)
        o_ref[rows, :] = h if first else o_ref[rows, :] + h
        c_s[d] = jnp.concatenate([dec, dec], axis=1) * caug + upd

    ready = []
    for i in range(nc + MLSTM_LOOKAHEAD):
        if i < nc:
            cf, cb = i, nc - 1 - i
            ready.append((cf, cb, chunk_weights(0, cf), chunk_weights(1, cb)))
        if i >= MLSTM_LOOKAHEAD:
            cf, cb, wf, wb = ready.pop(0)
            chunk_output(0, cf, cf < cb, *wf)
            chunk_output(1, cb, cb > cf, *wb)


def _mlstm(lq, lk, lv, tab, sel, cw, cb, layer, batch, seq):
    n = lq.shape[0]
    nh = LSTM_HEADS
    nc = seq // LSTM_CHUNK
    head_blk = lambda: pl.BlockSpec((seq, LANES), lambda b, h: (b, h))
    return pl.pallas_call(
        _mlstm_kernel,
        grid=(batch, nh),
        in_specs=[head_blk(), head_blk(), head_blk(),
                  pl.BlockSpec((2, N_TAB, nc, LSTM_CHUNK), lambda b, h: (0, 0, b * nh + h, 0)),
                  pl.BlockSpec((P_ROWS, N_COLQ * LANES), lambda b, h: (0, 0)),
                  pl.BlockSpec((None, 1, CONV_K, LANES), lambda b, h: (layer, h, 0, 0)),
                  pl.BlockSpec((None, 1, CONV_K, LANES), lambda b, h: (layer, nh + h, 0, 0)),
                  pl.BlockSpec((None, 1, 1, LANES), lambda b, h: (layer, h, 0, 0)),
                  pl.BlockSpec((None, 1, 1, LANES), lambda b, h: (layer, nh + h, 0, 0))],
        out_specs=head_blk(),
        out_shape=jax.ShapeDtypeStruct((n, BRANCH_W), F32),
        scratch_shapes=[pltpu.VMEM((seq + 2 * CONV_PAD, LANES), F32),
                        pltpu.VMEM((seq + 2 * CONV_PAD, LANES), F32),
                        pltpu.VMEM((seq, LANES), F32),
                        pltpu.VMEM((nc, HEAD_DIM, LSTM_CHUNK), F32),
                        pltpu.VMEM((nc, HEAD_DIM, LSTM_CHUNK), BF16),
                        pltpu.VMEM((2, HEAD_DIM, 2 * LANES), F32)],
        compiler_params=pltpu.CompilerParams(dimension_semantics=("arbitrary", "arbitrary"),
                                             vmem_limit_bytes=VMEM_LIMIT),
        name="mlstm",
    )(lq, lk, lv, tab, sel, cw, cw, cb, cb)


COMBINE_COL_CHUNK = 256


def _combine_kernel(x_ref, ya_ref, yb_ref, hc_ref, lo_ref, g_ref, hn_ref, wg_hbm, bg_ref, wb_hbm, wo_hbm, o_ref,
                    wg_ref, wb_ref, wo_ref, wg_stage, wb_stage, wo_stage, w_sem, *, layer):
    @pl.when(pl.program_id(0) == 0)
    def _():
        _cast_weight(wg_hbm.at[layer], wg_ref, wg_stage, w_sem.at[0])
        for i in range(N_BRANCH):
            _cast_weight(wb_hbm.at[layer, i], wb_ref.at[i], wb_stage, w_sem.at[1])
        _cast_weight(wo_hbm.at[layer], wo_ref, wo_stage, w_sem.at[2])

    x = x_ref[...]
    xn = _rms(x, g_ref[...]).astype(BF16)
    yc = jnp.concatenate(
        [_rms(hc_ref[:, h * LANES:(h + 1) * LANES], hn_ref[:, h * LANES:(h + 1) * LANES]) for h in range(LSTM_HEADS)],
        axis=1)
    yc = (yc * jax.nn.sigmoid(lo_ref[...])).astype(BF16)
    ys = (ya_ref[...], yb_ref[...], yc)
    acc = x
    for k0 in range(0, D_MODEL, COMBINE_COL_CHUNK):
        k1 = k0 + COMBINE_COL_CHUNK
        mix = None
        for i, y in enumerate(ys):
            c0 = i * D_MODEL + k0
            gate = jax.nn.sigmoid(_dot(xn, wg_ref[:, c0:c0 + COMBINE_COL_CHUNK]) + bg_ref[:, c0:c0 + COMBINE_COL_CHUNK])
            term = gate * _dot(y, wb_ref[i, :, k0:k1])
            mix = term if mix is None else mix + term
        acc = acc + _dot(mix.astype(BF16), wo_ref[k0:k1, :])
    o_ref[...] = acc


def _combine(x2, ya, yb, hc, lo, g, hn, wg, bg, wb, wo, layer):
    n = x2.shape[0]
    t = TOK_TILE
    tok = lambda w: pl.BlockSpec((t, w), lambda i: (i, 0))
    hbm = lambda: pl.BlockSpec(memory_space=pl.ANY)
    row = lambda w: pl.BlockSpec((None, 1, w), lambda i: (layer, 0, 0))
    stage = lambda k, n_: pltpu.VMEM((2, _stage_rows(k, n_), n_), F32)
    return pl.pallas_call(
        functools.partial(_combine_kernel, layer=layer),
        grid=(n // t,),
        in_specs=[tok(D_MODEL), tok(BRANCH_W), tok(BRANCH_W), tok(BRANCH_W), tok(BRANCH_W), row(D_MODEL),
                  row(BRANCH_W), hbm(), row(N_BRANCH * D_MODEL), hbm(), hbm()],
        out_specs=tok(D_MODEL),
        out_shape=jax.ShapeDtypeStruct((n, D_MODEL), F32),
        scratch_shapes=[pltpu.VMEM((D_MODEL, N_BRANCH * D_MODEL), BF16),
                        pltpu.VMEM((N_BRANCH, BRANCH_W, D_MODEL), BF16),
                        pltpu.VMEM((D_MODEL, D_MODEL), BF16),
                        stage(D_MODEL, N_BRANCH * D_MODEL), stage(BRANCH_W, D_MODEL), stage(D_MODEL, D_MODEL),
                        pltpu.SemaphoreType.DMA((3, 2))],
        compiler_params=pltpu.CompilerParams(dimension_semantics=("arbitrary",),
                                             vmem_limit_bytes=VMEM_LIMIT),
        name="combine",
    )(x2, ya, yb, hc, lo, g, hn, wg, bg, wb, wo)


FFN_COL_CHUNK = 256
FFN_STAGE_SLOTS = 3


def _ffn_kernel(x_ref, g_ref, wi_hbm, wo_hbm, o_ref, wi_ref, wo_ref, wi_stage, wo_stage, w_sem, *, layer):
    @pl.when(pl.program_id(0) == 0)
    def _():
        _cast_weight(wi_hbm.at[layer], wi_ref, wi_stage, w_sem.at[0])
        _cast_weight(wo_hbm.at[layer], wo_ref, wo_stage, w_sem.at[1])

    x = x_ref[...]
    xn = _rms(x, g_ref[...]).astype(BF16)
    acc = x
    for c0 in range(0, FFN_HIDDEN, FFN_COL_CHUNK):
        c1 = min(c0 + FFN_COL_CHUNK, FFN_HIDDEN)
        a = jax.nn.silu(_dot(xn, wi_ref[:, c0:c1])) * _dot(xn, wi_ref[:, FFN_HIDDEN + c0:FFN_HIDDEN + c1])
        acc = acc + _dot(a.astype(BF16), wo_ref[c0:c1, :])
    o_ref[...] = acc


def _ffn(x2, g, wi, wo, layer):
    n = x2.shape[0]
    t = TOK_TILE
    tok = lambda w: pl.BlockSpec((t, w), lambda i: (i, 0))
    hbm = lambda: pl.BlockSpec(memory_space=pl.ANY)
    stage = lambda k, n_: pltpu.VMEM((FFN_STAGE_SLOTS, _stage_rows(k, n_), n_), F32)
    return pl.pallas_call(
        functools.partial(_ffn_kernel, layer=layer),
        grid=(n // t,),
        in_specs=[tok(D_MODEL), pl.BlockSpec((None, 1, D_MODEL), lambda i: (layer, 0, 0)), hbm(), hbm()],
        out_specs=tok(D_MODEL),
        out_shape=jax.ShapeDtypeStruct((n, D_MODEL), F32),
        scratch_shapes=[pltpu.VMEM((D_MODEL, 2 * FFN_HIDDEN), BF16), pltpu.VMEM((FFN_HIDDEN, D_MODEL), BF16),
                        stage(D_MODEL, 2 * FFN_HIDDEN), stage(FFN_HIDDEN, D_MODEL),
                        pltpu.SemaphoreType.DMA((2, FFN_STAGE_SLOTS))],
        compiler_params=pltpu.CompilerParams(dimension_semantics=("arbitrary",),
                                             vmem_limit_bytes=VMEM_LIMIT),
        name="ffn",
    )(x2, g, wi, wo)


def _rope_tables(seq):
    rows = seq // GRID_W
    row = jnp.repeat(jnp.arange(rows, dtype=F32), GRID_W)
    col = jnp.tile(jnp.arange(GRID_W, dtype=F32), rows)
    axis_dim = HEAD_DIM // 2
    freqs = ROPE_THETA ** (-jnp.arange(axis_dim // 2, dtype=F32) * 2.0 / axis_dim)
    ang = jnp.concatenate([row[:, None] * freqs[None], col[:, None] * freqs[None]], axis=-1)
    cos, sin = jnp.cos(ang), jnp.sin(ang)
    cos_full = jnp.repeat(cos, 2, axis=-1)
    sin_signed = jnp.stack([-sin, sin], axis=-1).reshape(seq, HEAD_DIM)
    return cos_full, sin_signed


def _sel_matrix():
    sel = np.zeros((P_ROWS, N_COLQ * LANES), np.float32)
    for q in range(N_COLQ):
        sel[q * N_SPLIT:(q + 1) * N_SPLIT, q * LANES:(q + 1) * LANES] = 1.0
    return jnp.asarray(sel, BF16)


def kernel(x, norm_mix, w_in, sgu_ln_g, sgu_ln_b, sgu_w, sgu_b, q_norm, k_norm, conv_w, conv_b, igate_b, fgate_b,
           lstm_norm, w_gate, b_gate, w_branch, w_out, norm_ffn, w_ffn_in, w_ffn_out):
    batch, seq, d = x.shape
    depth = norm_mix.shape[0]
    nh = LSTM_HEADS
    L = LSTM_CHUNK
    nc = seq // L
    cos, sin = _rope_tables(seq)
    sel = _sel_matrix()
    x2 = x.reshape(batch * seq, d)
    rows = lambda p: p[:, None, :]
    w_in_b = w_in.astype(BF16)
    w_g = jnp.swapaxes(w_in_b[:, :, N_MAIN:], 1, 2).reshape(depth, 4, nh, d)
    w_g = jnp.pad(w_g, ((0, 0), (0, 0), (0, SUBLANES - nh), (0, 0))).reshape(depth, 4 * SUBLANES, d)
    sgub = jnp.repeat(jnp.swapaxes(sgu_b, 1, 2), LANES, axis=2)
    sguw = sgu_w.astype(BF16)
    gbias = jnp.stack([igate_b, fgate_b], axis=1)
    gbias = jnp.broadcast_to(gbias[:, :, :, None, :, None], (depth, 2, 2, batch, nh, L))
    gbias = gbias.reshape(depth, 2, 2, batch * nh, L)
    cw = conv_w.reshape(depth, CONV_K, 2 * nh, LANES).transpose(0, 2, 1, 3)
    cb = conv_b.reshape(depth, 2 * nh, 1, LANES)
    for l in range(depth):
        ya, qa, ka, va, lq, lk, lv, lo, graw = _inproj(
            x2, rows(norm_mix), w_in_b, l, w_g, rows(sgu_ln_g), rows(sgu_ln_b),
            sguw, sgub, rows(q_norm), rows(k_norm), cos, sin, seq)
        yb = _attention(qa, ka, va, batch, seq)

        tab = _gate_prep(graw.reshape(2, 2, nc * batch * SUBLANES, L), gbias[l])
        hc = _mlstm(lq, lk, lv, tab, sel, cw, cb, l, batch, seq)

        x2 = _combine(x2, ya, yb, hc, lo, rows(norm_mix), rows(lstm_norm), w_gate,
                      rows(b_gate), w_branch, w_out, l)
        x2 = _ffn(x2, rows(norm_ffn), w_ffn_in, w_ffn_out, l)
    return x2.reshape(batch, seq, d)
```
